```python
import jax
import jax.numpy as jnp
from jax import lax
import numpy as np

D_MODEL = 1024
BATCH = 8
SEQ = 2048
DEPTH = 1
DEC_BATCH = 128
DEC_SEQ = 1
PAST_LEN = 16384
PAGE_SIZE = 128

GDN_HEADS = 4
GDN_DK = 128
GDN_DV = 128
CONV_W = 4
CHUNK = 64
MLA_HEADS = 4
Q_LORA = 384
KV_LORA = 256
QK_NOPE = 128
QK_ROPE = 64
V_HEAD = 128
ROPE_THETA = 10000.0
Q_BLOCK = 128
MEM_TOKENS = 256
MEM_HEADS = 4
MEM_HD = 128
D_FF = -(-8 * D_MODEL // (3 * 256)) * 256
EPS = 1e-6

GDN_QK = GDN_HEADS * GDN_DK
GDN_V = GDN_HEADS * GDN_DV
CONV_DIM = 2 * GDN_QK + GDN_V
MLA_QDIM = MLA_HEADS * (QK_NOPE + QK_ROPE)
MLA_KVDIM = MLA_HEADS * (QK_NOPE + V_HEAD)
MLA_SCALE = (QK_NOPE + QK_ROPE) ** -0.5
MEM_DIM = MEM_HEADS * MEM_HD
MIX_OUT = GDN_V + MLA_HEADS * V_HEAD
IN_SPLITS = (CONV_DIM, CONV_DIM + GDN_V, CONV_DIM + GDN_V + GDN_HEADS,
             CONV_DIM + GDN_V + 2 * GDN_HEADS, CONV_DIM + GDN_V + 2 * GDN_HEADS + Q_LORA)
IN_DIM = IN_SPLITS[-1] + KV_LORA + QK_ROPE

kernel_name = 'hybrid_gdn_mla_mem_decoder_step'

F32 = jnp.float32


def rms_norm(x, g):
    xf = x.astype(F32)
    y = xf * lax.rsqrt(jnp.mean(xf * xf, axis=-1, keepdims=True) + EPS)
    return (y * g.astype(F32)).astype(x.dtype)


def l2_norm(x):
    xf = x.astype(F32)
    return xf * lax.rsqrt(jnp.sum(xf * xf, axis=-1, keepdims=True) + EPS)


def rope(x, pos):
    half = QK_ROPE // 2
    inv = ROPE_THETA ** (-jnp.arange(half, dtype=F32) / half)
    ang = pos.astype(F32)[:, None] * inv[None, :]
    cos = jnp.cos(ang)[None, :, None, :]
    sin = jnp.sin(ang)[None, :, None, :]
    xf = x.astype(F32)
    x1, x2 = xf[..., :half], xf[..., half:]
    return jnp.concatenate([x1 * cos - x2 * sin, x2 * cos + x1 * sin], axis=-1).astype(x.dtype)


def gdn_features(qkv_ext, b, a, p):
    t = qkv_ext.shape[1] - (CONV_W - 1)
    w = p['gdn_conv_w']
    conv = sum(qkv_ext[:, i:i + t] * w[i] for i in range(CONV_W))
    conv = jax.nn.silu(conv.astype(F32))
    bsz = conv.shape[0]
    q = conv[..., :GDN_QK].reshape(bsz, t, GDN_HEADS, GDN_DK)
    k = conv[..., GDN_QK:2 * GDN_QK].reshape(bsz, t, GDN_HEADS, GDN_DK)
    v = conv[..., 2 * GDN_QK:].reshape(bsz, t, GDN_HEADS, GDN_DV)
    q = l2_norm(q) * (GDN_DK ** -0.5)
    k = l2_norm(k)
    beta = jax.nn.sigmoid(b.astype(F32))
    g = -jnp.exp(p['gdn_A_log'].astype(F32)) * jax.nn.softplus(a.astype(F32) + p['gdn_dt_bias'].astype(F32))
    return q, k, v, g, beta


def gdn_chunked(q, k, v, g, beta, state):
    bsz, t, h, _ = q.shape
    dv = v.shape[-1]
    n = t // CHUNK

    def blocks(a):
        return jnp.moveaxis(a.reshape((bsz, n, CHUNK, h) + a.shape[3:]), 3, 1)

    qc, kc, vc, bc = blocks(q), blocks(k), blocks(v), blocks(beta)
    gc = jnp.cumsum(blocks(g), axis=-1)
    lower = jnp.tril(jnp.ones((CHUNK, CHUNK), bool))
    strict = jnp.tril(jnp.ones((CHUNK, CHUNK), bool), -1)
    diff = gc[..., :, None] - gc[..., None, :]
    decay = jnp.where(lower, jnp.exp(jnp.where(lower, diff, 0.0)), 0.0)
    kb = kc * bc[..., None]
    a_mat = jnp.where(strict, jnp.einsum('bhnik,bhnjk->bhnij', kb, kc) * decay, 0.0)
    eye = jnp.eye(CHUNK, dtype=F32)
    t_mat = lax.linalg.triangular_solve(eye + a_mat, jnp.broadcast_to(eye, a_mat.shape),
                                        left_side=True, lower=True)
    u = jnp.einsum('bhnij,bhnjv->bhniv', t_mat, vc * bc[..., None])
    w = jnp.einsum('bhnij,bhnjk->bhnik', t_mat, kb * jnp.exp(gc)[..., None])
    attn = jnp.where(lower, jnp.einsum('bhnik,bhnjk->bhnij', qc, kc) * decay, 0.0)
    q_dec = qc * jnp.exp(gc)[..., None]
    g_last = gc[..., -1]
    k_tail = kc * jnp.exp(g_last[..., None] - gc)[..., None]

    def step(s, inp):
        q_i, k_i, u_i, w_i, a_i, gl = inp
        v_new = u_i - jnp.einsum('bhck,bhkv->bhcv', w_i, s)
        o = jnp.einsum('bhck,bhkv->bhcv', q_i, s) + jnp.einsum('bhcj,bhjv->bhcv', a_i, v_new)
        s = s * jnp.exp(gl)[..., None, None] + jnp.einsum('bhck,bhcv->bhkv', k_i, v_new)
        return s, o

    xs = tuple(jnp.moveaxis(a, 2, 0) for a in (q_dec, k_tail, u, w, attn, g_last))
    state, o = lax.scan(step, state, xs)
    o = jnp.transpose(o, (1, 0, 3, 2, 4)).reshape(bsz, t, h, dv)
    return o, state


def gdn_recurrent(q, k, v, g, beta, state):
    def step(s, inp):
        q_t, k_t, v_t, g_t, b_t = inp
        s = s * jnp.exp(g_t)[..., None, None]
        kv = jnp.einsum('bhk,bhkv->bhv', k_t, s)
        s = s + jnp.einsum('bhk,bhv->bhkv', k_t, (v_t - kv) * b_t[..., None])
        return s, jnp.einsum('bhk,bhkv->bhv', q_t, s)

    xs = tuple(jnp.moveaxis(a, 1, 0) for a in (q, k, v, g, beta))
    state, o = lax.scan(step, state, xs)
    return jnp.moveaxis(o, 0, 1), state


def gdn_output(o, gate_z, p):
    bsz, t = o.shape[:2]
    z = gate_z.reshape(bsz, t, GDN_HEADS, GDN_DV).astype(F32)
    y = rms_norm(o, p['gdn_out_norm_g']) * jax.nn.silu(z)
    return y.reshape(bsz, t, GDN_V)


def mla_queries(q_a, pos, p):
    bsz, t = q_a.shape[:2]
    q = (rms_norm(q_a, p['mla_q_a_norm_g']) @ p['mla_w_q_b']).reshape(bsz, t, MLA_HEADS, QK_NOPE + QK_ROPE)
    q_nope = rms_norm(q[..., :QK_NOPE], p['mla_qn_nope_g'])
    q_pe = rope(rms_norm(q[..., QK_NOPE:], p['mla_qn_rope_g']), pos)
    return q_nope, q_pe


def mla_latent(kv_a, pos, p):
    c = rms_norm(kv_a[..., :KV_LORA], p['mla_kv_a_norm_g'])
    k_pe = rope(rms_norm(kv_a[..., KV_LORA:], p['mla_kn_rope_g'])[:, :, None, :], pos)[:, :, 0, :]
    return c, k_pe


def mla_up(p):
    return p['mla_w_kv_b'].reshape(KV_LORA, MLA_HEADS, QK_NOPE + V_HEAD)


def mla_keys(c, p):
    k_nope = jnp.einsum('bsc,chd->bshd', c, mla_up(p)[..., :QK_NOPE])
    return rms_norm(k_nope, p['mla_kn_nope_g'])


def mla_scores(q_nope, q_pe, k_nope, k_pe):
    s = jnp.einsum('bqhd,bkhd->bhqk', q_nope, k_nope).astype(F32) \
        + jnp.einsum('bqhr,bkr->bhqk', q_pe, k_pe).astype(F32)
    return s * MLA_SCALE


def mla_prompt_attention(q_nope, q_pe, c, k_pe, p):
    bsz, s_len = c.shape[:2]
    k_nope = mla_keys(c, p)
    v = jnp.einsum('bsc,chd->bshd', c, mla_up(p)[..., QK_NOPE:])
    nqb = s_len // Q_BLOCK
    qn_b = jnp.swapaxes(q_nope.reshape(bsz, nqb, Q_BLOCK, MLA_HEADS, QK_NOPE), 0, 1)
    qp_b = jnp.swapaxes(q_pe.reshape(bsz, nqb, Q_BLOCK, MLA_HEADS, QK_ROPE), 0, 1)
    key_pos = jnp.arange(s_len)

    def block(args):
        i, qn, qp = args
        s = mla_scores(qn, qp, k_nope, k_pe)
        q_pos = i * Q_BLOCK + jnp.arange(Q_BLOCK)
        s = jnp.where(key_pos[None, :] <= q_pos[:, None], s, -jnp.inf)
        pr = jax.nn.softmax(s, axis=-1).astype(v.dtype)
        return jnp.einsum('bhqk,bkhd->bqhd', pr, v)

    o = lax.map(block, (jnp.arange(nqb), qn_b, qp_b))
    return jnp.swapaxes(o, 0, 1).reshape(bsz, s_len, MLA_HEADS * V_HEAD)


def mla_sample_attention(q_nope, q_pe, c_new, kpe_new, ckv_pool, kpe_pool, page_table, p):
    db, ds = q_nope.shape[:2]

    def update(carry, s, c_blk):
        m, l, acc = carry
        m_new = jnp.maximum(m, jnp.max(s, axis=-1))
        pr = jnp.exp(s - m_new[..., None])
        corr = jnp.exp(m - m_new)
        l = l * corr + jnp.sum(pr, axis=-1)
        acc = acc * corr[..., None] + jnp.einsum('bhqk,bkc->bhqc', pr, c_blk.astype(F32))
        return (m_new, l, acc)

    def page_step(carry, phys):
        c_blk = ckv_pool[phys]
        kp_blk = kpe_pool[phys]
        s = mla_scores(q_nope, q_pe, mla_keys(c_blk, p), kp_blk)
        return update(carry, s, c_blk), None

    init = (jnp.full((db, MLA_HEADS, ds), -jnp.inf, F32),
            jnp.zeros((db, MLA_HEADS, ds), F32),
            jnp.zeros((db, MLA_HEADS, ds, KV_LORA), F32))
    carry, _ = lax.scan(page_step, init, page_table.T)
    s = mla_scores(q_nope, q_pe, mla_keys(c_new, p), kpe_new)
    s = jnp.where(jnp.tril(jnp.ones((ds, ds), bool)), s, -jnp.inf)
    _, l, acc = update(carry, s, c_new)
    lat = acc / l[..., None]
    o = jnp.einsum('bhqc,chd->bqhd', lat, mla_up(p)[..., QK_NOPE:].astype(F32))
    return o.reshape(db, ds, MLA_HEADS * V_HEAD)


def mem_kv(mem, p):
    bsz, m = mem.shape[:2]
    k = rms_norm((mem @ p['mem_wk']).reshape(bsz, m, MEM_HEADS, MEM_HD), p['mem_kn_g'])
    v = (mem @ p['mem_wv']).reshape(bsz, m, MEM_HEADS, MEM_HD)
    return k, v


def mem_attend(h, k, v, p):
    bsz, t = h.shape[:2]
    q = rms_norm((h @ p['mem_wq']).reshape(bsz, t, MEM_HEADS, MEM_HD), p['mem_qn_g'])
    s = jnp.einsum('bthd,bmhd->bhtm', q, k).astype(F32) * (MEM_HD ** -0.5)
    pr = jax.nn.softmax(s, axis=-1).astype(v.dtype)
    o = jnp.einsum('bhtm,bmhd->bthd', pr, v).reshape(bsz, t, MEM_DIM)
    return o @ p['mem_wo']


def swiglu(h, p):
    return (jax.nn.silu(h @ p['ffn_w_gate']) * (h @ p['ffn_w_up'])) @ p['ffn_w_down']


def finish_layer(x, y_gdn, y_mla, mk, mv, p):
    x = x + jnp.concatenate([y_gdn.astype(x.dtype), y_mla.astype(x.dtype)], axis=-1) @ p['w_mix_out']
    x = x + mem_attend(rms_norm(x, p['norm_mem_g']), mk, mv, p)
    return x + swiglu(rms_norm(x, p['norm_ffn_g']), p)


def prompt_layer(x, mem, p):
    bsz, s_len = x.shape[:2]
    pos = jnp.arange(s_len)
    h = rms_norm(x, p['norm_mix_g'])
    qkv, gz, b, a, q_a, kv_a = jnp.split(h @ p['w_in'], IN_SPLITS, axis=-1)
    qkv_ext = jnp.pad(qkv, ((0, 0), (CONV_W - 1, 0), (0, 0)))
    q, k, v, g, beta = gdn_features(qkv_ext, b, a, p)
    s0 = jnp.zeros((bsz, GDN_HEADS, GDN_DK, GDN_DV), F32)
    o_gdn, s_fin = gdn_chunked(q, k, v, g, beta, s0)
    y_gdn = gdn_output(o_gdn, gz, p)
    conv_state = qkv_ext[:, s_len:]
    q_nope, q_pe = mla_queries(q_a, pos, p)
    c, k_pe = mla_latent(kv_a, pos, p)
    y_mla = mla_prompt_attention(q_nope, q_pe, c, k_pe, p)
    mk, mv = mem_kv(mem, p)
    x = finish_layer(x, y_gdn, y_mla, mk, mv, p)
    return x, (c, k_pe, mk, mv, s_fin, conv_state)


def sample_layer(x, ckv_pool, kpe_pool, page_table, mk, mv, s_prev, conv_prev, p):
    ds = x.shape[1]
    pos = PAST_LEN + jnp.arange(ds)
    h = rms_norm(x, p['norm_mix_g'])
    qkv, gz, b, a, q_a, kv_a = jnp.split(h @ p['w_in'], IN_SPLITS, axis=-1)
    qkv_ext = jnp.concatenate([conv_prev.astype(qkv.dtype), qkv], axis=1)
    q, k, v, g, beta = gdn_features(qkv_ext, b, a, p)
    o_gdn, s_new = gdn_recurrent(q, k, v, g, beta, s_prev.astype(F32))
    y_gdn = gdn_output(o_gdn, gz, p)
    conv_new = qkv_ext[:, ds:]
    q_nope, q_pe = mla_queries(q_a, pos, p)
    c, k_pe = mla_latent(kv_a, pos, p)
    y_mla = mla_sample_attention(q_nope, q_pe, c, k_pe, ckv_pool, kpe_pool, page_table, p)
    x = finish_layer(x, y_gdn, y_mla, mk, mv, p)
    return x, (c, k_pe, s_new, conv_new)


def setup_inputs(seed: int = 0) -> dict:
    key = jax.random.key(seed)
    ks = iter(jax.random.split(key, 64))
    n_pages = PAST_LEN // PAGE_SIZE
    n_used = DEC_BATCH * n_pages
    n_pool = n_used + n_used // 4

    def normal(shape, scale=1.0):
        return jax.random.normal(next(ks), shape, F32) * scale

    def gain(n):
        return 1.0 + normal((DEPTH, n), 0.02)

    inp = {}
    inp['x_prompt'] = normal((BATCH, SEQ, D_MODEL))
    inp['x_sample'] = normal((DEC_BATCH, DEC_SEQ, D_MODEL))
    inp['cache_mla_ckv'] = normal((DEPTH, n_pool, PAGE_SIZE, KV_LORA))
    inp['cache_mla_kpe'] = normal((DEPTH, n_pool, PAGE_SIZE, QK_ROPE))
    inp['cache_mem_k'] = normal((DEPTH, DEC_BATCH, MEM_TOKENS, MEM_HEADS, MEM_HD))
    inp['cache_mem_v'] = normal((DEPTH, DEC_BATCH, MEM_TOKENS, MEM_HEADS, MEM_HD))
    inp['state_gdn_S'] = normal((DEPTH, DEC_BATCH, GDN_HEADS, GDN_DK, GDN_DV), 0.05)
    inp['state_gdn_conv'] = normal((DEPTH, DEC_BATCH, CONV_W - 1, CONV_DIM))
    perm = jax.random.permutation(next(ks), n_pool)[:n_used]
    inp['page_table'] = perm.reshape(DEC_BATCH, n_pages).astype(jnp.int32)
    inp['mem_prompt'] = normal((BATCH, MEM_TOKENS, D_MODEL))
    inp['norm_mix_g'] = gain(D_MODEL)
    inp['w_in'] = normal((DEPTH, D_MODEL, IN_DIM), D_MODEL ** -0.5)
    inp['gdn_conv_w'] = normal((DEPTH, CONV_W, CONV_DIM), CONV_W ** -0.5)
    inp['gdn_A_log'] = jnp.log(jax.random.uniform(next(ks), (DEPTH, GDN_HEADS), F32, 1.0, 16.0))
    dt = jnp.exp(jax.random.uniform(next(ks), (DEPTH, GDN_HEADS), F32, np.log(1e-3), np.log(1e-1)))
    inp['gdn_dt_bias'] = dt + jnp.log(-jnp.expm1(-dt))
    inp['gdn_out_norm_g'] = gain(GDN_DV)
    inp['mla_q_a_norm_g'] = gain(Q_LORA)
    inp['mla_w_q_b'] = normal((DEPTH, Q_LORA, MLA_QDIM), Q_LORA ** -0.5)
    inp['mla_kv_a_norm_g'] = gain(KV_LORA)
    inp['mla_w_kv_b'] = normal((DEPTH, KV_LORA, MLA_KVDIM), KV_LORA ** -0.5)
    inp['mla_qn_nope_g'] = gain(QK_NOPE)
    inp['mla_qn_rope_g'] = gain(QK_ROPE)
    inp['mla_kn_nope_g'] = gain(QK_NOPE)
    inp['mla_kn_rope_g'] = gain(QK_ROPE)
    inp['w_mix_out'] = normal((DEPTH, MIX_OUT, D_MODEL), MIX_OUT ** -0.5)
    inp['norm_mem_g'] = gain(D_MODEL)
    inp['mem_wq'] = normal((DEPTH, D_MODEL, MEM_DIM), D_MODEL ** -0.5)
    inp['mem_wk'] = normal((DEPTH, D_MODEL, MEM_DIM), D_MODEL ** -0.5)
    inp['mem_wv'] = normal((DEPTH, D_MODEL, MEM_DIM), D_MODEL ** -0.5)
    inp['mem_wo'] = normal((DEPTH, MEM_DIM, D_MODEL), MEM_DIM ** -0.5)
    inp['mem_qn_g'] = gain(MEM_HD)
    inp['mem_kn_g'] = gain(MEM_HD)
    inp['norm_ffn_g'] = gain(D_MODEL)
    inp['ffn_w_gate'] = normal((DEPTH, D_MODEL, D_FF), D_MODEL ** -0.5)
    inp['ffn_w_up'] = normal((DEPTH, D_MODEL, D_FF), D_MODEL ** -0.5)
    inp['ffn_w_down'] = normal((DEPTH, D_FF, D_MODEL), D_FF ** -0.5)
    return inp


def reference(x_prompt, x_sample, cache_mla_ckv, cache_mla_kpe, cache_mem_k, cache_mem_v,
              state_gdn_S, state_gdn_conv, page_table, mem_prompt,
              norm_mix_g, w_in, gdn_conv_w, gdn_A_log, gdn_dt_bias, gdn_out_norm_g,
              mla_q_a_norm_g, mla_w_q_b, mla_kv_a_norm_g, mla_w_kv_b,
              mla_qn_nope_g, mla_qn_rope_g, mla_kn_nope_g, mla_kn_rope_g, w_mix_out,
              norm_mem_g, mem_wq, mem_wk, mem_wv, mem_wo, mem_qn_g, mem_kn_g,
              norm_ffn_g, ffn_w_gate, ffn_w_up, ffn_w_down):
    xp, xs = x_prompt, x_sample
    st_p, st_s = [], []
    for l in range(DEPTH):
        p = dict(norm_mix_g=norm_mix_g[l], w_in=w_in[l], gdn_conv_w=gdn_conv_w[l],
                 gdn_A_log=gdn_A_log[l], gdn_dt_bias=gdn_dt_bias[l], gdn_out_norm_g=gdn_out_norm_g[l],
                 mla_q_a_norm_g=mla_q_a_norm_g[l], mla_w_q_b=mla_w_q_b[l],
                 mla_kv_a_norm_g=mla_kv_a_norm_g[l], mla_w_kv_b=mla_w_kv_b[l],
                 mla_qn_nope_g=mla_qn_nope_g[l], mla_qn_rope_g=mla_qn_rope_g[l],
                 mla_kn_nope_g=mla_kn_nope_g[l], mla_kn_rope_g=mla_kn_rope_g[l],
                 w_mix_out=w_mix_out[l], norm_mem_g=norm_mem_g[l], mem_wq=mem_wq[l],
                 mem_wk=mem_wk[l], mem_wv=mem_wv[l], mem_wo=mem_wo[l], mem_qn_g=mem_qn_g[l],
                 mem_kn_g=mem_kn_g[l], norm_ffn_g=norm_ffn_g[l], ffn_w_gate=ffn_w_gate[l],
                 ffn_w_up=ffn_w_up[l], ffn_w_down=ffn_w_down[l])
        xp, sp = prompt_layer(xp, mem_prompt, p)
        xs, ss = sample_layer(xs, cache_mla_ckv[l], cache_mla_kpe[l], page_table,
                              cache_mem_k[l], cache_mem_v[l], state_gdn_S[l], state_gdn_conv[l], p)
        st_p.append(sp)
        st_s.append(ss)
    p_ckv = jnp.stack([s[0] for s in st_p])
    p_kpe = jnp.stack([s[1] for s in st_p])
    p_mem_k = jnp.stack([s[2] for s in st_p])
    p_mem_v = jnp.stack([s[3] for s in st_p])
    p_S = jnp.stack([s[4] for s in st_p])
    p_conv = jnp.stack([s[5] for s in st_p])
    s_ckv = jnp.stack([s[0] for s in st_s])
    s_kpe = jnp.stack([s[1] for s in st_s])
    s_S = jnp.stack([s[2] for s in st_s])
    s_conv = jnp.stack([s[3] for s in st_s])
    return (xp, xs, p_ckv, p_kpe, p_mem_k, p_mem_v, p_S, p_conv, s_ckv, s_kpe, s_S, s_conv)
```

```python
import functools

import jax
import jax.numpy as jnp
from jax import lax
from jax.experimental import pallas as pl
from jax.experimental.pallas import tpu as pltpu

F32 = jnp.float32
BF16 = jnp.bfloat16

D_MODEL = 1024
GDN_HEADS = 4
GDN_DK = 128
GDN_DV = 128
CONV_W = 4
CHUNK = 64
MLA_HEADS = 4
Q_LORA = 384
KV_LORA = 256
QK_NOPE = 128
QK_ROPE = 64
V_HEAD = 128
ROPE_THETA = 10000.0
PAGE_SIZE = 128
MEM_HEADS = 4
MEM_HD = 128
EPS = 1e-6
GDN_QK = GDN_HEADS * GDN_DK
GDN_V = GDN_HEADS * GDN_DV
CONV_DIM = 2 * GDN_QK + GDN_V
MLA_SCALE = (QK_NOPE + QK_ROPE) ** -0.5
MEM_DIM = MEM_HEADS * MEM_HD

LANE = 128
SUBLANE = 8
BF16_ROWS = 16
QK_PAD = 2 * LANE
MISC_B = QK_ROPE
MISC_A = QK_ROPE + GDN_HEADS
VMEM_LIMIT = 56 * 1024 * 1024

GDN_TB = 2 * CHUNK
PAGES_PER_STEP = 8
KEY_SUB = 2 * PAGE_SIZE


def _cparams(sem):
    return pltpu.CompilerParams(dimension_semantics=sem, vmem_limit_bytes=VMEM_LIMIT)


def _full(shape):
    n = len(shape)
    return pl.BlockSpec(shape, lambda *_: (0,) * n)


def _dot(a, b):
    return jnp.dot(a, b, preferred_element_type=F32)


def _dot_nt(a, b):
    return lax.dot_general(a, b, (((1,), (1,)), ((), ())), preferred_element_type=F32)


def _dot_tn(a, b):
    return lax.dot_general(a, b, (((0,), (0,)), ((), ())), preferred_element_type=F32)


def _split3(x):
    hi = x.astype(BF16)
    r = x - hi.astype(F32)
    mid = r.astype(BF16)
    lo = (r - mid.astype(F32)).astype(BF16)
    return hi, mid, lo


def _dot01_left(m01, x):
    hi, mid, lo = _split3(x)
    return _dot(m01, hi) + _dot(m01, mid) + _dot(m01, lo)


def _dot01_right(x, m01):
    hi, mid, lo = _split3(x)
    return _dot(hi, m01) + _dot(mid, m01) + _dot(lo, m01)


def _rms(x, g, n=None):
    n = x.shape[-1] if n is None else n
    ms = jnp.sum(x * x, axis=-1, keepdims=True) * (1.0 / n)
    return x * lax.rsqrt(ms + EPS) * g


def _silu(x):
    return x * jax.nn.sigmoid(x)


def _softplus(x):
    return jnp.maximum(x, 0.0) + jnp.log1p(jnp.exp(-jnp.abs(x)))


def _iota(shape, dim):
    return lax.broadcasted_iota(jnp.int32, shape, dim)


def _block_id(i, size):
    assert size & (size - 1) == 0
    return lax.shift_right_logical(i, size.bit_length() - 1)


def _in_proj_body(x_ref, g_ref, w_ref, qkv_ref, gz_ref, qa_ref, kvc_ref, misc_ref):
    h = _rms(x_ref[...], g_ref[...]).astype(BF16)
    off = 0
    for ref in (qkv_ref, gz_ref, qa_ref, kvc_ref, misc_ref):
        n = ref.shape[-1]
        ref[...] = _dot(h, w_ref[:, off:off + n])
        off += n


def _in_proj(x, g, w, tm):
    t = x.shape[0]
    widths = (CONV_DIM, GDN_V, Q_LORA, KV_LORA, LANE)
    return pl.pallas_call(
        _in_proj_body,
        grid=(t // tm,),
        in_specs=[pl.BlockSpec((tm, D_MODEL), lambda i: (i, 0)), _full(g.shape), _full(w.shape)],
        out_specs=[pl.BlockSpec((tm, n), lambda i: (i, 0)) for n in widths],
        out_shape=[jax.ShapeDtypeStruct((t, n), F32) for n in widths],
        compiler_params=_cparams(("parallel",)),
        name="in_proj",
    )(x, g, w)


def _gate_lane_values(misc, lanep):
    lane = _iota(misc.shape, 1)
    is_a = (lane >= MISC_A) & (lane < MISC_A + GDN_HEADS)
    beta_l = jax.nn.sigmoid(misc)
    g_l = -jnp.exp(lanep[0:1, :]) * _softplus(jnp.where(is_a, misc, 0.0) + lanep[1:2, :])
    return beta_l, g_l


def _expand_matrix(first_lane):
    r = _iota((LANE, GDN_V), 0)
    c = _iota((LANE, GDN_V), 1)
    return jnp.where(r == first_lane + _block_id(c, GDN_DV), 1.0, 0.0).astype(BF16)


def _gdn_gate_out(o, z, gout):
    return _rms(o, gout) * _silu(z)


def _tri_inverse(a, eye, same16, same32):
    def mm(x, y):
        return _dot(x.astype(BF16), y.astype(BF16))

    ad = jnp.where(same16, a, 0.0)
    t = eye - ad
    p = mm(ad, ad)
    t = t + mm(t, p)
    p = mm(p, p)
    t = t + mm(t, p)
    p = mm(p, p)
    t = t + mm(t, p)
    b1 = jnp.where(same32, a, 0.0) - ad
    t = t - mm(t, mm(b1, t))
    b2 = jnp.where(same32, 0.0, a)
    t = t - mm(t, mm(b2, t))
    return t


def _gdn_prompt_body(qkv_ref, misc_ref, grow_ref, gz_ref, convw_ref, lanep_ref, rowp_ref, gout_ref,
                     y_ref, s_out_ref, xe_ref, s_ref):
    tb = GDN_TB
    t_idx = pl.program_id(1)

    @pl.when(t_idx == 0)
    def _():
        s_ref[...] = jnp.zeros(s_ref.shape, F32)
        xe_ref[0:SUBLANE, :] = jnp.zeros((SUBLANE, CONV_DIM), F32)

    x = qkv_ref[0]
    xe_ref[SUBLANE:SUBLANE + tb, :] = x
    w = convw_ref[...]
    conv = x * w[CONV_W - 1:CONV_W, :]
    for i in range(CONV_W - 1):
        lo = SUBLANE - (CONV_W - 1) + i
        conv = conv + xe_ref[lo:lo + tb, :] * w[i:i + 1, :]
    xe_ref[0:SUBLANE, :] = x[tb - SUBLANE:tb, :]

    ri = _iota((tb, tb), 0)
    ci = _iota((tb, tb), 1)
    same64 = _block_id(ri, CHUNK) == _block_id(ci, CHUNK)
    lower = same64 & (ci <= ri)
    strict = same64 & (ci < ri)
    same16 = _block_id(ri, 16) == _block_id(ci, 16)
    same32 = _block_id(ri, 32) == _block_id(ci, 32)
    eye = jnp.where(ri == ci, 1.0, 0.0)
    l_incl = jnp.where(lower, 1.0, 0.0).astype(BF16)
    u_incl = jnp.where(same64 & (ri <= ci), 1.0, 0.0).astype(BF16)
    u_strict = jnp.where(same64 & (ci > ri), 1.0, 0.0).astype(BF16)

    beta_l, g_l = _gate_lane_values(misc_ref[0], lanep_ref[...])
    e_b = _expand_matrix(MISC_B)
    e_a = _expand_matrix(MISC_A)
    beta_bc = _dot01_right(beta_l, e_b)
    gc_bc = _dot01_right(_dot01_left(l_incl, g_l), e_a)
    gt_bc = _dot01_right(_dot01_left(u_strict, g_l), e_a)
    egc = jnp.exp(gc_bc)
    etail = jnp.exp(gt_bc)

    grow = grow_ref[0]
    rowi = _iota(grow.shape, 0)
    g_r = -jnp.exp(rowp_ref[0:SUBLANE, :]) * _softplus(
        jnp.where(rowi >= GDN_HEADS, grow, 0.0) + rowp_ref[SUBLANE:2 * SUBLANE, :])
    gcr = _dot01_right(g_r, u_incl)

    gout = gout_ref[...]
    gz = gz_ref[0]
    for h in range(GDN_HEADS):
        sl = slice(h * GDN_DK, (h + 1) * GDN_DK)
        qh = _silu(conv[:, h * GDN_DK:(h + 1) * GDN_DK])
        kh = _silu(conv[:, GDN_QK + h * GDN_DK:GDN_QK + (h + 1) * GDN_DK])
        vh = _silu(conv[:, 2 * GDN_QK + h * GDN_DV:2 * GDN_QK + (h + 1) * GDN_DV])
        qh = qh * lax.rsqrt(jnp.sum(qh * qh, axis=-1, keepdims=True) + EPS) * (GDN_DK ** -0.5)
        kh = kh * lax.rsqrt(jnp.sum(kh * kh, axis=-1, keepdims=True) + EPS)
        bb = beta_bc[:, sl]
        kb = kh * bb
        vb = vh * bb
        diff = gc_bc[:, sl] - gcr[GDN_HEADS + h:GDN_HEADS + h + 1, :]
        dm = jnp.where(lower, jnp.exp(jnp.where(lower, diff, 0.0)), 0.0)
        p = _dot_nt(jnp.concatenate([kb, qh], axis=0).astype(BF16), kh.astype(BF16))
        a_mat = jnp.where(strict, p[:tb] * dm, 0.0)
        attn = p[tb:] * dm
        t_mat = _tri_inverse(a_mat, eye, same16, same32)
        rhs = jnp.concatenate([vb, kb * egc[:, sl]], axis=1).astype(BF16)
        uw = _dot(t_mat.astype(BF16), rhs)
        u = uw[:, :GDN_DV]
        wmat = uw[:, GDN_DV:]
        qd = qh * egc[:, sl]
        kt = kh * etail[:, sl]
        s = s_ref[h]
        o_parts = []
        v_parts = []
        for c in range(tb // CHUNK):
            rs = slice(c * CHUNK, (c + 1) * CHUNK)
            wq = _dot(jnp.concatenate([wmat[rs], qd[rs]], axis=0).astype(BF16), s.astype(BF16))
            vn = u[rs] - wq[:CHUNK]
            egl = jnp.exp(gc_bc[(c + 1) * CHUNK - 1:(c + 1) * CHUNK, sl])
            s = s * egl + _dot_tn(kt[rs].astype(BF16), vn.astype(BF16))
            o_parts.append(wq[CHUNK:])
            v_parts.append(vn)
        s_ref[h] = s
        o = jnp.concatenate(o_parts, axis=0) + _dot(
            attn.astype(BF16), jnp.concatenate(v_parts, axis=0).astype(BF16))
        y_ref[0, :, sl] = _gdn_gate_out(o, gz[:, sl], gout).astype(BF16)

    @pl.when(t_idx == pl.num_programs(1) - 1)
    def _():
        s_out_ref[0] = s_ref[...]


def _gdn_prompt(qkv, misc, grow, gz, convw, lanep, rowp, gout):
    b, t, _ = qkv.shape
    tb = GDN_TB
    return pl.pallas_call(
        _gdn_prompt_body,
        grid=(b, t // tb),
        in_specs=[
            pl.BlockSpec((1, tb, CONV_DIM), lambda i, j: (i, j, 0)),
            pl.BlockSpec((1, tb, LANE), lambda i, j: (i, j, 0)),
            pl.BlockSpec((1, SUBLANE, tb), lambda i, j: (i, 0, j)),
            pl.BlockSpec((1, tb, GDN_V), lambda i, j: (i, j, 0)),
            _full(convw.shape), _full(lanep.shape), _full(rowp.shape), _full(gout.shape),
        ],
        out_specs=[
            pl.BlockSpec((1, tb, GDN_V), lambda i, j: (i, j, 0)),
            pl.BlockSpec((1, GDN_HEADS, GDN_DK, GDN_DV), lambda i, j: (i, 0, 0, 0)),
        ],
        out_shape=[
            jax.ShapeDtypeStruct((b, t, GDN_V), BF16),
            jax.ShapeDtypeStruct((b, GDN_HEADS, GDN_DK, GDN_DV), F32),
        ],
        scratch_shapes=[
            pltpu.VMEM((SUBLANE + tb, CONV_DIM), F32),
            pltpu.VMEM((GDN_HEADS, GDN_DK, GDN_DV), F32),
        ],
        compiler_params=_cparams(("parallel", "arbitrary")),
        name="gdn_prompt",
    )(qkv, misc, grow, gz, convw, lanep, rowp, gout)


GDN_DEC_SEQS = 8


def _gdn_decode_body(qkv_ref, cprev_ref, misc_ref, gz_ref, s_in_ref, convw_ref, lanep_ref, gout_ref,
                     y_ref, s_out_ref, o_ref):
    ns = GDN_DEC_SEQS
    w = convw_ref[...]
    conv = qkv_ref[...] * w[CONV_W - 1:CONV_W, :]
    for i in range(CONV_W - 1):
        conv = conv + cprev_ref[i] * w[i:i + 1, :]
    conv = _silu(conv)
    beta_l, g_l = _gate_lane_values(misc_ref[...], lanep_ref[...])
    beta_bc = _dot01_right(beta_l, _expand_matrix(MISC_B))
    eg = jnp.exp(_dot01_right(g_l, _expand_matrix(MISC_A)))
    row0 = jnp.where(_iota((SUBLANE, LANE), 0) == 0, 1.0, 0.0).astype(BF16)
    for h in range(GDN_HEADS):
        sl = slice(h * GDN_DK, (h + 1) * GDN_DK)
        qh = conv[:, h * GDN_DK:(h + 1) * GDN_DK]
        kh = conv[:, GDN_QK + h * GDN_DK:GDN_QK + (h + 1) * GDN_DK]
        vh = conv[:, 2 * GDN_QK + h * GDN_DV:2 * GDN_QK + (h + 1) * GDN_DV]
        qh = qh * lax.rsqrt(jnp.sum(qh * qh, axis=-1, keepdims=True) + EPS) * (GDN_DK ** -0.5)
        kh = kh * lax.rsqrt(jnp.sum(kh * kh, axis=-1, keepdims=True) + EPS)
        for i in range(ns):
            k8 = jnp.broadcast_to(kh[i:i + 1, :], (SUBLANE, GDN_DK))
            q8 = jnp.broadcast_to(qh[i:i + 1, :], (SUBLANE, GDN_DK))
            s = s_in_ref[i, h] * eg[i:i + 1, sl]
            kv = _dot(k8.astype(BF16), s.astype(BF16))[0:1, :]
            delta = (vh[i:i + 1, :] - kv) * beta_bc[i:i + 1, sl]
            k_hi, k_mid, k_lo = _split3(k8)
            kcol = _dot_tn(k_hi, row0) + _dot_tn(k_mid, row0) + _dot_tn(k_lo, row0)
            s = s + kcol * delta
            s_out_ref[i, h] = s
            o_ref[i:i + 1, sl] = _dot(q8.astype(BF16), s.astype(BF16))[0:1, :]
    gout = gout_ref[...]
    gz = gz_ref[...]
    for h in range(GDN_HEADS):
        sl = slice(h * GDN_DV, (h + 1) * GDN_DV)
        y_ref[:, sl] = _gdn_gate_out(o_ref[:, sl], gz[:, sl], gout).astype(BF16)


def _gdn_decode(qkv, cprev, misc, gz, s_in, convw, lanep, gout):
    n = qkv.shape[0]
    ns = GDN_DEC_SEQS
    state_spec = pl.BlockSpec((ns, GDN_HEADS, GDN_DK, GDN_DV), lambda i: (i, 0, 0, 0))
    return pl.pallas_call(
        _gdn_decode_body,
        grid=(n // ns,),
        in_specs=[
            pl.BlockSpec((ns, CONV_DIM), lambda i: (i, 0)),
            pl.BlockSpec((CONV_W - 1, ns, CONV_DIM), lambda i: (0, i, 0)),
            pl.BlockSpec((ns, LANE), lambda i: (i, 0)),
            pl.BlockSpec((ns, GDN_V), lambda i: (i, 0)),
            state_spec,
            _full(convw.shape), _full(lanep.shape), _full(gout.shape),
        ],
        out_specs=[pl.BlockSpec((ns, GDN_V), lambda i: (i, 0)), state_spec],
        out_shape=[
            jax.ShapeDtypeStruct((n, GDN_V), BF16),
            jax.ShapeDtypeStruct(s_in.shape, F32),
        ],
        scratch_shapes=[pltpu.VMEM((ns, GDN_V), F32)],
        compiler_params=_cparams(("parallel",)),
        name="gdn_decode",
    )(qkv, cprev, misc, gz, s_in, convw, lanep, gout)


def _rope_padded(x, cos, sina, sinb):
    return (x * cos + pltpu.roll(x, LANE - QK_ROPE // 2, axis=1) * sina
            + pltpu.roll(x, QK_ROPE // 2, axis=1) * sinb)


def _mla_queries(qa, wqb_ref, g_qa, g_nope, g_rope, cos, sina, sinb):
    q = _dot(_rms(qa, g_qa).astype(BF16), wqb_ref[...])
    out = []
    for h in range(MLA_HEADS):
        qn = _rms(q[:, h * QK_PAD:h * QK_PAD + QK_NOPE], g_nope)
        qp = _rms(q[:, h * QK_PAD + QK_NOPE:(h + 1) * QK_PAD], g_rope, QK_ROPE)
        out.append((qn, _rope_padded(qp, cos, sina, sinb)))
    return out


def _mla_latent(kvc, misc, g_kva, g_krope, cos, sina, sinb):
    c = _rms(kvc, g_kva)
    lane = _iota(misc.shape, 1)
    kp = _rms(jnp.where(lane < QK_ROPE, misc, 0.0), g_krope, QK_ROPE)
    return c, _rope_padded(kp, cos, sina, sinb)


def _mla_prep_body(qa_ref, kvc_ref, misc_ref, tab_ref, wqb_ref, wk_ref, wv_ref,
                   g_qa_ref, g_nope_ref, g_rope_ref, g_kva_ref, g_knope_ref, g_krope_ref,
                   q_ref, k_ref, v_ref, c_ref, kpe_ref):
    cos, sina, sinb = tab_ref[0], tab_ref[1], tab_ref[2]
    qs = _mla_queries(qa_ref[...], wqb_ref, g_qa_ref[...], g_nope_ref[...], g_rope_ref[...],
                      cos, sina, sinb)
    c, kp = _mla_latent(kvc_ref[...], misc_ref[...], g_kva_ref[...], g_krope_ref[...], cos, sina, sinb)
    c_ref[...] = c
    kpe_ref[...] = kp[:, :QK_ROPE]
    cb = c.astype(BF16)
    kn = _dot(cb, wk_ref[...])
    v = _dot(cb, wv_ref[...])
    g_kn = g_knope_ref[...]
    kp16 = kp.astype(BF16)
    for h in range(MLA_HEADS):
        qn, qp = qs[h]
        q_ref[h, :, 0:QK_NOPE] = (qn * MLA_SCALE).astype(BF16)
        q_ref[h, :, QK_NOPE:QK_PAD] = (qp * MLA_SCALE).astype(BF16)
        k_ref[h, :, 0:QK_NOPE] = _rms(kn[:, h * QK_NOPE:(h + 1) * QK_NOPE], g_kn).astype(BF16)
        k_ref[h, :, QK_NOPE:QK_PAD] = kp16
        v_ref[h] = v[:, h * V_HEAD:(h + 1) * V_HEAD].astype(BF16)


def _mla_prep(qa, kvc, misc, tabs, wqb, wk, wv, gains, tm):
    t = qa.shape[0]
    n_tab = tabs.shape[1] // tm
    row = lambda n: pl.BlockSpec((tm, n), lambda i: (i, 0))
    head = lambda n: pl.BlockSpec((MLA_HEADS, tm, n), lambda i: (0, i, 0))
    return pl.pallas_call(
        _mla_prep_body,
        grid=(t // tm,),
        in_specs=[row(Q_LORA), row(KV_LORA), row(LANE),
                  pl.BlockSpec((3, tm, LANE), lambda i: (0, i % n_tab, 0)),
                  _full(wqb.shape), _full(wk.shape), _full(wv.shape)]
                 + [_full(g.shape) for g in gains],
        out_specs=[head(QK_PAD), head(QK_PAD), head(V_HEAD), row(KV_LORA), row(QK_ROPE)],
        out_shape=[
            jax.ShapeDtypeStruct((MLA_HEADS, t, QK_PAD), BF16),
            jax.ShapeDtypeStruct((MLA_HEADS, t, QK_PAD), BF16),
            jax.ShapeDtypeStruct((MLA_HEADS, t, V_HEAD), BF16),
            jax.ShapeDtypeStruct((t, KV_LORA), F32),
            jax.ShapeDtypeStruct((t, QK_ROPE), F32),
        ],
        compiler_params=_cparams(("parallel",)),
        name="mla_prep",
    )(qa, kvc, misc, tabs, wqb, wk, wv, *gains)


FLASH_T = 256


def _flash_body(q_ref, k_ref, v_ref, o_ref):
    tq = FLASH_T
    qi = pl.program_id(2)
    q = q_ref[0]

    def block(j, carry, masked):
        m, l, acc = carry
        start = pl.multiple_of(j * tq, tq)
        s = _dot_nt(q, k_ref[0, pl.ds(start, tq), :])
        if masked:
            s = jnp.where(_iota(s.shape, 1) <= _iota(s.shape, 0), s, -jnp.inf)
        m_new = jnp.maximum(m, jnp.max(s, axis=-1, keepdims=True))
        p = jnp.exp(s - m_new)
        corr = jnp.exp(m - m_new)
        l = l * corr + jnp.sum(p, axis=-1, keepdims=True)
        acc = acc * corr + _dot(p.astype(BF16), v_ref[0, pl.ds(start, tq), :])
        return m_new, l, acc

    init = (jnp.full((tq, 1), -jnp.inf, F32), jnp.zeros((tq, 1), F32), jnp.zeros((tq, V_HEAD), F32))
    carry = lax.fori_loop(0, qi, lambda j, c: block(j, c, False), init)
    _, l, acc = block(qi, carry, True)
    o_ref[...] = (acc / l).astype(BF16)


def _mla_flash(q, k, v, b, t):
    tq = FLASH_T
    nq = t // tq
    return pl.pallas_call(
        _flash_body,
        grid=(b, MLA_HEADS, nq),
        in_specs=[
            pl.BlockSpec((1, tq, QK_PAD), lambda i, h, j: (h, i * nq + j, 0)),
            pl.BlockSpec((1, t, QK_PAD), lambda i, h, j: (h, i, 0)),
            pl.BlockSpec((1, t, V_HEAD), lambda i, h, j: (h, i, 0)),
        ],
        out_specs=pl.BlockSpec((tq, V_HEAD), lambda i, h, j: (i * nq + j, h)),
        out_shape=jax.ShapeDtypeStruct((b * t, MLA_HEADS * V_HEAD), BF16),
        compiler_params=_cparams(("parallel", "parallel", "arbitrary")),
        name="mla_flash",
    )(q, k, v)


def _mla_dec_prep_body(qa_ref, kvc_ref, misc_ref, tab_ref, wqb_ref, wk_ref,
                       g_qa_ref, g_nope_ref, g_rope_ref, g_kva_ref, g_knope_ref, g_krope_ref,
                       qabs_ref, qpe_ref, c_ref, kpe_ref):
    cos, sina, sinb = tab_ref[0], tab_ref[1], tab_ref[2]
    qs = _mla_queries(qa_ref[...], wqb_ref, g_qa_ref[...], g_nope_ref[...], g_rope_ref[...],
                      cos, sina, sinb)
    c, kp = _mla_latent(kvc_ref[...], misc_ref[...], g_kva_ref[...], g_krope_ref[...], cos, sina, sinb)
    c_ref[...] = c
    kpe_ref[...] = kp
    g_kn = g_knope_ref[...]
    for h in range(MLA_HEADS):
        qn, qp = qs[h]
        qabs_ref[h] = _dot_nt((qn * g_kn * MLA_SCALE).astype(BF16),
                              wk_ref[:, h * QK_NOPE:(h + 1) * QK_NOPE])
        qpe_ref[h] = qp * MLA_SCALE


def _mla_dec_prep(qa, kvc, misc, tabs, wqb, wk, gains):
    n = qa.shape[0]
    args = (qa, kvc, misc, tabs, wqb, wk) + tuple(gains)
    return pl.pallas_call(
        _mla_dec_prep_body,
        grid=(1,),
        in_specs=[_full(a.shape) for a in args],
        out_specs=[_full((MLA_HEADS, n, KV_LORA)), _full((MLA_HEADS, n, LANE)),
                   _full((n, KV_LORA)), _full((n, LANE))],
        out_shape=[
            jax.ShapeDtypeStruct((MLA_HEADS, n, KV_LORA), F32),
            jax.ShapeDtypeStruct((MLA_HEADS, n, LANE), F32),
            jax.ShapeDtypeStruct((n, KV_LORA), F32),
            jax.ShapeDtypeStruct((n, LANE), F32),
        ],
        compiler_params=_cparams(("arbitrary",)),
        name="mla_dec_prep",
    )(*args)


def _mla_dec_attn_body(pt_ref, *refs):
    npg = PAGES_PER_STEP
    ckv_refs = refs[:npg]
    kpe_refs = refs[npg:2 * npg]
    qabs_ref, qpe_ref, wkt_ref = refs[2 * npg:2 * npg + 3]
    m_out, l_out, acc_out = refs[2 * npg + 3:2 * npg + 6]
    lhs_ref, m_ref, l_ref, acc_ref = refs[2 * npg + 6:]
    del pt_ref
    b = pl.program_id(0)
    g = pl.program_id(1)
    nk = MLA_HEADS * QK_NOPE

    @pl.when((b == 0) & (g == 0))
    def _():
        lhs_ref[0:nk, :] = wkt_ref[...]

    @pl.when(g == 0)
    def _():
        lhs_ref[nk:nk + BF16_ROWS, :] = qabs_ref[0]
        m_ref[...] = jnp.full(m_ref.shape, -jnp.inf, F32)
        l_ref[...] = jnp.zeros(l_ref.shape, F32)
        acc_ref[...] = jnp.zeros(acc_ref.shape, F32)

    qpe = qpe_ref[0]
    pages_per_sub = KEY_SUB // PAGE_SIZE
    for sb in range(npg // pages_per_sub):
        pr = range(sb * pages_per_sub, (sb + 1) * pages_per_sub)
        c2 = jnp.concatenate([ckv_refs[i][...] for i in pr], axis=0).astype(BF16)
        kp2 = jnp.concatenate([kpe_refs[i][...] for i in pr], axis=0).astype(BF16)
        kt = _dot_nt(lhs_ref[...], c2)
        ssq = [jnp.sum(jnp.square(kt[h * QK_NOPE:(h + 1) * QK_NOPE]), axis=0, keepdims=True)
               for h in range(MLA_HEADS)]
        ssq = jnp.concatenate(ssq + [jnp.ones((SUBLANE - MLA_HEADS, KEY_SUB), F32)], axis=0)
        pe = _dot_nt(qpe, kp2)
        s = kt[nk:nk + SUBLANE] * lax.rsqrt(ssq * (1.0 / QK_NOPE) + EPS) + pe[0:SUBLANE]
        m_old = m_ref[:, 0:1]
        m_new = jnp.maximum(m_old, jnp.max(s, axis=-1, keepdims=True))
        p = jnp.exp(s - m_new)
        corr = jnp.exp(m_old - m_new)
        l_ref[...] = l_ref[...] * corr + jnp.sum(p, axis=-1, keepdims=True)
        acc_ref[...] = acc_ref[...] * corr + _dot(p.astype(BF16), c2)
        m_ref[...] = jnp.broadcast_to(m_new, m_ref.shape)

    @pl.when(g == pl.num_programs(1) - 1)
    def _():
        m_out[0] = m_ref[...]
        l_out[0] = l_ref[...]
        acc_out[0] = acc_ref[...]


def _mla_dec_attn(page_table, ckv_pool, kpe_pool, qabs, qpe, wkt):
    n, n_pages = page_table.shape
    npg = PAGES_PER_STEP
    nk = MLA_HEADS * QK_NOPE

    def page_map(i):
        return lambda b, g, pt: (pt[b * n_pages + g * npg + i], 0, 0)

    seq = lambda shape: pl.BlockSpec((1,) + shape, lambda b, g, pt: (b, 0, 0))
    grid_spec = pltpu.PrefetchScalarGridSpec(
        num_scalar_prefetch=1,
        grid=(n, n_pages // npg),
        in_specs=[pl.BlockSpec((None, PAGE_SIZE, KV_LORA), page_map(i)) for i in range(npg)]
                 + [pl.BlockSpec((None, PAGE_SIZE, QK_ROPE), page_map(i)) for i in range(npg)]
                 + [seq((BF16_ROWS, KV_LORA)), seq((BF16_ROWS, QK_ROPE)),
                    pl.BlockSpec(wkt.shape, lambda b, g, pt: (0, 0))],
        out_specs=[seq((SUBLANE, LANE)), seq((SUBLANE, LANE)), seq((SUBLANE, KV_LORA))],
        scratch_shapes=[
            pltpu.VMEM((nk + BF16_ROWS, KV_LORA), BF16),
            pltpu.VMEM((SUBLANE, LANE), F32),
            pltpu.VMEM((SUBLANE, LANE), F32),
            pltpu.VMEM((SUBLANE, KV_LORA), F32),
        ],
    )
    return pl.pallas_call(
        _mla_dec_attn_body,
        grid_spec=grid_spec,
        out_shape=[
            jax.ShapeDtypeStruct((n, SUBLANE, LANE), F32),
            jax.ShapeDtypeStruct((n, SUBLANE, LANE), F32),
            jax.ShapeDtypeStruct((n, SUBLANE, KV_LORA), F32),
        ],
        compiler_params=_cparams(("arbitrary", "arbitrary")),
        name="mla_dec_attn",
    )(page_table.reshape(-1), *([ckv_pool] * npg), *([kpe_pool] * npg), qabs, qpe, wkt)


def _mla_dec_finish_body(m_ref, l_ref, acc_ref, qabs_ref, qpe_ref, c_ref, kpe_ref, wk_ref, wv_ref,
                         y_ref):
    c = c_ref[...]
    kp = kpe_ref[...]
    cb = c.astype(BF16)
    kn = _dot(cb, wk_ref[...])
    for h in range(MLA_HEADS):
        knh = kn[:, h * QK_NOPE:(h + 1) * QK_NOPE]
        r = lax.rsqrt(jnp.sum(knh * knh, axis=-1, keepdims=True) * (1.0 / QK_NOPE) + EPS)
        s_new = (r * jnp.sum(qabs_ref[h] * c, axis=-1, keepdims=True)
                 + jnp.sum(qpe_ref[h] * kp, axis=-1, keepdims=True))
        m_old = m_ref[h][:, 0:1]
        m_new = jnp.maximum(m_old, s_new)
        p = jnp.exp(s_new - m_new)
        corr = jnp.exp(m_old - m_new)
        l = l_ref[h][:, 0:1] * corr + p
        lat = (acc_ref[h] * corr + p * c) / l
        y_ref[:, h * V_HEAD:(h + 1) * V_HEAD] = _dot(
            lat.astype(BF16), wv_ref[:, h * V_HEAD:(h + 1) * V_HEAD]).astype(BF16)


def _mla_dec_finish(m, l, acc, qabs, qpe, c, kpe, wk, wv):
    n = c.shape[0]
    args = (m, l, acc, qabs, qpe, c, kpe, wk, wv)
    return pl.pallas_call(
        _mla_dec_finish_body,
        grid=(1,),
        in_specs=[_full(a.shape) for a in args],
        out_specs=_full((n, MLA_HEADS * V_HEAD)),
        out_shape=jax.ShapeDtypeStruct((n, MLA_HEADS * V_HEAD), BF16),
        compiler_params=_cparams(("arbitrary",)),
        name="mla_dec_finish",
    )(*args)


def _mix_q_body(x_ref, yg_ref, ym_ref, wmix_ref, g_mem_ref, wq_ref, g_qn_ref, x1_ref, q_ref):
    x1 = (x_ref[...] + _dot(yg_ref[...], wmix_ref[0:GDN_V, :])
          + _dot(ym_ref[...], wmix_ref[GDN_V:, :]))
    x1_ref[...] = x1
    q = _dot(_rms(x1, g_mem_ref[...]).astype(BF16), wq_ref[...])
    g_qn = g_qn_ref[...]
    for h in range(MEM_HEADS):
        sl = slice(h * MEM_HD, (h + 1) * MEM_HD)
        q_ref[:, sl] = (_rms(q[:, sl], g_qn) * (MEM_HD ** -0.5)).astype(BF16)


def _mix_q(x, yg, ym, wmix, g_mem, wq, g_qn, tm):
    t = x.shape[0]
    row = lambda n: pl.BlockSpec((tm, n), lambda i: (i, 0))
    return pl.pallas_call(
        _mix_q_body,
        grid=(t // tm,),
        in_specs=[row(D_MODEL), row(GDN_V), row(MLA_HEADS * V_HEAD), _full(wmix.shape),
                  _full(g_mem.shape), _full(wq.shape), _full(g_qn.shape)],
        out_specs=[row(D_MODEL), row(MEM_DIM)],
        out_shape=[jax.ShapeDtypeStruct((t, D_MODEL), F32), jax.ShapeDtypeStruct((t, MEM_DIM), BF16)],
        compiler_params=_cparams(("parallel",)),
        name="mix_q",
    )(x, yg, ym, wmix, g_mem, wq, g_qn)


def _mem_attn_body(q_ref, k_ref, v_ref, o_ref):
    q = q_ref[0]
    for h in range(MEM_HEADS):
        sl = slice(h * MEM_HD, (h + 1) * MEM_HD)
        s = _dot_nt(q[:, sl], k_ref[0, :, sl].astype(BF16))
        p = jnp.exp(s - jnp.max(s, axis=-1, keepdims=True))
        p = p / jnp.sum(p, axis=-1, keepdims=True)
        o_ref[0, :, sl] = _dot(p.astype(BF16), v_ref[0, :, sl].astype(BF16)).astype(BF16)


def _mem_attn(q, k, v, tq):
    nb, t, _ = q.shape
    m = k.shape[1]
    return pl.pallas_call(
        _mem_attn_body,
        grid=(nb, t // tq),
        in_specs=[pl.BlockSpec((1, tq, MEM_DIM), lambda i, j: (i, j, 0)),
                  pl.BlockSpec((1, m, MEM_DIM), lambda i, j: (i, 0, 0)),
                  pl.BlockSpec((1, m, MEM_DIM), lambda i, j: (i, 0, 0))],
        out_specs=pl.BlockSpec((1, tq, MEM_DIM), lambda i, j: (i, j, 0)),
        out_shape=jax.ShapeDtypeStruct((nb, t, MEM_DIM), BF16),
        compiler_params=_cparams(("parallel", "arbitrary")),
        name="mem_attn",
    )(q, k, v)


def _out_ffn_body(x1_ref, o_ref, wo_ref, g_ffn_ref, wg_ref, wu_ref, wd_ref, y_ref):
    x2 = x1_ref[...] + _dot(o_ref[...], wo_ref[...])
    h = _rms(x2, g_ffn_ref[...]).astype(BF16)
    act = (_silu(_dot(h, wg_ref[...])) * _dot(h, wu_ref[...])).astype(BF16)
    y_ref[...] = x2 + _dot(act, wd_ref[...])


def _out_ffn(x1, o, wo, g_ffn, wg, wu, wd, tm):
    t = x1.shape[0]
    row = lambda n: pl.BlockSpec((tm, n), lambda i: (i, 0))
    const = lambda a: pl.BlockSpec(a.shape, lambda i: (0,) * a.ndim, pipeline_mode=pl.Buffered(1))
    return pl.pallas_call(
        _out_ffn_body,
        grid=(t // tm,),
        in_specs=[row(D_MODEL), row(MEM_DIM), const(wo), const(g_ffn), const(wg), const(wu), const(wd)],
        out_specs=row(D_MODEL),
        out_shape=jax.ShapeDtypeStruct((t, D_MODEL), F32),
        compiler_params=_cparams(("parallel",)),
        name="out_ffn",
    )(x1, o, wo, g_ffn, wg, wu, wd)


def _mem_kv_body(mem_ref, wk_ref, wv_ref, g_ref, k_ref, v_ref):
    mb = mem_ref[...].astype(BF16)
    k = _dot(mb, wk_ref[...])
    g = g_ref[...]
    for h in range(MEM_HEADS):
        sl = slice(h * MEM_HD, (h + 1) * MEM_HD)
        k_ref[:, sl] = _rms(k[:, sl], g)
    v_ref[...] = _dot(mb, wv_ref[...])


def _mem_kv(mem, wk, wv, g, tm):
    t = mem.shape[0]
    row = lambda n: pl.BlockSpec((tm, n), lambda i: (i, 0))
    return pl.pallas_call(
        _mem_kv_body,
        grid=(t // tm,),
        in_specs=[row(D_MODEL), _full(wk.shape), _full(wv.shape), _full(g.shape)],
        out_specs=[row(MEM_DIM), row(MEM_DIM)],
        out_shape=[jax.ShapeDtypeStruct((t, MEM_DIM), F32)] * 2,
        compiler_params=_cparams(("parallel",)),
        name="mem_kv",
    )(mem, wk, wv, g)


def _row(v):
    return v.reshape(1, -1).astype(F32)


def _pad_lanes(v, width=LANE):
    return jnp.pad(v, ((0, 0), (0, width - v.shape[1])))


def _rope_tables(pos, rows):
    half = QK_ROPE // 2
    inv = ROPE_THETA ** (-jnp.arange(half, dtype=F32) / half)
    ang = pos.astype(F32)[:, None] * inv[None, :]
    cos, sin = jnp.cos(ang), jnp.sin(ang)
    zero = jnp.zeros_like(cos)
    tabs = jnp.stack([
        _pad_lanes(jnp.concatenate([cos, cos], axis=1)),
        _pad_lanes(jnp.concatenate([-sin, zero], axis=1)),
        _pad_lanes(jnp.concatenate([zero, sin], axis=1)),
    ])
    return jnp.broadcast_to(tabs, (3, rows, LANE)) if tabs.shape[1] == 1 else tabs


def kernel(x_prompt, x_sample, cache_mla_ckv, cache_mla_kpe, cache_mem_k, cache_mem_v, state_gdn_S, state_gdn_conv, page_table, mem_prompt, norm_mix_g, w_in, gdn_conv_w, gdn_A_log, gdn_dt_bias, gdn_out_norm_g, mla_q_a_norm_g, mla_w_q_b, mla_kv_a_norm_g, mla_w_kv_b, mla_qn_nope_g, mla_qn_rope_g, mla_kn_nope_g, mla_kn_rope_g, w_mix_out, norm_mem_g, mem_wq, mem_wk, mem_wv, mem_wo, mem_qn_g, mem_kn_g, norm_ffn_g, ffn_w_gate, ffn_w_up, ffn_w_down):
    depth = w_in.shape[0]
    assert depth == 1, "single-layer trunk"
    bsz, seq, _ = x_prompt.shape
    nseq, dseq, _ = x_sample.shape
    assert dseq == 1, "one new token per decode sequence"
    past_len = page_table.shape[1] * PAGE_SIZE
    n_tok = bsz * seq

    w = w_in[0]
    o_gz = CONV_DIM
    o_b = o_gz + GDN_V
    o_a = o_b + GDN_HEADS
    o_qa = o_a + GDN_HEADS
    o_c = o_qa + Q_LORA
    o_kpe = o_c + KV_LORA
    misc_w = _pad_lanes(jnp.concatenate([w[:, o_kpe:o_kpe + QK_ROPE], w[:, o_b:o_qa]], axis=1))
    w_in_p = jnp.concatenate([w[:, :o_b], w[:, o_qa:o_kpe], misc_w], axis=1).astype(BF16)

    wqb = mla_w_q_b[0].reshape(Q_LORA, MLA_HEADS, QK_NOPE + QK_ROPE)
    wqb = jnp.pad(wqb, ((0, 0), (0, 0), (0, QK_PAD - QK_NOPE - QK_ROPE)))
    wqb = wqb.reshape(Q_LORA, MLA_HEADS * QK_PAD).astype(BF16)
    wkvb = mla_w_kv_b[0].reshape(KV_LORA, MLA_HEADS, QK_NOPE + V_HEAD)
    wk = wkvb[:, :, :QK_NOPE].reshape(KV_LORA, MLA_HEADS * QK_NOPE).astype(BF16)
    wv = wkvb[:, :, QK_NOPE:].reshape(KV_LORA, MLA_HEADS * V_HEAD).astype(BF16)
    wkt = wk.T

    lanep = jnp.zeros((2, LANE), F32)
    lanep = lanep.at[0, MISC_A:MISC_A + GDN_HEADS].set(gdn_A_log[0])
    lanep = lanep.at[1, MISC_A:MISC_A + GDN_HEADS].set(gdn_dt_bias[0])
    rowp = jnp.zeros((2 * SUBLANE,), F32)
    rowp = rowp.at[GDN_HEADS:2 * GDN_HEADS].set(gdn_A_log[0])
    rowp = rowp.at[SUBLANE + GDN_HEADS:SUBLANE + 2 * GDN_HEADS].set(gdn_dt_bias[0])
    rowp = jnp.broadcast_to(rowp[:, None], (2 * SUBLANE, GDN_TB))

    g_mix = _row(norm_mix_g[0])
    g_out = _row(gdn_out_norm_g[0])
    mla_gains = (_row(mla_q_a_norm_g[0]), _row(mla_qn_nope_g[0]), _pad_lanes(_row(mla_qn_rope_g[0])),
                 _row(mla_kv_a_norm_g[0]), _row(mla_kn_nope_g[0]), _pad_lanes(_row(mla_kn_rope_g[0])))
    wmix = w_mix_out[0].astype(BF16)
    wq_mem = mem_wq[0].astype(BF16)
    wk_mem = mem_wk[0].astype(BF16)
    wv_mem = mem_wv[0].astype(BF16)
    wo_mem = mem_wo[0].astype(BF16)
    wg = ffn_w_gate[0].astype(BF16)
    wu = ffn_w_up[0].astype(BF16)
    wd = ffn_w_down[0].astype(BF16)
    g_mem = _row(norm_mem_g[0])
    g_ffn = _row(norm_ffn_g[0])
    g_qn = _row(mem_qn_g[0])
    g_kn = _row(mem_kn_g[0])
    conv_w = gdn_conv_w[0]

    xp = x_prompt.reshape(n_tok, D_MODEL)
    qkv, gz, qa, kvc, misc = _in_proj(xp, g_mix, w_in_p, 256)
    qkv3 = qkv.reshape(bsz, seq, CONV_DIM)
    grow = jnp.swapaxes(misc.reshape(bsz, seq, LANE)[:, :, MISC_B:MISC_B + SUBLANE], 1, 2)
    y_gdn, p_s = _gdn_prompt(qkv3, misc.reshape(bsz, seq, LANE), grow, gz.reshape(bsz, seq, GDN_V),
                             conv_w, lanep, rowp, g_out)
    p_conv = qkv3[:, seq - (CONV_W - 1):, :]

    tabs_p = _rope_tables(jnp.arange(seq), seq)
    q_full, k_full, v_full, p_c, p_kpe = _mla_prep(qa, kvc, misc, tabs_p, wqb, wk, wv, mla_gains, 256)
    y_mla = _mla_flash(q_full, k_full, v_full, bsz, seq)

    mem_k, mem_v = _mem_kv(mem_prompt.reshape(-1, D_MODEL), wk_mem, wv_mem, g_kn, 256)
    n_mem = mem_prompt.shape[1]
    x1, q_mem = _mix_q(xp, y_gdn.reshape(n_tok, GDN_V), y_mla, wmix, g_mem, wq_mem, g_qn, 256)
    o_mem = _mem_attn(q_mem.reshape(bsz, seq, MEM_DIM), mem_k.reshape(bsz, n_mem, MEM_DIM),
                      mem_v.reshape(bsz, n_mem, MEM_DIM), 256)
    y_prompt = _out_ffn(x1, o_mem.reshape(n_tok, MEM_DIM), wo_mem, g_ffn, wg, wu, wd, 256)

    xs = x_sample.reshape(nseq, D_MODEL)
    qkv_s, gz_s, qa_s, kvc_s, misc_s = _in_proj(xs, g_mix, w_in_p, nseq)
    conv_prev = state_gdn_conv[0]
    y_gdn_s, s_new = _gdn_decode(qkv_s, jnp.swapaxes(conv_prev, 0, 1), misc_s, gz_s, state_gdn_S[0],
                                 conv_w, lanep, g_out)
    s_conv = jnp.concatenate([conv_prev[:, 1:, :], qkv_s[:, None, :]], axis=1)

    tabs_s = _rope_tables(jnp.full((1,), past_len), nseq)
    qabs, qpe, c_new, kpe_new = _mla_dec_prep(qa_s, kvc_s, misc_s, tabs_s, wqb, wk, mla_gains)
    pad_rows = lambda a: jnp.pad(jnp.swapaxes(a, 0, 1), ((0, 0), (0, BF16_ROWS - MLA_HEADS), (0, 0)))
    m_run, l_run, acc_run = _mla_dec_attn(
        page_table, cache_mla_ckv[0], cache_mla_kpe[0],
        pad_rows(qabs).astype(BF16), pad_rows(qpe[:, :, :QK_ROPE]).astype(BF16), wkt)
    heads_first = lambda a: jnp.swapaxes(a[:, :MLA_HEADS], 0, 1)
    y_mla_s = _mla_dec_finish(heads_first(m_run), heads_first(l_run), heads_first(acc_run),
                              qabs, qpe, c_new, kpe_new, wk, wv)

    x1_s, q_mem_s = _mix_q(xs, y_gdn_s, y_mla_s, wmix, g_mem, wq_mem, g_qn, nseq)
    q_rep = jnp.broadcast_to(q_mem_s[:, None, :], (nseq, BF16_ROWS, MEM_DIM))
    o_mem_s = _mem_attn(q_rep, cache_mem_k[0].reshape(nseq, -1, MEM_DIM),
                        cache_mem_v[0].reshape(nseq, -1, MEM_DIM), BF16_ROWS)[:, 0, :]
    y_sample = _out_ffn(x1_s, o_mem_s, wo_mem, g_ffn, wg, wu, wd, nseq)

    return (
        y_prompt.reshape(bsz, seq, D_MODEL),
        y_sample.reshape(nseq, 1, D_MODEL),
        p_c.reshape(1, bsz, seq, KV_LORA),
        p_kpe.reshape(1, bsz, seq, QK_ROPE),
        mem_k.reshape(1, bsz, n_mem, MEM_HEADS, MEM_HD),
        mem_v.reshape(1, bsz, n_mem, MEM_HEADS, MEM_HD),
        p_s[None],
        p_conv[None],
        c_new.reshape(1, nseq, 1, KV_LORA),
        kpe_new[:, :QK_ROPE].reshape(1, nseq, 1, QK_ROPE),
        s_new[None],
        s_conv[None],
    )
```

```python
import functools

import jax
import jax.numpy as jnp
from jax import lax
from jax.experimental import pallas as pl
from jax.experimental.pallas import tpu as pltpu

F32 = jnp.float32
BF16 = jnp.bfloat16

D_MODEL = 1024
GDN_HEADS = 4
GDN_DK = 128
GDN_DV = 128
CONV_W = 4
CHUNK = 64
MLA_HEADS = 4
Q_LORA = 384
KV_LORA = 256
QK_NOPE = 128
QK_ROPE = 64
V_HEAD = 128
ROPE_THETA = 10000.0
PAGE_SIZE = 128
MEM_HEADS = 4
MEM_HD = 128
EPS = 1e-6
GDN_QK = GDN_HEADS * GDN_DK
GDN_V = GDN_HEADS * GDN_DV
CONV_DIM = 2 * GDN_QK + GDN_V
MLA_SCALE = (QK_NOPE + QK_ROPE) ** -0.5
MEM_DIM = MEM_HEADS * MEM_HD

LANE = 128
SUBLANE = 8
BF16_ROWS = 16
QK_PAD = 2 * LANE
MISC_B = QK_ROPE
MISC_A = QK_ROPE + GDN_HEADS
VMEM_LIMIT = 56 * 1024 * 1024

GDN_TB = 2 * CHUNK
PAGES_PER_STEP = 8


def _cparams(sem):
    return pltpu.CompilerParams(dimension_semantics=sem, vmem_limit_bytes=VMEM_LIMIT)


def _full(shape):
    n = len(shape)
    return pl.BlockSpec(shape, lambda *_: (0,) * n)


def _dot(a, b):
    return jnp.dot(a, b, preferred_element_type=F32)


def _dot_nt(a, b):
    return lax.dot_general(a, b, (((1,), (1,)), ((), ())), preferred_element_type=F32)


def _dot_tn(a, b):
    return lax.dot_general(a, b, (((0,), (0,)), ((), ())), preferred_element_type=F32)


def _split3(x):
    hi = x.astype(BF16)
    r = x - hi.astype(F32)
    mid = r.astype(BF16)
    lo = (r - mid.astype(F32)).astype(BF16)
    return hi, mid, lo


def _dot01_left(m01, x):
    hi, mid, lo = _split3(x)
    return _dot(m01, hi) + _dot(m01, mid) + _dot(m01, lo)


def _dot01_right(x, m01):
    hi, mid, lo = _split3(x)
    return _dot(hi, m01) + _dot(mid, m01) + _dot(lo, m01)


def _rms(x, g, n=None):
    n = x.shape[-1] if n is None else n
    ms = jnp.sum(x * x, axis=-1, keepdims=True) * (1.0 / n)
    return x * lax.rsqrt(ms + EPS) * g


def _silu(x):
    return x * jax.nn.sigmoid(x)


def _softplus(x):
    return jnp.maximum(x, 0.0) + jnp.log1p(jnp.exp(-jnp.abs(x)))


def _iota(shape, dim):
    return lax.broadcasted_iota(jnp.int32, shape, dim)


def _block_id(i, size):
    assert size & (size - 1) == 0
    return lax.shift_right_logical(i, size.bit_length() - 1)


def _in_proj_body(x_ref, g_ref, w_ref, qkv_ref, gz_ref, qa_ref, kvc_ref, misc_ref):
    h = _rms(x_ref[...], g_ref[...]).astype(BF16)
    off = 0
    for ref in (qkv_ref, gz_ref, qa_ref, kvc_ref, misc_ref):
        n = ref.shape[-1]
        ref[...] = _dot(h, w_ref[:, off:off + n])
        off += n


def _in_proj(x, g, w, tm):
    t = x.shape[0]
    widths = (CONV_DIM, GDN_V, Q_LORA, KV_LORA, LANE)
    return pl.pallas_call(
        _in_proj_body,
        grid=(t // tm,),
        in_specs=[pl.BlockSpec((tm, D_MODEL), lambda i: (i, 0)), _full(g.shape), _full(w.shape)],
        out_specs=[pl.BlockSpec((tm, n), lambda i: (i, 0)) for n in widths],
        out_shape=[jax.ShapeDtypeStruct((t, n), F32) for n in widths],
        compiler_params=_cparams(("parallel",)),
        name="in_proj",
    )(x, g, w)


def _gate_lane_values(misc, lanep):
    lane = _iota(misc.shape, 1)
    is_a = (lane >= MISC_A) & (lane < MISC_A + GDN_HEADS)
    beta_l = jax.nn.sigmoid(misc)
    g_l = -jnp.exp(lanep[0:1, :]) * _softplus(jnp.where(is_a, misc, 0.0) + lanep[1:2, :])
    return beta_l, g_l


def _expand_matrix(first_lane):
    r = _iota((LANE, GDN_V), 0)
    c = _iota((LANE, GDN_V), 1)
    return jnp.where(r == first_lane + _block_id(c, GDN_DV), 1.0, 0.0).astype(BF16)


def _gdn_gate_out(o, z, gout):
    return _rms(o, gout) * _silu(z)


def _tri_inverse(a_list, eye, same16, same32):
    def mm(xs, ys):
        return [_dot(x.astype(BF16), y.astype(BF16)) for x, y in zip(xs, ys)]

    def add(ts, us):
        return [t + u for t, u in zip(ts, us)]

    def sub(ts, us):
        return [t - u for t, u in zip(ts, us)]

    ad = [jnp.where(same16, a, 0.0) for a in a_list]
    t = [eye - x for x in ad]
    p = mm(ad, ad)
    t = add(t, mm(t, p))
    p = mm(p, p)
    t = add(t, mm(t, p))
    p = mm(p, p)
    t = add(t, mm(t, p))
    b1 = [jnp.where(same32, a, 0.0) - x for a, x in zip(a_list, ad)]
    t = sub(t, mm(t, mm(b1, t)))
    b2 = [jnp.where(same32, 0.0, a) for a in a_list]
    t = sub(t, mm(t, mm(b2, t)))
    return t


def _gdn_prompt_body(qkv_ref, misc_ref, grow_ref, gz_ref, convw_ref, lanep_ref, rowp_ref, gout_ref,
                     y_ref, s_out_ref, xe_ref, s_ref):
    tb = GDN_TB
    t_idx = pl.program_id(1)

    @pl.when(t_idx == 0)
    def _():
        s_ref[...] = jnp.zeros(s_ref.shape, F32)
        xe_ref[0:SUBLANE, :] = jnp.zeros((SUBLANE, CONV_DIM), F32)

    x = qkv_ref[0]
    xe_ref[SUBLANE:SUBLANE + tb, :] = x
    w = convw_ref[...]
    conv = x * w[CONV_W - 1:CONV_W, :]
    for i in range(CONV_W - 1):
        lo = SUBLANE - (CONV_W - 1) + i
        conv = conv + xe_ref[lo:lo + tb, :] * w[i:i + 1, :]
    xe_ref[0:SUBLANE, :] = x[tb - SUBLANE:tb, :]

    ri = _iota((tb, tb), 0)
    ci = _iota((tb, tb), 1)
    same64 = _block_id(ri, CHUNK) == _block_id(ci, CHUNK)
    lower = same64 & (ci <= ri)
    strict = same64 & (ci < ri)
    same16 = _block_id(ri, 16) == _block_id(ci, 16)
    same32 = _block_id(ri, 32) == _block_id(ci, 32)
    eye = jnp.where(ri == ci, 1.0, 0.0)
    l_incl = jnp.where(lower, 1.0, 0.0).astype(BF16)
    u_incl = jnp.where(same64 & (ri <= ci), 1.0, 0.0).astype(BF16)
    u_strict = jnp.where(same64 & (ci > ri), 1.0, 0.0).astype(BF16)

    beta_l, g_l = _gate_lane_values(misc_ref[0], lanep_ref[...])
    e_b = _expand_matrix(MISC_B)
    e_a = _expand_matrix(MISC_A)
    beta_bc = _dot01_right(beta_l, e_b)
    gc_bc = _dot01_right(_dot01_left(l_incl, g_l), e_a)
    gt_bc = _dot01_right(_dot01_left(u_strict, g_l), e_a)
    egc = jnp.exp(gc_bc)
    etail = jnp.exp(gt_bc)

    grow = grow_ref[0]
    rowi = _iota(grow.shape, 0)
    g_r = -jnp.exp(rowp_ref[0:SUBLANE, :]) * _softplus(
        jnp.where(rowi >= GDN_HEADS, grow, 0.0) + rowp_ref[SUBLANE:2 * SUBLANE, :])
    gcr = _dot01_right(g_r, u_incl)

    gout = gout_ref[...]
    gz = gz_ref[0]
    heads = range(GDN_HEADS)
    sls = [slice(h * GDN_DK, (h + 1) * GDN_DK) for h in heads]
    qh, kh, kb, vb, dm = [], [], [], [], []
    for h in heads:
        q = _silu(conv[:, h * GDN_DK:(h + 1) * GDN_DK])
        k = _silu(conv[:, GDN_QK + h * GDN_DK:GDN_QK + (h + 1) * GDN_DK])
        v = _silu(conv[:, 2 * GDN_QK + h * GDN_DV:2 * GDN_QK + (h + 1) * GDN_DV])
        q = q * lax.rsqrt(jnp.sum(q * q, axis=-1, keepdims=True) + EPS) * (GDN_DK ** -0.5)
        k = k * lax.rsqrt(jnp.sum(k * k, axis=-1, keepdims=True) + EPS)
        bb = beta_bc[:, sls[h]]
        qh.append(q)
        kh.append(k)
        kb.append(k * bb)
        vb.append(v * bb)
        diff = gc_bc[:, sls[h]] - gcr[GDN_HEADS + h:GDN_HEADS + h + 1, :]
        dm.append(jnp.where(lower, jnp.exp(jnp.where(lower, diff, 0.0)), 0.0))
    p = [_dot_nt(jnp.concatenate([kb[h], qh[h]], axis=0).astype(BF16), kh[h].astype(BF16))
         for h in heads]
    a_mat = [jnp.where(strict, p[h][:tb] * dm[h], 0.0) for h in heads]
    attn = [(p[h][tb:] * dm[h]).astype(BF16) for h in heads]
    t_mat = _tri_inverse(a_mat, eye, same16, same32)
    uw = [_dot(t_mat[h].astype(BF16),
               jnp.concatenate([vb[h], kb[h] * egc[:, sls[h]]], axis=1).astype(BF16)) for h in heads]
    qd = [qh[h] * egc[:, sls[h]] for h in heads]
    kt = [(kh[h] * etail[:, sls[h]]).astype(BF16) for h in heads]
    s = [s_ref[h] for h in heads]
    o_inter = [[] for _ in heads]
    v_new = [[] for _ in heads]
    for c in range(tb // CHUNK):
        rs = slice(c * CHUNK, (c + 1) * CHUNK)
        wq = [_dot(jnp.concatenate([uw[h][rs, GDN_DV:], qd[h][rs]], axis=0).astype(BF16),
                   s[h].astype(BF16)) for h in heads]
        vn = [uw[h][rs, :GDN_DV] - wq[h][:CHUNK] for h in heads]
        egl = [jnp.exp(gc_bc[(c + 1) * CHUNK - 1:(c + 1) * CHUNK, sls[h]]) for h in heads]
        s = [s[h] * egl[h] + _dot_tn(kt[h][rs], vn[h].astype(BF16)) for h in heads]
        for h in heads:
            o_inter[h].append(wq[h][CHUNK:])
            v_new[h].append(vn[h])
    o = [jnp.concatenate(o_inter[h], axis=0)
         + _dot(attn[h], jnp.concatenate(v_new[h], axis=0).astype(BF16)) for h in heads]
    for h in heads:
        s_ref[h] = s[h]
        y_ref[0, :, sls[h]] = _gdn_gate_out(o[h], gz[:, sls[h]], gout).astype(BF16)

    @pl.when(t_idx == pl.num_programs(1) - 1)
    def _():
        s_out_ref[0] = s_ref[...]


def _gdn_prompt(qkv, misc, grow, gz, convw, lanep, rowp, gout):
    b, t, _ = qkv.shape
    tb = GDN_TB
    return pl.pallas_call(
        _gdn_prompt_body,
        grid=(b, t // tb),
        in_specs=[
            pl.BlockSpec((1, tb, CONV_DIM), lambda i, j: (i, j, 0)),
            pl.BlockSpec((1, tb, LANE), lambda i, j: (i, j, 0)),
            pl.BlockSpec((1, SUBLANE, tb), lambda i, j: (i, 0, j)),
            pl.BlockSpec((1, tb, GDN_V), lambda i, j: (i, j, 0)),
            _full(convw.shape), _full(lanep.shape), _full(rowp.shape), _full(gout.shape),
        ],
        out_specs=[
            pl.BlockSpec((1, tb, GDN_V), lambda i, j: (i, j, 0)),
            pl.BlockSpec((1, GDN_HEADS, GDN_DK, GDN_DV), lambda i, j: (i, 0, 0, 0)),
        ],
        out_shape=[
            jax.ShapeDtypeStruct((b, t, GDN_V), BF16),
            jax.ShapeDtypeStruct((b, GDN_HEADS, GDN_DK, GDN_DV), F32),
        ],
        scratch_shapes=[
            pltpu.VMEM((SUBLANE + tb, CONV_DIM), F32),
            pltpu.VMEM((GDN_HEADS, GDN_DK, GDN_DV), F32),
        ],
        compiler_params=_cparams(("parallel", "arbitrary")),
        name="gdn_prompt",
    )(qkv, misc, grow, gz, convw, lanep, rowp, gout)


GDN_DEC_SEQS = 8


def _gdn_decode_body(qkv_ref, cprev_ref, misc_ref, gz_ref, s_in_ref, convw_ref, lanep_ref, gout_ref,
                     y_ref, s_out_ref, o_ref):
    ns = GDN_DEC_SEQS
    w = convw_ref[...]
    conv = qkv_ref[...] * w[CONV_W - 1:CONV_W, :]
    for i in range(CONV_W - 1):
        conv = conv + cprev_ref[i] * w[i:i + 1, :]
    conv = _silu(conv)
    beta_l, g_l = _gate_lane_values(misc_ref[...], lanep_ref[...])
    beta_bc = _dot01_right(beta_l, _expand_matrix(MISC_B))
    eg = jnp.exp(_dot01_right(g_l, _expand_matrix(MISC_A)))
    row0 = jnp.where(_iota((SUBLANE, LANE), 0) == 0, 1.0, 0.0).astype(BF16)
    units = [(h, i) for h in range(GDN_HEADS) for i in range(ns)]
    qn, kn, vv = {}, {}, {}
    for h in range(GDN_HEADS):
        qh = conv[:, h * GDN_DK:(h + 1) * GDN_DK]
        kh = conv[:, GDN_QK + h * GDN_DK:GDN_QK + (h + 1) * GDN_DK]
        qn[h] = qh * lax.rsqrt(jnp.sum(qh * qh, axis=-1, keepdims=True) + EPS) * (GDN_DK ** -0.5)
        kn[h] = kh * lax.rsqrt(jnp.sum(kh * kh, axis=-1, keepdims=True) + EPS)
        vv[h] = conv[:, 2 * GDN_QK + h * GDN_DV:2 * GDN_QK + (h + 1) * GDN_DV]
    sl = lambda h: slice(h * GDN_DK, (h + 1) * GDN_DK)
    k8 = {u: jnp.broadcast_to(kn[u[0]][u[1]:u[1] + 1, :], (SUBLANE, GDN_DK)) for u in units}
    s_dec = {(h, i): s_in_ref[i, h] * eg[i:i + 1, sl(h)] for h, i in units}
    kv = {u: _dot(k8[u].astype(BF16), s_dec[u].astype(BF16))[0:1, :] for u in units}
    kcol = {}
    for u in units:
        k_hi, k_mid, k_lo = _split3(k8[u])
        kcol[u] = _dot_tn(k_hi, row0) + _dot_tn(k_mid, row0) + _dot_tn(k_lo, row0)
    s_new = {}
    for h, i in units:
        delta = (vv[h][i:i + 1, :] - kv[h, i]) * beta_bc[i:i + 1, sl(h)]
        s_new[h, i] = s_dec[h, i] + kcol[h, i] * delta
        s_out_ref[i, h] = s_new[h, i]
    for h, i in units:
        q8 = jnp.broadcast_to(qn[h][i:i + 1, :], (SUBLANE, GDN_DK))
        o_ref[i:i + 1, sl(h)] = _dot(q8.astype(BF16), s_new[h, i].astype(BF16))[0:1, :]
    gout = gout_ref[...]
    gz = gz_ref[...]
    for h in range(GDN_HEADS):
        sl = slice(h * GDN_DV, (h + 1) * GDN_DV)
        y_ref[:, sl] = _gdn_gate_out(o_ref[:, sl], gz[:, sl], gout).astype(BF16)


def _gdn_decode(qkv, cprev, misc, gz, s_in, convw, lanep, gout):
    n = qkv.shape[0]
    ns = GDN_DEC_SEQS
    state_spec = pl.BlockSpec((ns, GDN_HEADS, GDN_DK, GDN_DV), lambda i: (i, 0, 0, 0))
    return pl.pallas_call(
        _gdn_decode_body,
        grid=(n // ns,),
        in_specs=[
            pl.BlockSpec((ns, CONV_DIM), lambda i: (i, 0)),
            pl.BlockSpec((CONV_W - 1, ns, CONV_DIM), lambda i: (0, i, 0)),
            pl.BlockSpec((ns, LANE), lambda i: (i, 0)),
            pl.BlockSpec((ns, GDN_V), lambda i: (i, 0)),
            state_spec,
            _full(convw.shape), _full(lanep.shape), _full(gout.shape),
        ],
        out_specs=[pl.BlockSpec((ns, GDN_V), lambda i: (i, 0)), state_spec],
        out_shape=[
            jax.ShapeDtypeStruct((n, GDN_V), BF16),
            jax.ShapeDtypeStruct(s_in.shape, F32),
        ],
        scratch_shapes=[pltpu.VMEM((ns, GDN_V), F32)],
        compiler_params=_cparams(("parallel",)),
        name="gdn_decode",
    )(qkv, cprev, misc, gz, s_in, convw, lanep, gout)


def _rope_padded(x, cos, sina, sinb):
    return (x * cos + pltpu.roll(x, LANE - QK_ROPE // 2, axis=1) * sina
            + pltpu.roll(x, QK_ROPE // 2, axis=1) * sinb)


def _mla_queries(qa, wqb_ref, g_qa, g_nope, g_rope, cos, sina, sinb):
    q = _dot(_rms(qa, g_qa).astype(BF16), wqb_ref[...])
    out = []
    for h in range(MLA_HEADS):
        qn = _rms(q[:, h * QK_PAD:h * QK_PAD + QK_NOPE], g_nope)
        qp = _rms(q[:, h * QK_PAD + QK_NOPE:(h + 1) * QK_PAD], g_rope, QK_ROPE)
        out.append((qn, _rope_padded(qp, cos, sina, sinb)))
    return out


def _mla_latent(kvc, misc, g_kva, g_krope, cos, sina, sinb):
    c = _rms(kvc, g_kva)
    lane = _iota(misc.shape, 1)
    kp = _rms(jnp.where(lane < QK_ROPE, misc, 0.0), g_krope, QK_ROPE)
    return c, _rope_padded(kp, cos, sina, sinb)


def _mla_prep_body(qa_ref, kvc_ref, misc_ref, tab_ref, wqb_ref, wk_ref, wv_ref,
                   g_qa_ref, g_nope_ref, g_rope_ref, g_kva_ref, g_knope_ref, g_krope_ref,
                   q_ref, k_ref, v_ref, c_ref, kpe_ref):
    cos, sina, sinb = tab_ref[0], tab_ref[1], tab_ref[2]
    qs = _mla_queries(qa_ref[...], wqb_ref, g_qa_ref[...], g_nope_ref[...], g_rope_ref[...],
                      cos, sina, sinb)
    c, kp = _mla_latent(kvc_ref[...], misc_ref[...], g_kva_ref[...], g_krope_ref[...], cos, sina, sinb)
    c_ref[...] = c
    kpe_ref[...] = kp[:, :QK_ROPE]
    cb = c.astype(BF16)
    kn = _dot(cb, wk_ref[...])
    v = _dot(cb, wv_ref[...])
    g_kn = g_knope_ref[...]
    kp16 = kp.astype(BF16)
    for h in range(MLA_HEADS):
        qn, qp = qs[h]
        q_ref[h, :, 0:QK_NOPE] = (qn * MLA_SCALE).astype(BF16)
        q_ref[h, :, QK_NOPE:QK_PAD] = (qp * MLA_SCALE).astype(BF16)
        k_ref[h, :, 0:QK_NOPE] = _rms(kn[:, h * QK_NOPE:(h + 1) * QK_NOPE], g_kn).astype(BF16)
        k_ref[h, :, QK_NOPE:QK_PAD] = kp16
        v_ref[h] = v[:, h * V_HEAD:(h + 1) * V_HEAD].astype(BF16)


def _mla_prep(qa, kvc, misc, tabs, wqb, wk, wv, gains, tm):
    t = qa.shape[0]
    n_tab = tabs.shape[1] // tm
    row = lambda n: pl.BlockSpec((tm, n), lambda i: (i, 0))
    head = lambda n: pl.BlockSpec((MLA_HEADS, tm, n), lambda i: (0, i, 0))
    return pl.pallas_call(
        _mla_prep_body,
        grid=(t // tm,),
        in_specs=[row(Q_LORA), row(KV_LORA), row(LANE),
                  pl.BlockSpec((3, tm, LANE), lambda i: (0, i % n_tab, 0)),
                  _full(wqb.shape), _full(wk.shape), _full(wv.shape)]
                 + [_full(g.shape) for g in gains],
        out_specs=[head(QK_PAD), head(QK_PAD), head(V_HEAD), row(KV_LORA), row(QK_ROPE)],
        out_shape=[
            jax.ShapeDtypeStruct((MLA_HEADS, t, QK_PAD), BF16),
            jax.ShapeDtypeStruct((MLA_HEADS, t, QK_PAD), BF16),
            jax.ShapeDtypeStruct((MLA_HEADS, t, V_HEAD), BF16),
            jax.ShapeDtypeStruct((t, KV_LORA), F32),
            jax.ShapeDtypeStruct((t, QK_ROPE), F32),
        ],
        compiler_params=_cparams(("parallel",)),
        name="mla_prep",
    )(qa, kvc, misc, tabs, wqb, wk, wv, *gains)


FLASH_T = 512


def _flash_body(q_ref, k_ref, v_ref, o_ref):
    tq = FLASH_T
    qi = pl.program_id(2)
    q = q_ref[0]

    def block(j, carry, masked):
        m, l, acc = carry
        start = pl.multiple_of(j * tq, tq)
        s = _dot_nt(q, k_ref[0, pl.ds(start, tq), :])
        if masked:
            s = jnp.where(_iota(s.shape, 1) <= _iota(s.shape, 0), s, -jnp.inf)
        m_new = jnp.maximum(m, jnp.max(s, axis=-1, keepdims=True))
        p = jnp.exp(s - m_new)
        corr = jnp.exp(m - m_new)
        l = l * corr + jnp.sum(p, axis=-1, keepdims=True)
        acc = acc * corr + _dot(p.astype(BF16), v_ref[0, pl.ds(start, tq), :])
        return m_new, l, acc

    init = (jnp.full((tq, 1), -jnp.inf, F32), jnp.zeros((tq, 1), F32), jnp.zeros((tq, V_HEAD), F32))
    carry = lax.fori_loop(0, qi, lambda j, c: block(j, c, False), init)
    _, l, acc = block(qi, carry, True)
    o_ref[...] = (acc / l).astype(BF16)


def _mla_flash(q, k, v, b, t):
    tq = FLASH_T
    nq = t // tq
    return pl.pallas_call(
        _flash_body,
        grid=(b, MLA_HEADS, nq),
        in_specs=[
            pl.BlockSpec((1, tq, QK_PAD), lambda i, h, j: (h, i * nq + j, 0)),
            pl.BlockSpec((1, t, QK_PAD), lambda i, h, j: (h, i, 0)),
            pl.BlockSpec((1, t, V_HEAD), lambda i, h, j: (h, i, 0)),
        ],
        out_specs=pl.BlockSpec((tq, V_HEAD), lambda i, h, j: (i * nq + j, h)),
        out_shape=jax.ShapeDtypeStruct((b * t, MLA_HEADS * V_HEAD), BF16),
        compiler_params=_cparams(("parallel", "parallel", "arbitrary")),
        name="mla_flash",
    )(q, k, v)


def _mla_dec_prep_body(qa_ref, kvc_ref, misc_ref, tab_ref, wqb_ref, wk_ref,
                       g_qa_ref, g_nope_ref, g_rope_ref, g_kva_ref, g_knope_ref, g_krope_ref,
                       qabs_ref, qpe_ref, c_ref, kpe_ref):
    cos, sina, sinb = tab_ref[0], tab_ref[1], tab_ref[2]
    qs = _mla_queries(qa_ref[...], wqb_ref, g_qa_ref[...], g_nope_ref[...], g_rope_ref[...],
                      cos, sina, sinb)
    c, kp = _mla_latent(kvc_ref[...], misc_ref[...], g_kva_ref[...], g_krope_ref[...], cos, sina, sinb)
    c_ref[...] = c
    kpe_ref[...] = kp
    g_kn = g_knope_ref[...]
    for h in range(MLA_HEADS):
        qn, qp = qs[h]
        qabs_ref[h] = _dot_nt((qn * g_kn * MLA_SCALE).astype(BF16),
                              wk_ref[:, h * QK_NOPE:(h + 1) * QK_NOPE])
        qpe_ref[h] = qp * MLA_SCALE


def _mla_dec_prep(qa, kvc, misc, tabs, wqb, wk, gains):
    n = qa.shape[0]
    args = (qa, kvc, misc, tabs, wqb, wk) + tuple(gains)
    return pl.pallas_call(
        _mla_dec_prep_body,
        grid=(1,),
        in_specs=[_full(a.shape) for a in args],
        out_specs=[_full((MLA_HEADS, n, KV_LORA)), _full((MLA_HEADS, n, LANE)),
                   _full((n, KV_LORA)), _full((n, LANE))],
        out_shape=[
            jax.ShapeDtypeStruct((MLA_HEADS, n, KV_LORA), F32),
            jax.ShapeDtypeStruct((MLA_HEADS, n, LANE), F32),
            jax.ShapeDtypeStruct((n, KV_LORA), F32),
            jax.ShapeDtypeStruct((n, LANE), F32),
        ],
        compiler_params=_cparams(("arbitrary",)),
        name="mla_dec_prep",
    )(*args)


def _mla_dec_attn_body(pt_ref, *refs):
    npg = PAGES_PER_STEP
    ckv_refs = refs[:npg]
    kpe_refs = refs[npg:2 * npg]
    qabs_ref, qpe_ref, wkt_ref = refs[2 * npg:2 * npg + 3]
    m_out, l_out, acc_out = refs[2 * npg + 3:2 * npg + 6]
    lhs_ref, m_ref, l_ref, acc_ref = refs[2 * npg + 6:]
    del pt_ref
    b = pl.program_id(0)
    g = pl.program_id(1)
    nk = MLA_HEADS * QK_NOPE

    @pl.when((b == 0) & (g == 0))
    def _():
        lhs_ref[0:nk, :] = wkt_ref[...]

    @pl.when(g == 0)
    def _():
        lhs_ref[nk:nk + BF16_ROWS, :] = qabs_ref[0]
        m_ref[...] = jnp.full(m_ref.shape, -jnp.inf, F32)
        l_ref[...] = jnp.zeros(l_ref.shape, F32)
        acc_ref[...] = jnp.zeros(acc_ref.shape, F32)

    keys = npg * PAGE_SIZE
    c_all = jnp.concatenate([r[...] for r in ckv_refs], axis=0).astype(BF16)
    kpt = jnp.concatenate([r[...] for r in kpe_refs], axis=1).astype(BF16)
    kt = _dot_nt(lhs_ref[...], c_all)
    ssq = [jnp.sum(jnp.square(kt[h * QK_NOPE:(h + 1) * QK_NOPE]), axis=0, keepdims=True)
           for h in range(MLA_HEADS)]
    ssq = jnp.concatenate(ssq + [jnp.ones((SUBLANE - MLA_HEADS, keys), F32)], axis=0)
    pe = _dot(qpe_ref[0], kpt)
    s = kt[nk:nk + SUBLANE] * lax.rsqrt(ssq * (1.0 / QK_NOPE) + EPS) + pe[0:SUBLANE]
    m_old = m_ref[:, 0:1]
    m_new = jnp.maximum(m_old, jnp.max(s, axis=-1, keepdims=True))
    p = jnp.exp(s - m_new)
    corr = jnp.exp(m_old - m_new)
    l_ref[...] = l_ref[...] * corr + jnp.sum(p, axis=-1, keepdims=True)
    acc_ref[...] = acc_ref[...] * corr + _dot(p.astype(BF16), c_all)
    m_ref[...] = jnp.broadcast_to(m_new, m_ref.shape)

    @pl.when(g == pl.num_programs(1) - 1)
    def _():
        m_out[0] = m_ref[...]
        l_out[0] = l_ref[...]
        acc_out[0] = acc_ref[...]


def _mla_dec_attn(page_table, ckv_pool, kpe_pool, qabs, qpe, wkt):
    n, n_pages = page_table.shape
    npg = PAGES_PER_STEP
    nk = MLA_HEADS * QK_NOPE

    def page_map(i):
        return lambda b, g, pt: (pt[b * n_pages + g * npg + i], 0, 0)

    seq = lambda shape: pl.BlockSpec((1,) + shape, lambda b, g, pt: (b, 0, 0))
    grid_spec = pltpu.PrefetchScalarGridSpec(
        num_scalar_prefetch=1,
        grid=(n, n_pages // npg),
        in_specs=[pl.BlockSpec((None, PAGE_SIZE, KV_LORA), page_map(i)) for i in range(npg)]
                 + [pl.BlockSpec((None, QK_ROPE, PAGE_SIZE), page_map(i)) for i in range(npg)]
                 + [seq((BF16_ROWS, KV_LORA)), seq((BF16_ROWS, QK_ROPE)),
                    pl.BlockSpec(wkt.shape, lambda b, g, pt: (0, 0))],
        out_specs=[seq((SUBLANE, LANE)), seq((SUBLANE, LANE)), seq((SUBLANE, KV_LORA))],
        scratch_shapes=[
            pltpu.VMEM((nk + BF16_ROWS, KV_LORA), BF16),
            pltpu.VMEM((SUBLANE, LANE), F32),
            pltpu.VMEM((SUBLANE, LANE), F32),
            pltpu.VMEM((SUBLANE, KV_LORA), F32),
        ],
    )
    return pl.pallas_call(
        _mla_dec_attn_body,
        grid_spec=grid_spec,
        out_shape=[
            jax.ShapeDtypeStruct((n, SUBLANE, LANE), F32),
            jax.ShapeDtypeStruct((n, SUBLANE, LANE), F32),
            jax.ShapeDtypeStruct((n, SUBLANE, KV_LORA), F32),
        ],
        compiler_params=_cparams(("arbitrary", "arbitrary")),
        name="mla_dec_attn",
    )(page_table.reshape(-1), *([ckv_pool] * npg), *([kpe_pool] * npg), qabs, qpe, wkt)


def _mla_dec_finish_body(m_ref, l_ref, acc_ref, qabs_ref, qpe_ref, c_ref, kpe_ref, wk_ref, wv_ref,
                         y_ref):
    c = c_ref[...]
    kp = kpe_ref[...]
    cb = c.astype(BF16)
    kn = _dot(cb, wk_ref[...])
    for h in range(MLA_HEADS):
        knh = kn[:, h * QK_NOPE:(h + 1) * QK_NOPE]
        r = lax.rsqrt(jnp.sum(knh * knh, axis=-1, keepdims=True) * (1.0 / QK_NOPE) + EPS)
        s_new = (r * jnp.sum(qabs_ref[h] * c, axis=-1, keepdims=True)
                 + jnp.sum(qpe_ref[h] * kp, axis=-1, keepdims=True))
        m_old = m_ref[h][:, 0:1]
        m_new = jnp.maximum(m_old, s_new)
        p = jnp.exp(s_new - m_new)
        corr = jnp.exp(m_old - m_new)
        l = l_ref[h][:, 0:1] * corr + p
        lat = (acc_ref[h] * corr + p * c) / l
        y_ref[:, h * V_HEAD:(h + 1) * V_HEAD] = _dot(
            lat.astype(BF16), wv_ref[:, h * V_HEAD:(h + 1) * V_HEAD]).astype(BF16)


def _mla_dec_finish(m, l, acc, qabs, qpe, c, kpe, wk, wv):
    n = c.shape[0]
    args = (m, l, acc, qabs, qpe, c, kpe, wk, wv)
    return pl.pallas_call(
        _mla_dec_finish_body,
        grid=(1,),
        in_specs=[_full(a.shape) for a in args],
        out_specs=_full((n, MLA_HEADS * V_HEAD)),
        out_shape=jax.ShapeDtypeStruct((n, MLA_HEADS * V_HEAD), BF16),
        compiler_params=_cparams(("arbitrary",)),
        name="mla_dec_finish",
    )(*args)


def _mix_q_body(x_ref, yg_ref, ym_ref, wmix_ref, g_mem_ref, wq_ref, g_qn_ref, x1_ref, q_ref):
    x1 = (x_ref[...] + _dot(yg_ref[...], wmix_ref[0:GDN_V, :])
          + _dot(ym_ref[...], wmix_ref[GDN_V:, :]))
    x1_ref[...] = x1
    q = _dot(_rms(x1, g_mem_ref[...]).astype(BF16), wq_ref[...])
    g_qn = g_qn_ref[...]
    for h in range(MEM_HEADS):
        sl = slice(h * MEM_HD, (h + 1) * MEM_HD)
        q_ref[:, sl] = (_rms(q[:, sl], g_qn) * (MEM_HD ** -0.5)).astype(BF16)


def _mix_q(x, yg, ym, wmix, g_mem, wq, g_qn, tm):
    t = x.shape[0]
    row = lambda n: pl.BlockSpec((tm, n), lambda i: (i, 0))
    return pl.pallas_call(
        _mix_q_body,
        grid=(t // tm,),
        in_specs=[row(D_MODEL), row(GDN_V), row(MLA_HEADS * V_HEAD), _full(wmix.shape),
                  _full(g_mem.shape), _full(wq.shape), _full(g_qn.shape)],
        out_specs=[row(D_MODEL), row(MEM_DIM)],
        out_shape=[jax.ShapeDtypeStruct((t, D_MODEL), F32), jax.ShapeDtypeStruct((t, MEM_DIM), BF16)],
        compiler_params=_cparams(("parallel",)),
        name="mix_q",
    )(x, yg, ym, wmix, g_mem, wq, g_qn)


def _mem_attn_body(q_ref, k_ref, v_ref, o_ref):
    q = q_ref[0]
    for h in range(MEM_HEADS):
        sl = slice(h * MEM_HD, (h + 1) * MEM_HD)
        s = _dot_nt(q[:, sl], k_ref[0, :, sl].astype(BF16))
        p = jnp.exp(s - jnp.max(s, axis=-1, keepdims=True))
        p = p / jnp.sum(p, axis=-1, keepdims=True)
        o_ref[0, :, sl] = _dot(p.astype(BF16), v_ref[0, :, sl].astype(BF16)).astype(BF16)


def _mem_attn(q, k, v, tq):
    nb, t, _ = q.shape
    m = k.shape[1]
    return pl.pallas_call(
        _mem_attn_body,
        grid=(nb, t // tq),
        in_specs=[pl.BlockSpec((1, tq, MEM_DIM), lambda i, j: (i, j, 0)),
                  pl.BlockSpec((1, m, MEM_DIM), lambda i, j: (i, 0, 0)),
                  pl.BlockSpec((1, m, MEM_DIM), lambda i, j: (i, 0, 0))],
        out_specs=pl.BlockSpec((1, tq, MEM_DIM), lambda i, j: (i, j, 0)),
        out_shape=jax.ShapeDtypeStruct((nb, t, MEM_DIM), BF16),
        compiler_params=_cparams(("parallel", "arbitrary")),
        name="mem_attn",
    )(q, k, v)


def _out_ffn_body(x1_ref, o_ref, wo_ref, g_ffn_ref, wg_ref, wu_ref, wd_ref, y_ref):
    x2 = x1_ref[...] + _dot(o_ref[...], wo_ref[...])
    h = _rms(x2, g_ffn_ref[...]).astype(BF16)
    act = (_silu(_dot(h, wg_ref[...])) * _dot(h, wu_ref[...])).astype(BF16)
    y_ref[...] = x2 + _dot(act, wd_ref[...])


def _out_ffn(x1, o, wo, g_ffn, wg, wu, wd, tm):
    t = x1.shape[0]
    row = lambda n: pl.BlockSpec((tm, n), lambda i: (i, 0))
    const = lambda a: pl.BlockSpec(a.shape, lambda i: (0,) * a.ndim, pipeline_mode=pl.Buffered(1))
    return pl.pallas_call(
        _out_ffn_body,
        grid=(t // tm,),
        in_specs=[row(D_MODEL), row(MEM_DIM), const(wo), const(g_ffn), const(wg), const(wu), const(wd)],
        out_specs=row(D_MODEL),
        out_shape=jax.ShapeDtypeStruct((t, D_MODEL), F32),
        compiler_params=_cparams(("parallel",)),
        name="out_ffn",
    )(x1, o, wo, g_ffn, wg, wu, wd)


def _mem_kv_body(mem_ref, wk_ref, wv_ref, g_ref, k_ref, v_ref):
    mb = mem_ref[...].astype(BF16)
    k = _dot(mb, wk_ref[...])
    g = g_ref[...]
    for h in range(MEM_HEADS):
        sl = slice(h * MEM_HD, (h + 1) * MEM_HD)
        k_ref[:, sl] = _rms(k[:, sl], g)
    v_ref[...] = _dot(mb, wv_ref[...])


def _mem_kv(mem, wk, wv, g, tm):
    t = mem.shape[0]
    row = lambda n: pl.BlockSpec((tm, n), lambda i: (i, 0))
    return pl.pallas_call(
        _mem_kv_body,
        grid=(t // tm,),
        in_specs=[row(D_MODEL), _full(wk.shape), _full(wv.shape), _full(g.shape)],
        out_specs=[row(MEM_DIM), row(MEM_DIM)],
        out_shape=[jax.ShapeDtypeStruct((t, MEM_DIM), F32)] * 2,
        compiler_params=_cparams(("parallel",)),
        name="mem_kv",
    )(mem, wk, wv, g)


def _row(v):
    return v.reshape(1, -1).astype(F32)


def _pad_lanes(v, width=LANE):
    return jnp.pad(v, ((0, 0), (0, width - v.shape[1])))


def _rope_tables(pos, rows):
    half = QK_ROPE // 2
    inv = ROPE_THETA ** (-jnp.arange(half, dtype=F32) / half)
    ang = pos.astype(F32)[:, None] * inv[None, :]
    cos, sin = jnp.cos(ang), jnp.sin(ang)
    zero = jnp.zeros_like(cos)
    tabs = jnp.stack([
        _pad_lanes(jnp.concatenate([cos, cos], axis=1)),
        _pad_lanes(jnp.concatenate([-sin, zero], axis=1)),
        _pad_lanes(jnp.concatenate([zero, sin], axis=1)),
    ])
    return jnp.broadcast_to(tabs, (3, rows, LANE)) if tabs.shape[1] == 1 else tabs


def kernel(x_prompt, x_sample, cache_mla_ckv, cache_mla_kpe, cache_mem_k, cache_mem_v, state_gdn_S, state_gdn_conv, page_table, mem_prompt, norm_mix_g, w_in, gdn_conv_w, gdn_A_log, gdn_dt_bias, gdn_out_norm_g, mla_q_a_norm_g, mla_w_q_b, mla_kv_a_norm_g, mla_w_kv_b, mla_qn_nope_g, mla_qn_rope_g, mla_kn_nope_g, mla_kn_rope_g, w_mix_out, norm_mem_g, mem_wq, mem_wk, mem_wv, mem_wo, mem_qn_g, mem_kn_g, norm_ffn_g, ffn_w_gate, ffn_w_up, ffn_w_down):
    depth = w_in.shape[0]
    assert depth == 1, "single-layer trunk"
    bsz, seq, _ = x_prompt.shape
    nseq, dseq, _ = x_sample.shape
    assert dseq == 1, "one new token per decode sequence"
    past_len = page_table.shape[1] * PAGE_SIZE
    n_tok = bsz * seq

    w = w_in[0]
    o_gz = CONV_DIM
    o_b = o_gz + GDN_V
    o_a = o_b + GDN_HEADS
    o_qa = o_a + GDN_HEADS
    o_c = o_qa + Q_LORA
    o_kpe = o_c + KV_LORA
    misc_w = _pad_lanes(jnp.concatenate([w[:, o_kpe:o_kpe + QK_ROPE], w[:, o_b:o_qa]], axis=1))
    w_in_p = jnp.concatenate([w[:, :o_b], w[:, o_qa:o_kpe], misc_w], axis=1).astype(BF16)

    wqb = mla_w_q_b[0].reshape(Q_LORA, MLA_HEADS, QK_NOPE + QK_ROPE)
    wqb = jnp.pad(wqb, ((0, 0), (0, 0), (0, QK_PAD - QK_NOPE - QK_ROPE)))
    wqb = wqb.reshape(Q_LORA, MLA_HEADS * QK_PAD).astype(BF16)
    wkvb = mla_w_kv_b[0].reshape(KV_LORA, MLA_HEADS, QK_NOPE + V_HEAD)
    wk = wkvb[:, :, :QK_NOPE].reshape(KV_LORA, MLA_HEADS * QK_NOPE).astype(BF16)
    wv = wkvb[:, :, QK_NOPE:].reshape(KV_LORA, MLA_HEADS * V_HEAD).astype(BF16)
    wkt = wk.T

    lanep = jnp.zeros((2, LANE), F32)
    lanep = lanep.at[0, MISC_A:MISC_A + GDN_HEADS].set(gdn_A_log[0])
    lanep = lanep.at[1, MISC_A:MISC_A + GDN_HEADS].set(gdn_dt_bias[0])
    rowp = jnp.zeros((2 * SUBLANE,), F32)
    rowp = rowp.at[GDN_HEADS:2 * GDN_HEADS].set(gdn_A_log[0])
    rowp = rowp.at[SUBLANE + GDN_HEADS:SUBLANE + 2 * GDN_HEADS].set(gdn_dt_bias[0])
    rowp = jnp.broadcast_to(rowp[:, None], (2 * SUBLANE, GDN_TB))

    g_mix = _row(norm_mix_g[0])
    g_out = _row(gdn_out_norm_g[0])
    mla_gains = (_row(mla_q_a_norm_g[0]), _row(mla_qn_nope_g[0]), _pad_lanes(_row(mla_qn_rope_g[0])),
                 _row(mla_kv_a_norm_g[0]), _row(mla_kn_nope_g[0]), _pad_lanes(_row(mla_kn_rope_g[0])))
    wmix = w_mix_out[0].astype(BF16)
    wq_mem = mem_wq[0].astype(BF16)
    wk_mem = mem_wk[0].astype(BF16)
    wv_mem = mem_wv[0].astype(BF16)
    wo_mem = mem_wo[0].astype(BF16)
    wg = ffn_w_gate[0].astype(BF16)
    wu = ffn_w_up[0].astype(BF16)
    wd = ffn_w_down[0].astype(BF16)
    g_mem = _row(norm_mem_g[0])
    g_ffn = _row(norm_ffn_g[0])
    g_qn = _row(mem_qn_g[0])
    g_kn = _row(mem_kn_g[0])
    conv_w = gdn_conv_w[0]

    xp = x_prompt.reshape(n_tok, D_MODEL)
    qkv, gz, qa, kvc, misc = _in_proj(xp, g_mix, w_in_p, 256)
    qkv3 = qkv.reshape(bsz, seq, CONV_DIM)
    grow = jnp.swapaxes(misc.reshape(bsz, seq, LANE)[:, :, MISC_B:MISC_B + SUBLANE], 1, 2)
    y_gdn, p_s = _gdn_prompt(qkv3, misc.reshape(bsz, seq, LANE), grow, gz.reshape(bsz, seq, GDN_V),
                             conv_w, lanep, rowp, g_out)
    p_conv = qkv3[:, seq - (CONV_W - 1):, :]

    tabs_p = _rope_tables(jnp.arange(seq), seq)
    q_full, k_full, v_full, p_c, p_kpe = _mla_prep(qa, kvc, misc, tabs_p, wqb, wk, wv, mla_gains, 256)
    y_mla = _mla_flash(q_full, k_full, v_full, bsz, seq)

    mem_k, mem_v = _mem_kv(mem_prompt.reshape(-1, D_MODEL), wk_mem, wv_mem, g_kn, 256)
    n_mem = mem_prompt.shape[1]
    x1, q_mem = _mix_q(xp, y_gdn.reshape(n_tok, GDN_V), y_mla, wmix, g_mem, wq_mem, g_qn, 256)
    o_mem = _mem_attn(q_mem.reshape(bsz, seq, MEM_DIM), mem_k.reshape(bsz, n_mem, MEM_DIM),
                      mem_v.reshape(bsz, n_mem, MEM_DIM), 256)
    y_prompt = _out_ffn(x1, o_mem.reshape(n_tok, MEM_DIM), wo_mem, g_ffn, wg, wu, wd, 256)

    xs = x_sample.reshape(nseq, D_MODEL)
    qkv_s, gz_s, qa_s, kvc_s, misc_s = _in_proj(xs, g_mix, w_in_p, nseq)
    conv_prev = state_gdn_conv[0]
    y_gdn_s, s_new = _gdn_decode(qkv_s, jnp.swapaxes(conv_prev, 0, 1), misc_s, gz_s, state_gdn_S[0],
                                 conv_w, lanep, g_out)
    s_conv = jnp.concatenate([conv_prev[:, 1:, :], qkv_s[:, None, :]], axis=1)

    tabs_s = _rope_tables(jnp.full((1,), past_len), nseq)
    qabs, qpe, c_new, kpe_new = _mla_dec_prep(qa_s, kvc_s, misc_s, tabs_s, wqb, wk, mla_gains)
    pad_rows = lambda a: jnp.pad(jnp.swapaxes(a, 0, 1), ((0, 0), (0, BF16_ROWS - MLA_HEADS), (0, 0)))
    m_run, l_run, acc_run = _mla_dec_attn(
        page_table, cache_mla_ckv[0], jnp.swapaxes(cache_mla_kpe[0], 1, 2),
        pad_rows(qabs).astype(BF16), pad_rows(qpe[:, :, :QK_ROPE]).astype(BF16), wkt)
    heads_first = lambda a: jnp.swapaxes(a[:, :MLA_HEADS], 0, 1)
    y_mla_s = _mla_dec_finish(heads_first(m_run), heads_first(l_run), heads_first(acc_run),
                              qabs, qpe, c_new, kpe_new, wk, wv)

    x1_s, q_mem_s = _mix_q(xs, y_gdn_s, y_mla_s, wmix, g_mem, wq_mem, g_qn, nseq)
    q_rep = jnp.broadcast_to(q_mem_s[:, None, :], (nseq, BF16_ROWS, MEM_DIM))
    o_mem_s = _mem_attn(q_rep, cache_mem_k[0].reshape(nseq, -1, MEM_DIM),
                        cache_mem_v[0].reshape(nseq, -1, MEM_DIM), BF16_ROWS)[:, 0, :]
    y_sample = _out_ffn(x1_s, o_mem_s, wo_mem, g_ffn, wg, wu, wd, nseq)

    return (
        y_prompt.reshape(bsz, seq, D_MODEL),
        y_sample.reshape(nseq, 1, D_MODEL),
        p_c.reshape(1, bsz, seq, KV_LORA),
        p_kpe.reshape(1, bsz, seq, QK_ROPE),
        mem_k.reshape(1, bsz, n_mem, MEM_HEADS, MEM_HD),
        mem_v.reshape(1, bsz, n_mem, MEM_HEADS, MEM_HD),
        p_s[None],
        p_conv[None],
        c_new.reshape(1, nseq, 1, KV_LORA),
        kpe_new[:, :QK_ROPE].reshape(1, nseq, 1, QK_ROPE),
        s_new[None],
        s_conv[None],
    )
```

```python
import functools

import jax
import jax.numpy as jnp
from jax import lax
from jax.experimental import pallas as pl
from jax.experimental.pallas import tpu as pltpu

F32 = jnp.float32
BF16 = jnp.bfloat16

D_MODEL = 1024
GDN_HEADS = 4
GDN_DK = 128
GDN_DV = 128
CONV_W = 4
CHUNK = 64
MLA_HEADS = 4
Q_LORA = 384
KV_LORA = 256
QK_NOPE = 128
QK_ROPE = 64
V_HEAD = 128
ROPE_THETA = 10000.0
PAGE_SIZE = 128
MEM_HEADS = 4
MEM_HD = 128
EPS = 1e-6
GDN_QK = GDN_HEADS * GDN_DK
GDN_V = GDN_HEADS * GDN_DV
CONV_DIM = 2 * GDN_QK + GDN_V
MLA_SCALE = (QK_NOPE + QK_ROPE) ** -0.5
MEM_DIM = MEM_HEADS * MEM_HD

LANE = 128
SUBLANE = 8
BF16_ROWS = 16
QK_PAD = 2 * LANE
MISC_B = QK_ROPE
MISC_A = QK_ROPE + GDN_HEADS
VMEM_LIMIT = 56 * 1024 * 1024

GDN_TB = 2 * CHUNK
PAGES_PER_STEP = 16
DEC_RING = 3


def _cparams(sem):
    return pltpu.CompilerParams(dimension_semantics=sem, vmem_limit_bytes=VMEM_LIMIT)


def _full(shape):
    n = len(shape)
    return pl.BlockSpec(shape, lambda *_: (0,) * n)


def _dot(a, b):
    return jnp.dot(a, b, preferred_element_type=F32)


def _dot_nt(a, b):
    return lax.dot_general(a, b, (((1,), (1,)), ((), ())), preferred_element_type=F32)


def _dot_tn(a, b):
    return lax.dot_general(a, b, (((0,), (0,)), ((), ())), preferred_element_type=F32)


def _split3(x):
    hi = x.astype(BF16)
    r = x - hi.astype(F32)
    mid = r.astype(BF16)
    lo = (r - mid.astype(F32)).astype(BF16)
    return hi, mid, lo


def _dot01_left(m01, x):
    hi, mid, lo = _split3(x)
    return _dot(m01, hi) + _dot(m01, mid) + _dot(m01, lo)


def _dot01_right(x, m01):
    hi, mid, lo = _split3(x)
    return _dot(hi, m01) + _dot(mid, m01) + _dot(lo, m01)


def _rms(x, g, n=None):
    n = x.shape[-1] if n is None else n
    ms = jnp.sum(x * x, axis=-1, keepdims=True) * (1.0 / n)
    return x * lax.rsqrt(ms + EPS) * g


def _silu(x):
    return x * jax.nn.sigmoid(x)


def _softplus(x):
    return jnp.maximum(x, 0.0) + jnp.log1p(jnp.exp(-jnp.abs(x)))


def _iota(shape, dim):
    return lax.broadcasted_iota(jnp.int32, shape, dim)


def _block_id(i, size):
    assert size & (size - 1) == 0
    return lax.shift_right_logical(i, size.bit_length() - 1)


def _in_proj_body(x_ref, g_ref, w_ref, qkv_ref, gz_ref, qa_ref, kvc_ref, misc_ref):
    h = _rms(x_ref[...], g_ref[...]).astype(BF16)
    off = 0
    for ref in (qkv_ref, gz_ref, qa_ref, kvc_ref, misc_ref):
        n = ref.shape[-1]
        ref[...] = _dot(h, w_ref[:, off:off + n])
        off += n


def _in_proj(x, g, w, tm):
    t = x.shape[0]
    widths = (CONV_DIM, GDN_V, Q_LORA, KV_LORA, LANE)
    return pl.pallas_call(
        _in_proj_body,
        grid=(t // tm,),
        in_specs=[pl.BlockSpec((tm, D_MODEL), lambda i: (i, 0)), _full(g.shape), _full(w.shape)],
        out_specs=[pl.BlockSpec((tm, n), lambda i: (i, 0)) for n in widths],
        out_shape=[jax.ShapeDtypeStruct((t, n), F32) for n in widths],
        compiler_params=_cparams(("parallel",)),
        name="in_proj",
    )(x, g, w)


def _gate_lane_values(misc, lanep):
    lane = _iota(misc.shape, 1)
    is_a = (lane >= MISC_A) & (lane < MISC_A + GDN_HEADS)
    beta_l = jax.nn.sigmoid(misc)
    g_l = -jnp.exp(lanep[0:1, :]) * _softplus(jnp.where(is_a, misc, 0.0) + lanep[1:2, :])
    return beta_l, g_l


def _expand_matrix(first_lane):
    r = _iota((LANE, GDN_V), 0)
    c = _iota((LANE, GDN_V), 1)
    return jnp.where(r == first_lane + _block_id(c, GDN_DV), 1.0, 0.0).astype(BF16)


def _gdn_gate_out(o, z, gout):
    return _rms(o, gout) * _silu(z)


def _tri_inverse(a_list, eye, same16, same32):
    def mm(xs, ys):
        return [_dot(x.astype(BF16), y.astype(BF16)) for x, y in zip(xs, ys)]

    def add(ts, us):
        return [t + u for t, u in zip(ts, us)]

    def sub(ts, us):
        return [t - u for t, u in zip(ts, us)]

    ad = [jnp.where(same16, a, 0.0) for a in a_list]
    t = [eye - x for x in ad]
    p = mm(ad, ad)
    t = add(t, mm(t, p))
    p = mm(p, p)
    t = add(t, mm(t, p))
    p = mm(p, p)
    t = add(t, mm(t, p))
    b1 = [jnp.where(same32, a, 0.0) - x for a, x in zip(a_list, ad)]
    t = sub(t, mm(t, mm(b1, t)))
    b2 = [jnp.where(same32, 0.0, a) for a in a_list]
    t = sub(t, mm(t, mm(b2, t)))
    return t


def _gdn_prompt_body(qkv_ref, misc_ref, grow_ref, gz_ref, convw_ref, lanep_ref, rowp_ref, gout_ref,
                     y_ref, s_out_ref, xe_ref, s_ref):
    tb = GDN_TB
    t_idx = pl.program_id(1)

    @pl.when(t_idx == 0)
    def _():
        s_ref[...] = jnp.zeros(s_ref.shape, F32)
        xe_ref[0:SUBLANE, :] = jnp.zeros((SUBLANE, CONV_DIM), F32)

    x = qkv_ref[0]
    xe_ref[SUBLANE:SUBLANE + tb, :] = x
    w = convw_ref[...]
    conv = x * w[CONV_W - 1:CONV_W, :]
    for i in range(CONV_W - 1):
        lo = SUBLANE - (CONV_W - 1) + i
        conv = conv + xe_ref[lo:lo + tb, :] * w[i:i + 1, :]
    xe_ref[0:SUBLANE, :] = x[tb - SUBLANE:tb, :]

    ri = _iota((tb, tb), 0)
    ci = _iota((tb, tb), 1)
    same64 = _block_id(ri, CHUNK) == _block_id(ci, CHUNK)
    lower = same64 & (ci <= ri)
    strict = same64 & (ci < ri)
    same16 = _block_id(ri, 16) == _block_id(ci, 16)
    same32 = _block_id(ri, 32) == _block_id(ci, 32)
    eye = jnp.where(ri == ci, 1.0, 0.0)
    l_incl = jnp.where(lower, 1.0, 0.0).astype(BF16)
    u_incl = jnp.where(same64 & (ri <= ci), 1.0, 0.0).astype(BF16)
    u_strict = jnp.where(same64 & (ci > ri), 1.0, 0.0).astype(BF16)

    beta_l, g_l = _gate_lane_values(misc_ref[0], lanep_ref[...])
    e_b = _expand_matrix(MISC_B)
    e_a = _expand_matrix(MISC_A)
    beta_bc = _dot01_right(beta_l, e_b)
    gc_bc = _dot01_right(_dot01_left(l_incl, g_l), e_a)
    gt_bc = _dot01_right(_dot01_left(u_strict, g_l), e_a)
    egc = jnp.exp(gc_bc)
    etail = jnp.exp(gt_bc)

    grow = grow_ref[0]
    rowi = _iota(grow.shape, 0)
    g_r = -jnp.exp(rowp_ref[0:SUBLANE, :]) * _softplus(
        jnp.where(rowi >= GDN_HEADS, grow, 0.0) + rowp_ref[SUBLANE:2 * SUBLANE, :])
    gcr = _dot01_right(g_r, u_incl)

    gout = gout_ref[...]
    gz = gz_ref[0]
    heads = range(GDN_HEADS)
    sls = [slice(h * GDN_DK, (h + 1) * GDN_DK) for h in heads]
    qh, kh, kb, vb, dm = [], [], [], [], []
    for h in heads:
        q = _silu(conv[:, h * GDN_DK:(h + 1) * GDN_DK])
        k = _silu(conv[:, GDN_QK + h * GDN_DK:GDN_QK + (h + 1) * GDN_DK])
        v = _silu(conv[:, 2 * GDN_QK + h * GDN_DV:2 * GDN_QK + (h + 1) * GDN_DV])
        q = q * lax.rsqrt(jnp.sum(q * q, axis=-1, keepdims=True) + EPS) * (GDN_DK ** -0.5)
        k = k * lax.rsqrt(jnp.sum(k * k, axis=-1, keepdims=True) + EPS)
        bb = beta_bc[:, sls[h]]
        qh.append(q)
        kh.append(k)
        kb.append(k * bb)
        vb.append(v * bb)
        diff = gc_bc[:, sls[h]] - gcr[GDN_HEADS + h:GDN_HEADS + h + 1, :]
        dm.append(jnp.where(lower, jnp.exp(jnp.where(lower, diff, 0.0)), 0.0))
    p = [_dot_nt(jnp.concatenate([kb[h], qh[h]], axis=0).astype(BF16), kh[h].astype(BF16))
         for h in heads]
    a_mat = [jnp.where(strict, p[h][:tb] * dm[h], 0.0) for h in heads]
    attn = [(p[h][tb:] * dm[h]).astype(BF16) for h in heads]
    t_mat = _tri_inverse(a_mat, eye, same16, same32)
    uw = [_dot(t_mat[h].astype(BF16),
               jnp.concatenate([vb[h], kb[h] * egc[:, sls[h]]], axis=1).astype(BF16)) for h in heads]
    qd = [qh[h] * egc[:, sls[h]] for h in heads]
    kt = [(kh[h] * etail[:, sls[h]]).astype(BF16) for h in heads]
    s = [s_ref[h] for h in heads]
    o_inter = [[] for _ in heads]
    v_new = [[] for _ in heads]
    for c in range(tb // CHUNK):
        rs = slice(c * CHUNK, (c + 1) * CHUNK)
        wq = [_dot(jnp.concatenate([uw[h][rs, GDN_DV:], qd[h][rs]], axis=0).astype(BF16),
                   s[h].astype(BF16)) for h in heads]
        vn = [uw[h][rs, :GDN_DV] - wq[h][:CHUNK] for h in heads]
        egl = [jnp.exp(gc_bc[(c + 1) * CHUNK - 1:(c + 1) * CHUNK, sls[h]]) for h in heads]
        s = [s[h] * egl[h] + _dot_tn(kt[h][rs], vn[h].astype(BF16)) for h in heads]
        for h in heads:
            o_inter[h].append(wq[h][CHUNK:])
            v_new[h].append(vn[h])
    o = [jnp.concatenate(o_inter[h], axis=0)
         + _dot(attn[h], jnp.concatenate(v_new[h], axis=0).astype(BF16)) for h in heads]
    for h in heads:
        s_ref[h] = s[h]
        y_ref[0, :, sls[h]] = _gdn_gate_out(o[h], gz[:, sls[h]], gout).astype(BF16)

    @pl.when(t_idx == pl.num_programs(1) - 1)
    def _():
        s_out_ref[0] = s_ref[...]


def _gdn_prompt(qkv, misc, grow, gz, convw, lanep, rowp, gout):
    b, t, _ = qkv.shape
    tb = GDN_TB
    return pl.pallas_call(
        _gdn_prompt_body,
        grid=(b, t // tb),
        in_specs=[
            pl.BlockSpec((1, tb, CONV_DIM), lambda i, j: (i, j, 0)),
            pl.BlockSpec((1, tb, LANE), lambda i, j: (i, j, 0)),
            pl.BlockSpec((1, SUBLANE, tb), lambda i, j: (i, 0, j)),
            pl.BlockSpec((1, tb, GDN_V), lambda i, j: (i, j, 0)),
            _full(convw.shape), _full(lanep.shape), _full(rowp.shape), _full(gout.shape),
        ],
        out_specs=[
            pl.BlockSpec((1, tb, GDN_V), lambda i, j: (i, j, 0)),
            pl.BlockSpec((1, GDN_HEADS, GDN_DK, GDN_DV), lambda i, j: (i, 0, 0, 0)),
        ],
        out_shape=[
            jax.ShapeDtypeStruct((b, t, GDN_V), BF16),
            jax.ShapeDtypeStruct((b, GDN_HEADS, GDN_DK, GDN_DV), F32),
        ],
        scratch_shapes=[
            pltpu.VMEM((SUBLANE + tb, CONV_DIM), F32),
            pltpu.VMEM((GDN_HEADS, GDN_DK, GDN_DV), F32),
        ],
        compiler_params=_cparams(("parallel", "arbitrary")),
        name="gdn_prompt",
    )(qkv, misc, grow, gz, convw, lanep, rowp, gout)


GDN_DEC_SEQS = 8


def _gdn_decode_body(qkv_ref, cprev_ref, misc_ref, gz_ref, s_in_ref, convw_ref, lanep_ref, gout_ref,
                     y_ref, s_out_ref, o_ref):
    ns = GDN_DEC_SEQS
    w = convw_ref[...]
    conv = qkv_ref[...] * w[CONV_W - 1:CONV_W, :]
    for i in range(CONV_W - 1):
        conv = conv + cprev_ref[i] * w[i:i + 1, :]
    conv = _silu(conv)
    beta_l, g_l = _gate_lane_values(misc_ref[...], lanep_ref[...])
    beta_bc = _dot01_right(beta_l, _expand_matrix(MISC_B))
    eg = jnp.exp(_dot01_right(g_l, _expand_matrix(MISC_A)))
    row0 = jnp.where(_iota((SUBLANE, LANE), 0) == 0, 1.0, 0.0).astype(BF16)
    units = [(h, i) for h in range(GDN_HEADS) for i in range(ns)]
    qn, kn, vv = {}, {}, {}
    for h in range(GDN_HEADS):
        qh = conv[:, h * GDN_DK:(h + 1) * GDN_DK]
        kh = conv[:, GDN_QK + h * GDN_DK:GDN_QK + (h + 1) * GDN_DK]
        qn[h] = qh * lax.rsqrt(jnp.sum(qh * qh, axis=-1, keepdims=True) + EPS) * (GDN_DK ** -0.5)
        kn[h] = kh * lax.rsqrt(jnp.sum(kh * kh, axis=-1, keepdims=True) + EPS)
        vv[h] = conv[:, 2 * GDN_QK + h * GDN_DV:2 * GDN_QK + (h + 1) * GDN_DV]
    sl = lambda h: slice(h * GDN_DK, (h + 1) * GDN_DK)
    k8 = {u: jnp.broadcast_to(kn[u[0]][u[1]:u[1] + 1, :], (SUBLANE, GDN_DK)) for u in units}
    s_dec = {(h, i): s_in_ref[i, h] * eg[i:i + 1, sl(h)] for h, i in units}
    kv = {u: _dot(k8[u].astype(BF16), s_dec[u].astype(BF16))[0:1, :] for u in units}
    kcol = {}
    for u in units:
        k_hi, k_mid, k_lo = _split3(k8[u])
        kcol[u] = _dot_tn(k_hi, row0) + _dot_tn(k_mid, row0) + _dot_tn(k_lo, row0)
    s_new = {}
    for h, i in units:
        delta = (vv[h][i:i + 1, :] - kv[h, i]) * beta_bc[i:i + 1, sl(h)]
        s_new[h, i] = s_dec[h, i] + kcol[h, i] * delta
        s_out_ref[i, h] = s_new[h, i]
    for h, i in units:
        q8 = jnp.broadcast_to(qn[h][i:i + 1, :], (SUBLANE, GDN_DK))
        o_ref[i:i + 1, sl(h)] = _dot(q8.astype(BF16), s_new[h, i].astype(BF16))[0:1, :]
    gout = gout_ref[...]
    gz = gz_ref[...]
    for h in range(GDN_HEADS):
        sl = slice(h * GDN_DV, (h + 1) * GDN_DV)
        y_ref[:, sl] = _gdn_gate_out(o_ref[:, sl], gz[:, sl], gout).astype(BF16)


def _gdn_decode(qkv, cprev, misc, gz, s_in, convw, lanep, gout):
    n = qkv.shape[0]
    ns = GDN_DEC_SEQS
    state_spec = pl.BlockSpec((ns, GDN_HEADS, GDN_DK, GDN_DV), lambda i: (i, 0, 0, 0))
    return pl.pallas_call(
        _gdn_decode_body,
        grid=(n // ns,),
        in_specs=[
            pl.BlockSpec((ns, CONV_DIM), lambda i: (i, 0)),
            pl.BlockSpec((CONV_W - 1, ns, CONV_DIM), lambda i: (0, i, 0)),
            pl.BlockSpec((ns, LANE), lambda i: (i, 0)),
            pl.BlockSpec((ns, GDN_V), lambda i: (i, 0)),
            state_spec,
            _full(convw.shape), _full(lanep.shape), _full(gout.shape),
        ],
        out_specs=[pl.BlockSpec((ns, GDN_V), lambda i: (i, 0)), state_spec],
        out_shape=[
            jax.ShapeDtypeStruct((n, GDN_V), BF16),
            jax.ShapeDtypeStruct(s_in.shape, F32),
        ],
        scratch_shapes=[pltpu.VMEM((ns, GDN_V), F32)],
        compiler_params=_cparams(("parallel",)),
        name="gdn_decode",
    )(qkv, cprev, misc, gz, s_in, convw, lanep, gout)


def _rope_padded(x, cos, sina, sinb):
    return (x * cos + pltpu.roll(x, LANE - QK_ROPE // 2, axis=1) * sina
            + pltpu.roll(x, QK_ROPE // 2, axis=1) * sinb)


def _mla_queries(qa, wqb_ref, g_qa, g_nope, g_rope, cos, sina, sinb):
    q = _dot(_rms(qa, g_qa).astype(BF16), wqb_ref[...])
    out = []
    for h in range(MLA_HEADS):
        qn = _rms(q[:, h * QK_PAD:h * QK_PAD + QK_NOPE], g_nope)
        qp = _rms(q[:, h * QK_PAD + QK_NOPE:(h + 1) * QK_PAD], g_rope, QK_ROPE)
        out.append((qn, _rope_padded(qp, cos, sina, sinb)))
    return out


def _mla_latent(kvc, misc, g_kva, g_krope, cos, sina, sinb):
    c = _rms(kvc, g_kva)
    lane = _iota(misc.shape, 1)
    kp = _rms(jnp.where(lane < QK_ROPE, misc, 0.0), g_krope, QK_ROPE)
    return c, _rope_padded(kp, cos, sina, sinb)


def _mla_prep_body(qa_ref, kvc_ref, misc_ref, tab_ref, wqb_ref, wk_ref, wv_ref,
                   g_qa_ref, g_nope_ref, g_rope_ref, g_kva_ref, g_knope_ref, g_krope_ref,
                   q_ref, k_ref, v_ref, c_ref, kpe_ref):
    cos, sina, sinb = tab_ref[0], tab_ref[1], tab_ref[2]
    qs = _mla_queries(qa_ref[...], wqb_ref, g_qa_ref[...], g_nope_ref[...], g_rope_ref[...],
                      cos, sina, sinb)
    c, kp = _mla_latent(kvc_ref[...], misc_ref[...], g_kva_ref[...], g_krope_ref[...], cos, sina, sinb)
    c_ref[...] = c
    kpe_ref[...] = kp[:, :QK_ROPE]
    cb = c.astype(BF16)
    kn = _dot(cb, wk_ref[...])
    v = _dot(cb, wv_ref[...])
    g_kn = g_knope_ref[...]
    kp16 = kp.astype(BF16)
    for h in range(MLA_HEADS):
        qn, qp = qs[h]
        q_ref[h, :, 0:QK_NOPE] = (qn * MLA_SCALE).astype(BF16)
        q_ref[h, :, QK_NOPE:QK_PAD] = (qp * MLA_SCALE).astype(BF16)
        k_ref[h, :, 0:QK_NOPE] = _rms(kn[:, h * QK_NOPE:(h + 1) * QK_NOPE], g_kn).astype(BF16)
        k_ref[h, :, QK_NOPE:QK_PAD] = kp16
        v_ref[h] = v[:, h * V_HEAD:(h + 1) * V_HEAD].astype(BF16)


def _mla_prep(qa, kvc, misc, tabs, wqb, wk, wv, gains, tm):
    t = qa.shape[0]
    n_tab = tabs.shape[1] // tm
    row = lambda n: pl.BlockSpec((tm, n), lambda i: (i, 0))
    head = lambda n: pl.BlockSpec((MLA_HEADS, tm, n), lambda i: (0, i, 0))
    return pl.pallas_call(
        _mla_prep_body,
        grid=(t // tm,),
        in_specs=[row(Q_LORA), row(KV_LORA), row(LANE),
                  pl.BlockSpec((3, tm, LANE), lambda i: (0, i % n_tab, 0)),
                  _full(wqb.shape), _full(wk.shape), _full(wv.shape)]
                 + [_full(g.shape) for g in gains],
        out_specs=[head(QK_PAD), head(QK_PAD), head(V_HEAD), row(KV_LORA), row(QK_ROPE)],
        out_shape=[
            jax.ShapeDtypeStruct((MLA_HEADS, t, QK_PAD), BF16),
            jax.ShapeDtypeStruct((MLA_HEADS, t, QK_PAD), BF16),
            jax.ShapeDtypeStruct((MLA_HEADS, t, V_HEAD), BF16),
            jax.ShapeDtypeStruct((t, KV_LORA), F32),
            jax.ShapeDtypeStruct((t, QK_ROPE), F32),
        ],
        compiler_params=_cparams(("parallel",)),
        name="mla_prep",
    )(qa, kvc, misc, tabs, wqb, wk, wv, *gains)


FLASH_T = 512


def _flash_body(q_ref, k_ref, v_ref, o_ref):
    tq = FLASH_T
    qi = pl.program_id(2)
    q = q_ref[0]

    def block(j, carry, masked):
        m, l, acc = carry
        start = pl.multiple_of(j * tq, tq)
        s = _dot_nt(q, k_ref[0, pl.ds(start, tq), :])
        if masked:
            s = jnp.where(_iota(s.shape, 1) <= _iota(s.shape, 0), s, -jnp.inf)
        m_new = jnp.maximum(m, jnp.max(s, axis=-1, keepdims=True))
        p = jnp.exp(s - m_new)
        corr = jnp.exp(m - m_new)
        l = l * corr + jnp.sum(p, axis=-1, keepdims=True)
        acc = acc * corr + _dot(p.astype(BF16), v_ref[0, pl.ds(start, tq), :])
        return m_new, l, acc

    init = (jnp.full((tq, 1), -jnp.inf, F32), jnp.zeros((tq, 1), F32), jnp.zeros((tq, V_HEAD), F32))
    carry = lax.fori_loop(0, qi, lambda j, c: block(j, c, False), init)
    _, l, acc = block(qi, carry, True)
    o_ref[...] = (acc / l).astype(BF16)


def _mla_flash(q, k, v, b, t):
    tq = FLASH_T
    nq = t // tq
    return pl.pallas_call(
        _flash_body,
        grid=(b, MLA_HEADS, nq),
        in_specs=[
            pl.BlockSpec((1, tq, QK_PAD), lambda i, h, j: (h, i * nq + j, 0)),
            pl.BlockSpec((1, t, QK_PAD), lambda i, h, j: (h, i, 0)),
            pl.BlockSpec((1, t, V_HEAD), lambda i, h, j: (h, i, 0)),
        ],
        out_specs=pl.BlockSpec((tq, V_HEAD), lambda i, h, j: (i * nq + j, h)),
        out_shape=jax.ShapeDtypeStruct((b * t, MLA_HEADS * V_HEAD), BF16),
        compiler_params=_cparams(("parallel", "parallel", "arbitrary")),
        name="mla_flash",
    )(q, k, v)


def _mla_dec_prep_body(qa_ref, kvc_ref, misc_ref, tab_ref, wqb_ref, wk_ref,
                       g_qa_ref, g_nope_ref, g_rope_ref, g_kva_ref, g_knope_ref, g_krope_ref,
                       qabs_ref, qpe_ref, c_ref, kpe_ref):
    cos, sina, sinb = tab_ref[0], tab_ref[1], tab_ref[2]
    qs = _mla_queries(qa_ref[...], wqb_ref, g_qa_ref[...], g_nope_ref[...], g_rope_ref[...],
                      cos, sina, sinb)
    c, kp = _mla_latent(kvc_ref[...], misc_ref[...], g_kva_ref[...], g_krope_ref[...], cos, sina, sinb)
    c_ref[...] = c
    kpe_ref[...] = kp
    g_kn = g_knope_ref[...]
    for h in range(MLA_HEADS):
        qn, qp = qs[h]
        qabs_ref[h] = _dot_nt((qn * g_kn * MLA_SCALE).astype(BF16),
                              wk_ref[:, h * QK_NOPE:(h + 1) * QK_NOPE])
        qpe_ref[h] = qp * MLA_SCALE


def _mla_dec_prep(qa, kvc, misc, tabs, wqb, wk, gains):
    n = qa.shape[0]
    args = (qa, kvc, misc, tabs, wqb, wk) + tuple(gains)
    return pl.pallas_call(
        _mla_dec_prep_body,
        grid=(1,),
        in_specs=[_full(a.shape) for a in args],
        out_specs=[_full((MLA_HEADS, n, KV_LORA)), _full((MLA_HEADS, n, LANE)),
                   _full((n, KV_LORA)), _full((n, LANE))],
        out_shape=[
            jax.ShapeDtypeStruct((MLA_HEADS, n, KV_LORA), F32),
            jax.ShapeDtypeStruct((MLA_HEADS, n, LANE), F32),
            jax.ShapeDtypeStruct((n, KV_LORA), F32),
            jax.ShapeDtypeStruct((n, LANE), F32),
        ],
        compiler_params=_cparams(("arbitrary",)),
        name="mla_dec_prep",
    )(*args)


def _mla_dec_attn_body(pt_ref, qabs_ref, qpe_ref, wkt_ref, ckv_hbm, kpe_hbm,
                       m_out, l_out, acc_out, lhs_ref, cbuf, kbuf, c16, sem_c, sem_k, *, blocks_per_seq):
    npg = PAGES_PER_STEP
    b = pl.program_id(0)
    total_blocks = pl.num_programs(0) * blocks_per_seq
    nk = MLA_HEADS * QK_NOPE
    keys = npg * PAGE_SIZE

    def block_copies(gb, page_of):
        slot = gb % DEC_RING
        cps = []
        for i in range(npg):
            page = page_of(gb * npg + i)
            cps.append(pltpu.make_async_copy(ckv_hbm.at[page], cbuf.at[slot, i], sem_c.at[slot]))
            cps.append(pltpu.make_async_copy(kpe_hbm.at[page], kbuf.at[slot, i], sem_k.at[slot]))
        return cps

    def start_block(gb):
        for cp in block_copies(gb, lambda idx: pt_ref[idx]):
            cp.start()

    def wait_block(gb):
        for cp in block_copies(gb, lambda idx: 0):
            cp.wait()

    @pl.when(b == 0)
    def _():
        lhs_ref[0:nk, :] = wkt_ref[...]
        for gb in range(DEC_RING - 1):
            start_block(gb)

    lhs_ref[nk:nk + BF16_ROWS, :] = qabs_ref[0]
    qpe = qpe_ref[0]

    def scores(j):
        slot = (b * blocks_per_seq + j) % DEC_RING
        c16[j % 2] = cbuf[slot].reshape(keys, KV_LORA).astype(BF16)
        kpt = jnp.concatenate([kbuf[slot, i] for i in range(npg)], axis=1).astype(BF16)
        kt = _dot_nt(lhs_ref[...], c16[j % 2])
        ssq = [jnp.sum(jnp.square(kt[h * QK_NOPE:(h + 1) * QK_NOPE]), axis=0, keepdims=True)
               for h in range(MLA_HEADS)]
        ssq = jnp.concatenate(ssq + [jnp.ones((SUBLANE - MLA_HEADS, keys), F32)], axis=0)
        pe = _dot(qpe, kpt)
        return kt[nk:nk + SUBLANE] * lax.rsqrt(ssq * (1.0 / QK_NOPE) + EPS) + pe[0:SUBLANE]

    def accumulate(j, s, carry):
        m_old, l_old, acc_old = carry
        m_new = jnp.maximum(m_old, jnp.max(s, axis=-1, keepdims=True))
        p = jnp.exp(s - m_new)
        corr = jnp.exp(m_old - m_new)
        l_new = l_old * corr + jnp.sum(p, axis=-1, keepdims=True)
        acc_new = acc_old * corr + _dot(p.astype(BF16), c16[j % 2])
        return m_new, l_new, acc_new

    def fetch_and_score(j, after):
        gb = b * blocks_per_seq + j
        wait_block(gb)
        s_cur = scores(j)
        state = after()
        nxt = gb + (DEC_RING - 1)

        @pl.when(nxt < total_blocks)
        def _():
            start_block(nxt)

        return s_cur, state

    def body(j, carry):
        s_prev, state = carry
        return fetch_and_score(j, lambda: accumulate(j - 1, s_prev, state))

    init = (jnp.full((SUBLANE, 1), -jnp.inf, F32), jnp.zeros((SUBLANE, 1), F32),
            jnp.zeros((SUBLANE, KV_LORA), F32))
    s_last, state = lax.fori_loop(1, blocks_per_seq, body, fetch_and_score(0, lambda: init))
    m_fin, l_fin, acc_fin = accumulate(blocks_per_seq - 1, s_last, state)
    m_out[0] = jnp.broadcast_to(m_fin, m_out.shape[1:])
    l_out[0] = jnp.broadcast_to(l_fin, l_out.shape[1:])
    acc_out[0] = acc_fin


def _mla_dec_attn(page_table, ckv_pool, kpe_pool, qabs, qpe, wkt):
    n, n_pages = page_table.shape
    npg = PAGES_PER_STEP
    nk = MLA_HEADS * QK_NOPE

    assert n_pages % npg == 0 and n * (n_pages // npg) >= DEC_RING
    seq = lambda shape: pl.BlockSpec((1,) + shape, lambda b, pt: (b, 0, 0))
    grid_spec = pltpu.PrefetchScalarGridSpec(
        num_scalar_prefetch=1,
        grid=(n,),
        in_specs=[seq((BF16_ROWS, KV_LORA)), seq((BF16_ROWS, QK_ROPE)),
                  pl.BlockSpec(wkt.shape, lambda b, pt: (0, 0)),
                  pl.BlockSpec(memory_space=pl.ANY), pl.BlockSpec(memory_space=pl.ANY)],
        out_specs=[seq((SUBLANE, LANE)), seq((SUBLANE, LANE)), seq((SUBLANE, KV_LORA))],
        scratch_shapes=[
            pltpu.VMEM((nk + BF16_ROWS, KV_LORA), BF16),
            pltpu.VMEM((DEC_RING, npg, PAGE_SIZE, KV_LORA), F32),
            pltpu.VMEM((DEC_RING, npg, QK_ROPE, PAGE_SIZE), F32),
            pltpu.VMEM((2, npg * PAGE_SIZE, KV_LORA), BF16),
            pltpu.SemaphoreType.DMA((DEC_RING,)),
            pltpu.SemaphoreType.DMA((DEC_RING,)),
        ],
    )
    return pl.pallas_call(
        functools.partial(_mla_dec_attn_body, blocks_per_seq=n_pages // npg),
        grid_spec=grid_spec,
        out_shape=[
            jax.ShapeDtypeStruct((n, SUBLANE, LANE), F32),
            jax.ShapeDtypeStruct((n, SUBLANE, LANE), F32),
            jax.ShapeDtypeStruct((n, SUBLANE, KV_LORA), F32),
        ],
        compiler_params=_cparams(("arbitrary",)),
        name="mla_dec_attn",
    )(page_table.reshape(-1), qabs, qpe, wkt, ckv_pool, kpe_pool)


def _mla_dec_finish_body(m_ref, l_ref, acc_ref, qabs_ref, qpe_ref, c_ref, kpe_ref, wk_ref, wv_ref,
                         y_ref):
    c = c_ref[...]
    kp = kpe_ref[...]
    cb = c.astype(BF16)
    kn = _dot(cb, wk_ref[...])
    for h in range(MLA_HEADS):
        knh = kn[:, h * QK_NOPE:(h + 1) * QK_NOPE]
        r = lax.rsqrt(jnp.sum(knh * knh, axis=-1, keepdims=True) * (1.0 / QK_NOPE) + EPS)
        s_new = (r * jnp.sum(qabs_ref[h] * c, axis=-1, keepdims=True)
                 + jnp.sum(qpe_ref[h] * kp, axis=-1, keepdims=True))
        m_old = m_ref[h][:, 0:1]
        m_new = jnp.maximum(m_old, s_new)
        p = jnp.exp(s_new - m_new)
        corr = jnp.exp(m_old - m_new)
        l = l_ref[h][:, 0:1] * corr + p
        lat = (acc_ref[h] * corr + p * c) / l
        y_ref[:, h * V_HEAD:(h + 1) * V_HEAD] = _dot(
            lat.astype(BF16), wv_ref[:, h * V_HEAD:(h + 1) * V_HEAD]).astype(BF16)


def _mla_dec_finish(m, l, acc, qabs, qpe, c, kpe, wk, wv):
    n = c.shape[0]
    args = (m, l, acc, qabs, qpe, c, kpe, wk, wv)
    return pl.pallas_call(
        _mla_dec_finish_body,
        grid=(1,),
        in_specs=[_full(a.shape) for a in args],
        out_specs=_full((n, MLA_HEADS * V_HEAD)),
        out_shape=jax.ShapeDtypeStruct((n, MLA_HEADS * V_HEAD), BF16),
        compiler_params=_cparams(("arbitrary",)),
        name="mla_dec_finish",
    )(*args)


def _mix_q_body(x_ref, yg_ref, ym_ref, wmix_ref, g_mem_ref, wq_ref, g_qn_ref, x1_ref, q_ref):
    x1 = (x_ref[...] + _dot(yg_ref[...], wmix_ref[0:GDN_V, :])
          + _dot(ym_ref[...], wmix_ref[GDN_V:, :]))
    x1_ref[...] = x1
    q = _dot(_rms(x1, g_mem_ref[...]).astype(BF16), wq_ref[...])
    g_qn = g_qn_ref[...]
    for h in range(MEM_HEADS):
        sl = slice(h * MEM_HD, (h + 1) * MEM_HD)
        q_ref[:, sl] = (_rms(q[:, sl], g_qn) * (MEM_HD ** -0.5)).astype(BF16)


def _mix_q(x, yg, ym, wmix, g_mem, wq, g_qn, tm):
    t = x.shape[0]
    row = lambda n: pl.BlockSpec((tm, n), lambda i: (i, 0))
    return pl.pallas_call(
        _mix_q_body,
        grid=(t // tm,),
        in_specs=[row(D_MODEL), row(GDN_V), row(MLA_HEADS * V_HEAD), _full(wmix.shape),
                  _full(g_mem.shape), _full(wq.shape), _full(g_qn.shape)],
        out_specs=[row(D_MODEL), row(MEM_DIM)],
        out_shape=[jax.ShapeDtypeStruct((t, D_MODEL), F32), jax.ShapeDtypeStruct((t, MEM_DIM), BF16)],
        compiler_params=_cparams(("parallel",)),
        name="mix_q",
    )(x, yg, ym, wmix, g_mem, wq, g_qn)


def _mem_attn_body(q_ref, k_ref, v_ref, o_ref):
    q = q_ref[0]
    for h in range(MEM_HEADS):
        sl = slice(h * MEM_HD, (h + 1) * MEM_HD)
        s = _dot_nt(q[:, sl], k_ref[0, :, sl].astype(BF16))
        p = jnp.exp(s - jnp.max(s, axis=-1, keepdims=True))
        p = p / jnp.sum(p, axis=-1, keepdims=True)
        o_ref[0, :, sl] = _dot(p.astype(BF16), v_ref[0, :, sl].astype(BF16)).astype(BF16)


def _mem_attn(q, k, v, tq):
    nb, t, _ = q.shape
    m = k.shape[1]
    return pl.pallas_call(
        _mem_attn_body,
        grid=(nb, t // tq),
        in_specs=[pl.BlockSpec((1, tq, MEM_DIM), lambda i, j: (i, j, 0)),
                  pl.BlockSpec((1, m, MEM_DIM), lambda i, j: (i, 0, 0)),
                  pl.BlockSpec((1, m, MEM_DIM), lambda i, j: (i, 0, 0))],
        out_specs=pl.BlockSpec((1, tq, MEM_DIM), lambda i, j: (i, j, 0)),
        out_shape=jax.ShapeDtypeStruct((nb, t, MEM_DIM), BF16),
        compiler_params=_cparams(("parallel", "arbitrary")),
        name="mem_attn",
    )(q, k, v)


def _out_ffn_body(x1_ref, o_ref, wo_ref, g_ffn_ref, wg_ref, wu_ref, wd_ref, y_ref):
    x2 = x1_ref[...] + _dot(o_ref[...], wo_ref[...])
    h = _rms(x2, g_ffn_ref[...]).astype(BF16)
    act = (_silu(_dot(h, wg_ref[...])) * _dot(h, wu_ref[...])).astype(BF16)
    y_ref[...] = x2 + _dot(act, wd_ref[...])


def _out_ffn(x1, o, wo, g_ffn, wg, wu, wd, tm):
    t = x1.shape[0]
    row = lambda n: pl.BlockSpec((tm, n), lambda i: (i, 0))
    const = lambda a: pl.BlockSpec(a.shape, lambda i: (0,) * a.ndim, pipeline_mode=pl.Buffered(1))
    return pl.pallas_call(
        _out_ffn_body,
        grid=(t // tm,),
        in_specs=[row(D_MODEL), row(MEM_DIM), const(wo), const(g_ffn), const(wg), const(wu), const(wd)],
        out_specs=row(D_MODEL),
        out_shape=jax.ShapeDtypeStruct((t, D_MODEL), F32),
        compiler_params=_cparams(("parallel",)),
        name="out_ffn",
    )(x1, o, wo, g_ffn, wg, wu, wd)


def _mem_kv_body(mem_ref, wk_ref, wv_ref, g_ref, k_ref, v_ref):
    mb = mem_ref[...].astype(BF16)
    k = _dot(mb, wk_ref[...])
    g = g_ref[...]
    for h in range(MEM_HEADS):
        sl = slice(h * MEM_HD, (h + 1) * MEM_HD)
        k_ref[:, sl] = _rms(k[:, sl], g)
    v_ref[...] = _dot(mb, wv_ref[...])


def _mem_kv(mem, wk, wv, g, tm):
    t = mem.shape[0]
    row = lambda n: pl.BlockSpec((tm, n), lambda i: (i, 0))
    return pl.pallas_call(
        _mem_kv_body,
        grid=(t // tm,),
        in_specs=[row(D_MODEL), _full(wk.shape), _full(wv.shape), _full(g.shape)],
        out_specs=[row(MEM_DIM), row(MEM_DIM)],
        out_shape=[jax.ShapeDtypeStruct((t, MEM_DIM), F32)] * 2,
        compiler_params=_cparams(("parallel",)),
        name="mem_kv",
    )(mem, wk, wv, g)


def _row(v):
    return v.reshape(1, -1).astype(F32)


def _pad_lanes(v, width=LANE):
    return jnp.pad(v, ((0, 0), (0, width - v.shape[1])))


def _rope_tables(pos, rows):
    half = QK_ROPE // 2
    inv = ROPE_THETA ** (-jnp.arange(half, dtype=F32) / half)
    ang = pos.astype(F32)[:, None] * inv[None, :]
    cos, sin = jnp.cos(ang), jnp.sin(ang)
    zero = jnp.zeros_like(cos)
    tabs = jnp.stack([
        _pad_lanes(jnp.concatenate([cos, cos], axis=1)),
        _pad_lanes(jnp.concatenate([-sin, zero], axis=1)),
        _pad_lanes(jnp.concatenate([zero, sin], axis=1)),
    ])
    return jnp.broadcast_to(tabs, (3, rows, LANE)) if tabs.shape[1] == 1 else tabs


def kernel(x_prompt, x_sample, cache_mla_ckv, cache_mla_kpe, cache_mem_k, cache_mem_v, state_gdn_S, state_gdn_conv, page_table, mem_prompt, norm_mix_g, w_in, gdn_conv_w, gdn_A_log, gdn_dt_bias, gdn_out_norm_g, mla_q_a_norm_g, mla_w_q_b, mla_kv_a_norm_g, mla_w_kv_b, mla_qn_nope_g, mla_qn_rope_g, mla_kn_nope_g, mla_kn_rope_g, w_mix_out, norm_mem_g, mem_wq, mem_wk, mem_wv, mem_wo, mem_qn_g, mem_kn_g, norm_ffn_g, ffn_w_gate, ffn_w_up, ffn_w_down):
    depth = w_in.shape[0]
    assert depth == 1, "single-layer trunk"
    bsz, seq, _ = x_prompt.shape
    nseq, dseq, _ = x_sample.shape
    assert dseq == 1, "one new token per decode sequence"
    past_len = page_table.shape[1] * PAGE_SIZE
    n_tok = bsz * seq

    w = w_in[0]
    o_gz = CONV_DIM
    o_b = o_gz + GDN_V
    o_a = o_b + GDN_HEADS
    o_qa = o_a + GDN_HEADS
    o_c = o_qa + Q_LORA
    o_kpe = o_c + KV_LORA
    misc_w = _pad_lanes(jnp.concatenate([w[:, o_kpe:o_kpe + QK_ROPE], w[:, o_b:o_qa]], axis=1))
    w_in_p = jnp.concatenate([w[:, :o_b], w[:, o_qa:o_kpe], misc_w], axis=1).astype(BF16)

    wqb = mla_w_q_b[0].reshape(Q_LORA, MLA_HEADS, QK_NOPE + QK_ROPE)
    wqb = jnp.pad(wqb, ((0, 0), (0, 0), (0, QK_PAD - QK_NOPE - QK_ROPE)))
    wqb = wqb.reshape(Q_LORA, MLA_HEADS * QK_PAD).astype(BF16)
    wkvb = mla_w_kv_b[0].reshape(KV_LORA, MLA_HEADS, QK_NOPE + V_HEAD)
    wk = wkvb[:, :, :QK_NOPE].reshape(KV_LORA, MLA_HEADS * QK_NOPE).astype(BF16)
    wv = wkvb[:, :, QK_NOPE:].reshape(KV_LORA, MLA_HEADS * V_HEAD).astype(BF16)
    wkt = wk.T

    lanep = jnp.zeros((2, LANE), F32)
    lanep = lanep.at[0, MISC_A:MISC_A + GDN_HEADS].set(gdn_A_log[0])
    lanep = lanep.at[1, MISC_A:MISC_A + GDN_HEADS].set(gdn_dt_bias[0])
    rowp = jnp.zeros((2 * SUBLANE,), F32)
    rowp = rowp.at[GDN_HEADS:2 * GDN_HEADS].set(gdn_A_log[0])
    rowp = rowp.at[SUBLANE + GDN_HEADS:SUBLANE + 2 * GDN_HEADS].set(gdn_dt_bias[0])
    rowp = jnp.broadcast_to(rowp[:, None], (2 * SUBLANE, GDN_TB))

    g_mix = _row(norm_mix_g[0])
    g_out = _row(gdn_out_norm_g[0])
    mla_gains = (_row(mla_q_a_norm_g[0]), _row(mla_qn_nope_g[0]), _pad_lanes(_row(mla_qn_rope_g[0])),
                 _row(mla_kv_a_norm_g[0]), _row(mla_kn_nope_g[0]), _pad_lanes(_row(mla_kn_rope_g[0])))
    wmix = w_mix_out[0].astype(BF16)
    wq_mem = mem_wq[0].astype(BF16)
    wk_mem = mem_wk[0].astype(BF16)
    wv_mem = mem_wv[0].astype(BF16)
    wo_mem = mem_wo[0].astype(BF16)
    wg = ffn_w_gate[0].astype(BF16)
    wu = ffn_w_up[0].astype(BF16)
    wd = ffn_w_down[0].astype(BF16)
    g_mem = _row(norm_mem_g[0])
    g_ffn = _row(norm_ffn_g[0])
    g_qn = _row(mem_qn_g[0])
    g_kn = _row(mem_kn_g[0])
    conv_w = gdn_conv_w[0]

    xp = x_prompt.reshape(n_tok, D_MODEL)
    qkv, gz, qa, kvc, misc = _in_proj(xp, g_mix, w_in_p, 256)
    qkv3 = qkv.reshape(bsz, seq, CONV_DIM)
    grow = jnp.swapaxes(misc.reshape(bsz, seq, LANE)[:, :, MISC_B:MISC_B + SUBLANE], 1, 2)
    y_gdn, p_s = _gdn_prompt(qkv3, misc.reshape(bsz, seq, LANE), grow, gz.reshape(bsz, seq, GDN_V),
                             conv_w, lanep, rowp, g_out)
    p_conv = qkv3[:, seq - (CONV_W - 1):, :]

    tabs_p = _rope_tables(jnp.arange(seq), seq)
    q_full, k_full, v_full, p_c, p_kpe = _mla_prep(qa, kvc, misc, tabs_p, wqb, wk, wv, mla_gains, 256)
    y_mla = _mla_flash(q_full, k_full, v_full, bsz, seq)

    mem_k, mem_v = _mem_kv(mem_prompt.reshape(-1, D_MODEL), wk_mem, wv_mem, g_kn, 256)
    n_mem = mem_prompt.shape[1]
    x1, q_mem = _mix_q(xp, y_gdn.reshape(n_tok, GDN_V), y_mla, wmix, g_mem, wq_mem, g_qn, 256)
    o_mem = _mem_attn(q_mem.reshape(bsz, seq, MEM_DIM), mem_k.reshape(bsz, n_mem, MEM_DIM),
                      mem_v.reshape(bsz, n_mem, MEM_DIM), 256)
    y_prompt = _out_ffn(x1, o_mem.reshape(n_tok, MEM_DIM), wo_mem, g_ffn, wg, wu, wd, 256)

    xs = x_sample.reshape(nseq, D_MODEL)
    qkv_s, gz_s, qa_s, kvc_s, misc_s = _in_proj(xs, g_mix, w_in_p, nseq)
    conv_prev = state_gdn_conv[0]
    y_gdn_s, s_new = _gdn_decode(qkv_s, jnp.swapaxes(conv_prev, 0, 1), misc_s, gz_s, state_gdn_S[0],
                                 conv_w, lanep, g_out)
    s_conv = jnp.concatenate([conv_prev[:, 1:, :], qkv_s[:, None, :]], axis=1)

    tabs_s = _rope_tables(jnp.full((1,), past_len), nseq)
    qabs, qpe, c_new, kpe_new = _mla_dec_prep(qa_s, kvc_s, misc_s, tabs_s, wqb, wk, mla_gains)
    pad_rows = lambda a: jnp.pad(jnp.swapaxes(a, 0, 1), ((0, 0), (0, BF16_ROWS - MLA_HEADS), (0, 0)))
    m_run, l_run, acc_run = _mla_dec_attn(
        page_table, cache_mla_ckv[0], jnp.swapaxes(cache_mla_kpe[0], 1, 2),
        pad_rows(qabs).astype(BF16), pad_rows(qpe[:, :, :QK_ROPE]).astype(BF16), wkt)
    heads_first = lambda a: jnp.swapaxes(a[:, :MLA_HEADS], 0, 1)
    y_mla_s = _mla_dec_finish(heads_first(m_run), heads_first(l_run), heads_first(acc_run),
                              qabs, qpe, c_new, kpe_new, wk, wv)

    x1_s, q_mem_s = _mix_q(xs, y_gdn_s, y_mla_s, wmix, g_mem, wq_mem, g_qn, nseq)
    q_rep = jnp.broadcast_to(q_mem_s[:, None, :], (nseq, BF16_ROWS, MEM_DIM))
    o_mem_s = _mem_attn(q_rep, cache_mem_k[0].reshape(nseq, -1, MEM_DIM),
                        cache_mem_v[0].reshape(nseq, -1, MEM_DIM), BF16_ROWS)[:, 0, :]
    y_sample = _out_ffn(x1_s, o_mem_s, wo_mem, g_ffn, wg, wu, wd, nseq)

    return (
        y_prompt.reshape(bsz, seq, D_MODEL),
        y_sample.reshape(nseq, 1, D_MODEL),
        p_c.reshape(1, bsz, seq, KV_LORA),
        p_kpe.reshape(1, bsz, seq, QK_ROPE),
        mem_k.reshape(1, bsz, n_mem, MEM_HEADS, MEM_HD),
        mem_v.reshape(1, bsz, n_mem, MEM_HEADS, MEM_HD),
        p_s[None],
        p_conv[None],
        c_new.reshape(1, nseq, 1, KV_LORA),
        kpe_new[:, :QK_ROPE].reshape(1, nseq, 1, QK_ROPE),
        s_new[None],
        s_conv[None],
    )
```

```python
import functools

import jax
import jax.numpy as jnp
from jax import lax
from jax.experimental import pallas as pl
from jax.experimental.pallas import tpu as pltpu

F32 = jnp.float32
BF16 = jnp.bfloat16

D_MODEL = 1024
GDN_HEADS = 4
GDN_DK = 128
GDN_DV = 128
CONV_W = 4
CHUNK = 64
MLA_HEADS = 4
Q_LORA = 384
KV_LORA = 256
QK_NOPE = 128
QK_ROPE = 64
V_HEAD = 128
ROPE_THETA = 10000.0
PAGE_SIZE = 128
MEM_HEADS = 4
MEM_HD = 128
EPS = 1e-6
GDN_QK = GDN_HEADS * GDN_DK
GDN_V = GDN_HEADS * GDN_DV
CONV_DIM = 2 * GDN_QK + GDN_V
MLA_SCALE = (QK_NOPE + QK_ROPE) ** -0.5
MEM_DIM = MEM_HEADS * MEM_HD

LANE = 128
SUBLANE = 8
BF16_ROWS = 16
QK_PAD = 2 * LANE
MISC_B = QK_ROPE
MISC_A = QK_ROPE + GDN_HEADS
VMEM_LIMIT = 56 * 1024 * 1024

GDN_TILE = 2 * CHUNK
GDN_TB = 2 * GDN_TILE
PAGES_PER_STEP = 32
DEC_RING = 3


def _cparams(sem):
    return pltpu.CompilerParams(dimension_semantics=sem, vmem_limit_bytes=VMEM_LIMIT)


def _full(shape):
    n = len(shape)
    return pl.BlockSpec(shape, lambda *_: (0,) * n)


def _dot(a, b):
    return jnp.dot(a, b, preferred_element_type=F32)


def _dot_nt(a, b):
    return lax.dot_general(a, b, (((1,), (1,)), ((), ())), preferred_element_type=F32)


def _dot_tn(a, b):
    return lax.dot_general(a, b, (((0,), (0,)), ((), ())), preferred_element_type=F32)


def _split3(x):
    hi = x.astype(BF16)
    r = x - hi.astype(F32)
    mid = r.astype(BF16)
    lo = (r - mid.astype(F32)).astype(BF16)
    return hi, mid, lo


def _dot01_left(m01, x):
    hi, mid, lo = _split3(x)
    return _dot(m01, hi) + _dot(m01, mid) + _dot(m01, lo)


def _dot01_right(x, m01):
    hi, mid, lo = _split3(x)
    return _dot(hi, m01) + _dot(mid, m01) + _dot(lo, m01)


def _rms(x, g, n=None):
    n = x.shape[-1] if n is None else n
    ms = jnp.sum(x * x, axis=-1, keepdims=True) * (1.0 / n)
    return x * lax.rsqrt(ms + EPS) * g


def _silu(x):
    return x * jax.nn.sigmoid(x)


def _softplus(x):
    return jnp.maximum(x, 0.0) + jnp.log1p(jnp.exp(-jnp.abs(x)))


def _iota(shape, dim):
    return lax.broadcasted_iota(jnp.int32, shape, dim)


def _block_id(i, size):
    assert size & (size - 1) == 0
    return lax.shift_right_logical(i, size.bit_length() - 1)


def _in_proj_body(x_ref, g_ref, w_ref, qkv_ref, gz_ref, qa_ref, kvc_ref, misc_ref):
    h = _rms(x_ref[...], g_ref[...]).astype(BF16)
    off = 0
    for ref in (qkv_ref, gz_ref, qa_ref, kvc_ref, misc_ref):
        n = ref.shape[-1]
        ref[...] = _dot(h, w_ref[:, off:off + n])
        off += n


def _in_proj(x, g, w, tm):
    t = x.shape[0]
    widths = (CONV_DIM, GDN_V, Q_LORA, KV_LORA, LANE)
    return pl.pallas_call(
        _in_proj_body,
        grid=(t // tm,),
        in_specs=[pl.BlockSpec((tm, D_MODEL), lambda i: (i, 0)), _full(g.shape), _full(w.shape)],
        out_specs=[pl.BlockSpec((tm, n), lambda i: (i, 0)) for n in widths],
        out_shape=[jax.ShapeDtypeStruct((t, n), F32) for n in widths],
        compiler_params=_cparams(("parallel",)),
        name="in_proj",
    )(x, g, w)


def _gate_lane_values(misc, lanep):
    lane = _iota(misc.shape, 1)
    is_a = (lane >= MISC_A) & (lane < MISC_A + GDN_HEADS)
    beta_l = jax.nn.sigmoid(misc)
    g_l = -jnp.exp(lanep[0:1, :]) * _softplus(jnp.where(is_a, misc, 0.0) + lanep[1:2, :])
    return beta_l, g_l


def _expand_matrix(first_lane):
    r = _iota((LANE, GDN_V), 0)
    c = _iota((LANE, GDN_V), 1)
    return jnp.where(r == first_lane + _block_id(c, GDN_DV), 1.0, 0.0).astype(BF16)


def _gdn_gate_out(o, z, gout):
    return _rms(o, gout) * _silu(z)


def _tri_inverse(a_list, eye, same16, same32):
    def mm(xs, ys):
        return [_dot(x.astype(BF16), y.astype(BF16)) for x, y in zip(xs, ys)]

    def add(ts, us):
        return [t + u for t, u in zip(ts, us)]

    def sub(ts, us):
        return [t - u for t, u in zip(ts, us)]

    ad = [jnp.where(same16, a, 0.0) for a in a_list]
    t = [eye - x for x in ad]
    p = mm(ad, ad)
    t = add(t, mm(t, p))
    p = mm(p, p)
    t = add(t, mm(t, p))
    p = mm(p, p)
    t = add(t, mm(t, p))
    b1 = [jnp.where(same32, a, 0.0) - x for a, x in zip(a_list, ad)]
    t = sub(t, mm(t, mm(b1, t)))
    b2 = [jnp.where(same32, 0.0, a) for a in a_list]
    t = sub(t, mm(t, mm(b2, t)))
    return t


def _gdn_prompt_body(qkv_ref, misc_ref, grow_ref, gz_ref, convw_ref, lanep_ref, rowp_ref, gout_ref,
                     y_ref, s_out_ref, xe_ref, s_ref):
    tb = GDN_TB
    t_idx = pl.program_id(1)

    @pl.when(t_idx == 0)
    def _():
        s_ref[...] = jnp.zeros(s_ref.shape, F32)
        xe_ref[0:SUBLANE, :] = jnp.zeros((SUBLANE, CONV_DIM), F32)

    x = qkv_ref[0]
    xe_ref[SUBLANE:SUBLANE + tb, :] = x
    w = convw_ref[...]
    conv = x * w[CONV_W - 1:CONV_W, :]
    for i in range(CONV_W - 1):
        lo = SUBLANE - (CONV_W - 1) + i
        conv = conv + xe_ref[lo:lo + tb, :] * w[i:i + 1, :]
    xe_ref[0:SUBLANE, :] = x[tb - SUBLANE:tb, :]

    tile = GDN_TILE
    tiles = range(tb // tile)
    rows = [slice(t * tile, (t + 1) * tile) for t in tiles]
    ri = _iota((tile, tile), 0)
    ci = _iota((tile, tile), 1)
    same64 = _block_id(ri, CHUNK) == _block_id(ci, CHUNK)
    lower = same64 & (ci <= ri)
    strict = same64 & (ci < ri)
    same16 = _block_id(ri, 16) == _block_id(ci, 16)
    same32 = _block_id(ri, 32) == _block_id(ci, 32)
    eye = jnp.where(ri == ci, 1.0, 0.0)
    l_incl = jnp.where(lower, 1.0, 0.0).astype(BF16)
    u_incl = jnp.where(same64 & (ri <= ci), 1.0, 0.0).astype(BF16)
    u_strict = jnp.where(same64 & (ci > ri), 1.0, 0.0).astype(BF16)

    beta_l, g_l = _gate_lane_values(misc_ref[0], lanep_ref[...])
    e_b = _expand_matrix(MISC_B)
    e_a = _expand_matrix(MISC_A)
    beta_bc = _dot01_right(beta_l, e_b)
    gc_col = jnp.concatenate([_dot01_left(l_incl, g_l[r]) for r in rows], axis=0)
    gt_col = jnp.concatenate([_dot01_left(u_strict, g_l[r]) for r in rows], axis=0)
    gc_bc = _dot01_right(gc_col, e_a)
    gt_bc = _dot01_right(gt_col, e_a)
    egc = jnp.exp(gc_bc)
    etail = jnp.exp(gt_bc)

    grow = grow_ref[0]
    rowi = _iota(grow.shape, 0)
    g_r = -jnp.exp(rowp_ref[0:SUBLANE, :]) * _softplus(
        jnp.where(rowi >= GDN_HEADS, grow, 0.0) + rowp_ref[SUBLANE:2 * SUBLANE, :])
    gcr = [_dot01_right(g_r[:, r], u_incl) for r in rows]

    gout = gout_ref[...]
    gz = gz_ref[0]
    heads = range(GDN_HEADS)
    sls = [slice(h * GDN_DK, (h + 1) * GDN_DK) for h in heads]
    units = [(t, h) for t in tiles for h in heads]
    qh, kh, kb, vb, dm = {}, {}, {}, {}, {}
    for h in heads:
        q = _silu(conv[:, h * GDN_DK:(h + 1) * GDN_DK])
        k = _silu(conv[:, GDN_QK + h * GDN_DK:GDN_QK + (h + 1) * GDN_DK])
        v = _silu(conv[:, 2 * GDN_QK + h * GDN_DV:2 * GDN_QK + (h + 1) * GDN_DV])
        q = q * lax.rsqrt(jnp.sum(q * q, axis=-1, keepdims=True) + EPS) * (GDN_DK ** -0.5)
        k = k * lax.rsqrt(jnp.sum(k * k, axis=-1, keepdims=True) + EPS)
        bb = beta_bc[:, sls[h]]
        for t in tiles:
            qh[t, h] = q[rows[t]]
            kh[t, h] = k[rows[t]]
            kb[t, h] = (k * bb)[rows[t]]
            vb[t, h] = (v * bb)[rows[t]]
            diff = gc_bc[rows[t], sls[h]] - gcr[t][GDN_HEADS + h:GDN_HEADS + h + 1, :]
            dm[t, h] = jnp.where(lower, jnp.exp(jnp.where(lower, diff, 0.0)), 0.0)
    p = {u: _dot_nt(jnp.concatenate([kb[u], qh[u]], axis=0).astype(BF16), kh[u].astype(BF16))
         for u in units}
    a_mat = [jnp.where(strict, p[u][:tile] * dm[u], 0.0) for u in units]
    attn = {u: (p[u][tile:] * dm[u]).astype(BF16) for u in units}
    t_mat = dict(zip(units, _tri_inverse(a_mat, eye, same16, same32)))
    egc_u = {(t, h): egc[rows[t], sls[h]] for t, h in units}
    uw = {u: _dot(t_mat[u].astype(BF16),
                  jnp.concatenate([vb[u], kb[u] * egc_u[u]], axis=1).astype(BF16)) for u in units}
    qd = {u: qh[u] * egc_u[u] for u in units}
    kt = {(t, h): (kh[t, h] * etail[rows[t], sls[h]]).astype(BF16) for t, h in units}
    s = [s_ref[h] for h in heads]
    o_inter = {u: [] for u in units}
    v_new = {u: [] for u in units}
    for t in tiles:
        for c in range(tile // CHUNK):
            rs = slice(c * CHUNK, (c + 1) * CHUNK)
            last = t * tile + (c + 1) * CHUNK - 1
            wq = [_dot(jnp.concatenate([uw[t, h][rs, GDN_DV:], qd[t, h][rs]], axis=0).astype(BF16),
                       s[h].astype(BF16)) for h in heads]
            vn = [uw[t, h][rs, :GDN_DV] - wq[h][:CHUNK] for h in heads]
            egl = [jnp.exp(gc_bc[last:last + 1, sls[h]]) for h in heads]
            s = [s[h] * egl[h] + _dot_tn(kt[t, h][rs], vn[h].astype(BF16)) for h in heads]
            for h in heads:
                o_inter[t, h].append(wq[h][CHUNK:])
                v_new[t, h].append(vn[h])
    o = {u: jnp.concatenate(o_inter[u], axis=0)
         + _dot(attn[u], jnp.concatenate(v_new[u], axis=0).astype(BF16)) for u in units}
    for h in heads:
        s_ref[h] = s[h]
        for t in tiles:
            y_ref[0, rows[t], sls[h]] = _gdn_gate_out(o[t, h], gz[rows[t], sls[h]], gout).astype(BF16)

    @pl.when(t_idx == pl.num_programs(1) - 1)
    def _():
        s_out_ref[0] = s_ref[...]


def _gdn_prompt(qkv, misc, grow, gz, convw, lanep, rowp, gout):
    b, t, _ = qkv.shape
    tb = GDN_TB
    return pl.pallas_call(
        _gdn_prompt_body,
        grid=(b, t // tb),
        in_specs=[
            pl.BlockSpec((1, tb, CONV_DIM), lambda i, j: (i, j, 0)),
            pl.BlockSpec((1, tb, LANE), lambda i, j: (i, j, 0)),
            pl.BlockSpec((1, SUBLANE, tb), lambda i, j: (i, 0, j)),
            pl.BlockSpec((1, tb, GDN_V), lambda i, j: (i, j, 0)),
            _full(convw.shape), _full(lanep.shape), _full(rowp.shape), _full(gout.shape),
        ],
        out_specs=[
            pl.BlockSpec((1, tb, GDN_V), lambda i, j: (i, j, 0)),
            pl.BlockSpec((1, GDN_HEADS, GDN_DK, GDN_DV), lambda i, j: (i, 0, 0, 0)),
        ],
        out_shape=[
            jax.ShapeDtypeStruct((b, t, GDN_V), BF16),
            jax.ShapeDtypeStruct((b, GDN_HEADS, GDN_DK, GDN_DV), F32),
        ],
        scratch_shapes=[
            pltpu.VMEM((SUBLANE + tb, CONV_DIM), F32),
            pltpu.VMEM((GDN_HEADS, GDN_DK, GDN_DV), F32),
        ],
        compiler_params=_cparams(("parallel", "arbitrary")),
        name="gdn_prompt",
    )(qkv, misc, grow, gz, convw, lanep, rowp, gout)


GDN_DEC_SEQS = 8


def _gdn_decode_body(qkv_ref, cprev_ref, misc_ref, gz_ref, s_in_ref, convw_ref, lanep_ref, gout_ref,
                     y_ref, s_out_ref, o_ref):
    ns = GDN_DEC_SEQS
    w = convw_ref[...]
    conv = qkv_ref[...] * w[CONV_W - 1:CONV_W, :]
    for i in range(CONV_W - 1):
        conv = conv + cprev_ref[i] * w[i:i + 1, :]
    conv = _silu(conv)
    beta_l, g_l = _gate_lane_values(misc_ref[...], lanep_ref[...])
    beta_bc = _dot01_right(beta_l, _expand_matrix(MISC_B))
    eg = jnp.exp(_dot01_right(g_l, _expand_matrix(MISC_A)))
    row0 = jnp.where(_iota((SUBLANE, LANE), 0) == 0, 1.0, 0.0).astype(BF16)
    units = [(h, i) for h in range(GDN_HEADS) for i in range(ns)]
    qn, kn, vv = {}, {}, {}
    for h in range(GDN_HEADS):
        qh = conv[:, h * GDN_DK:(h + 1) * GDN_DK]
        kh = conv[:, GDN_QK + h * GDN_DK:GDN_QK + (h + 1) * GDN_DK]
        qn[h] = qh * lax.rsqrt(jnp.sum(qh * qh, axis=-1, keepdims=True) + EPS) * (GDN_DK ** -0.5)
        kn[h] = kh * lax.rsqrt(jnp.sum(kh * kh, axis=-1, keepdims=True) + EPS)
        vv[h] = conv[:, 2 * GDN_QK + h * GDN_DV:2 * GDN_QK + (h + 1) * GDN_DV]
    sl = lambda h: slice(h * GDN_DK, (h + 1) * GDN_DK)
    k8 = {u: jnp.broadcast_to(kn[u[0]][u[1]:u[1] + 1, :], (SUBLANE, GDN_DK)) for u in units}
    s_dec = {(h, i): s_in_ref[i, h] * eg[i:i + 1, sl(h)] for h, i in units}
    kv = {u: _dot(k8[u].astype(BF16), s_dec[u].astype(BF16))[0:1, :] for u in units}
    kcol = {}
    for u in units:
        k_hi, k_mid, k_lo = _split3(k8[u])
        kcol[u] = _dot_tn(k_hi, row0) + _dot_tn(k_mid, row0) + _dot_tn(k_lo, row0)
    s_new = {}
    for h, i in units:
        delta = (vv[h][i:i + 1, :] - kv[h, i]) * beta_bc[i:i + 1, sl(h)]
        s_new[h, i] = s_dec[h, i] + kcol[h, i] * delta
        s_out_ref[i, h] = s_new[h, i]
    for h, i in units:
        q8 = jnp.broadcast_to(qn[h][i:i + 1, :], (SUBLANE, GDN_DK))
        o_ref[i:i + 1, sl(h)] = _dot(q8.astype(BF16), s_new[h, i].astype(BF16))[0:1, :]
    gout = gout_ref[...]
    gz = gz_ref[...]
    for h in range(GDN_HEADS):
        sl = slice(h * GDN_DV, (h + 1) * GDN_DV)
        y_ref[:, sl] = _gdn_gate_out(o_ref[:, sl], gz[:, sl], gout).astype(BF16)


def _gdn_decode(qkv, cprev, misc, gz, s_in, convw, lanep, gout):
    n = qkv.shape[0]
    ns = GDN_DEC_SEQS
    state_spec = pl.BlockSpec((ns, GDN_HEADS, GDN_DK, GDN_DV), lambda i: (i, 0, 0, 0))
    return pl.pallas_call(
        _gdn_decode_body,
        grid=(n // ns,),
        in_specs=[
            pl.BlockSpec((ns, CONV_DIM), lambda i: (i, 0)),
            pl.BlockSpec((CONV_W - 1, ns, CONV_DIM), lambda i: (0, i, 0)),
            pl.BlockSpec((ns, LANE), lambda i: (i, 0)),
            pl.BlockSpec((ns, GDN_V), lambda i: (i, 0)),
            state_spec,
            _full(convw.shape), _full(lanep.shape), _full(gout.shape),
        ],
        out_specs=[pl.BlockSpec((ns, GDN_V), lambda i: (i, 0)), state_spec],
        out_shape=[
            jax.ShapeDtypeStruct((n, GDN_V), BF16),
            jax.ShapeDtypeStruct(s_in.shape, F32),
        ],
        scratch_shapes=[pltpu.VMEM((ns, GDN_V), F32)],
        compiler_params=_cparams(("parallel",)),
        name="gdn_decode",
    )(qkv, cprev, misc, gz, s_in, convw, lanep, gout)


def _rope_padded(x, cos, sina, sinb):
    return (x * cos + pltpu.roll(x, LANE - QK_ROPE // 2, axis=1) * sina
            + pltpu.roll(x, QK_ROPE // 2, axis=1) * sinb)


def _mla_queries(qa, wqb_ref, g_qa, g_nope, g_rope, cos, sina, sinb):
    q = _dot(_rms(qa, g_qa).astype(BF16), wqb_ref[...])
    out = []
    for h in range(MLA_HEADS):
        qn = _rms(q[:, h * QK_PAD:h * QK_PAD + QK_NOPE], g_nope)
        qp = _rms(q[:, h * QK_PAD + QK_NOPE:(h + 1) * QK_PAD], g_rope, QK_ROPE)
        out.append((qn, _rope_padded(qp, cos, sina, sinb)))
    return out


def _mla_latent(kvc, misc, g_kva, g_krope, cos, sina, sinb):
    c = _rms(kvc, g_kva)
    lane = _iota(misc.shape, 1)
    kp = _rms(jnp.where(lane < QK_ROPE, misc, 0.0), g_krope, QK_ROPE)
    return c, _rope_padded(kp, cos, sina, sinb)


def _mla_prep_body(qa_ref, kvc_ref, misc_ref, tab_ref, wqb_ref, wk_ref, wv_ref,
                   g_qa_ref, g_nope_ref, g_rope_ref, g_kva_ref, g_knope_ref, g_krope_ref,
                   q_ref, k_ref, v_ref, c_ref, kpe_ref):
    cos, sina, sinb = tab_ref[0], tab_ref[1], tab_ref[2]
    qs = _mla_queries(qa_ref[...], wqb_ref, g_qa_ref[...], g_nope_ref[...], g_rope_ref[...],
                      cos, sina, sinb)
    c, kp = _mla_latent(kvc_ref[...], misc_ref[...], g_kva_ref[...], g_krope_ref[...], cos, sina, sinb)
    c_ref[...] = c
    kpe_ref[...] = kp[:, :QK_ROPE]
    cb = c.astype(BF16)
    kn = _dot(cb, wk_ref[...])
    v = _dot(cb, wv_ref[...])
    g_kn = g_knope_ref[...]
    kp16 = kp.astype(BF16)
    for h in range(MLA_HEADS):
        qn, qp = qs[h]
        q_ref[h, :, 0:QK_NOPE] = (qn * MLA_SCALE).astype(BF16)
        q_ref[h, :, QK_NOPE:QK_PAD] = (qp * MLA_SCALE).astype(BF16)
        k_ref[h, :, 0:QK_NOPE] = _rms(kn[:, h * QK_NOPE:(h + 1) * QK_NOPE], g_kn).astype(BF16)
        k_ref[h, :, QK_NOPE:QK_PAD] = kp16
        v_ref[h] = v[:, h * V_HEAD:(h + 1) * V_HEAD].astype(BF16)


def _mla_prep(qa, kvc, misc, tabs, wqb, wk, wv, gains, tm):
    t = qa.shape[0]
    n_tab = tabs.shape[1] // tm
    row = lambda n: pl.BlockSpec((tm, n), lambda i: (i, 0))
    head = lambda n: pl.BlockSpec((MLA_HEADS, tm, n), lambda i: (0, i, 0))
    return pl.pallas_call(
        _mla_prep_body,
        grid=(t // tm,),
        in_specs=[row(Q_LORA), row(KV_LORA), row(LANE),
                  pl.BlockSpec((3, tm, LANE), lambda i: (0, i % n_tab, 0)),
                  _full(wqb.shape), _full(wk.shape), _full(wv.shape)]
                 + [_full(g.shape) for g in gains],
        out_specs=[head(QK_PAD), head(QK_PAD), head(V_HEAD), row(KV_LORA), row(QK_ROPE)],
        out_shape=[
            jax.ShapeDtypeStruct((MLA_HEADS, t, QK_PAD), BF16),
            jax.ShapeDtypeStruct((MLA_HEADS, t, QK_PAD), BF16),
            jax.ShapeDtypeStruct((MLA_HEADS, t, V_HEAD), BF16),
            jax.ShapeDtypeStruct((t, KV_LORA), F32),
            jax.ShapeDtypeStruct((t, QK_ROPE), F32),
        ],
        compiler_params=_cparams(("parallel",)),
        name="mla_prep",
    )(qa, kvc, misc, tabs, wqb, wk, wv, *gains)


FLASH_T = 512


FLASH_HEADS = 2


def _flash_body(q_ref, k_ref, v_ref, o_ref):
    tq = FLASH_T
    qi = pl.program_id(2)
    heads = range(FLASH_HEADS)
    q = [q_ref[h] for h in heads]

    def block(j, carry, masked):
        start = pl.multiple_of(j * tq, tq)
        s = [_dot_nt(q[h], k_ref[h, pl.ds(start, tq), :]) for h in heads]
        if masked:
            keep = _iota(s[0].shape, 1) <= _iota(s[0].shape, 0)
            s = [jnp.where(keep, x, -jnp.inf) for x in s]
        m_new = [jnp.maximum(carry[h][0], jnp.max(s[h], axis=-1, keepdims=True)) for h in heads]
        p = [jnp.exp(s[h] - m_new[h]) for h in heads]
        pv = [_dot(p[h].astype(BF16), v_ref[h, pl.ds(start, tq), :]) for h in heads]
        out = []
        for h in heads:
            m, l, acc = carry[h]
            corr = jnp.exp(m - m_new[h])
            out.append((m_new[h], l * corr + jnp.sum(p[h], axis=-1, keepdims=True),
                        acc * corr + pv[h]))
        return tuple(out)

    init = tuple((jnp.full((tq, 1), -jnp.inf, F32), jnp.zeros((tq, 1), F32),
                  jnp.zeros((tq, V_HEAD), F32)) for _ in heads)
    carry = lax.fori_loop(0, qi, lambda j, c: block(j, c, False), init)
    carry = block(qi, carry, True)
    for h in heads:
        _, l, acc = carry[h]
        o_ref[:, h * V_HEAD:(h + 1) * V_HEAD] = (acc / l).astype(BF16)


def _mla_flash(q, k, v, b, t):
    tq = FLASH_T
    nq = t // tq
    nh = FLASH_HEADS
    return pl.pallas_call(
        _flash_body,
        grid=(b, MLA_HEADS // nh, nq),
        in_specs=[
            pl.BlockSpec((nh, tq, QK_PAD), lambda i, h, j: (h, i * nq + j, 0)),
            pl.BlockSpec((nh, t, QK_PAD), lambda i, h, j: (h, i, 0)),
            pl.BlockSpec((nh, t, V_HEAD), lambda i, h, j: (h, i, 0)),
        ],
        out_specs=pl.BlockSpec((tq, nh * V_HEAD), lambda i, h, j: (i * nq + j, h)),
        out_shape=jax.ShapeDtypeStruct((b * t, MLA_HEADS * V_HEAD), BF16),
        compiler_params=_cparams(("parallel", "parallel", "arbitrary")),
        name="mla_flash",
    )(q, k, v)


def _mla_dec_prep_body(qa_ref, kvc_ref, misc_ref, tab_ref, wqb_ref, wk_ref,
                       g_qa_ref, g_nope_ref, g_rope_ref, g_kva_ref, g_knope_ref, g_krope_ref,
                       qabs_ref, qpe_ref, c_ref, kpe_ref):
    cos, sina, sinb = tab_ref[0], tab_ref[1], tab_ref[2]
    qs = _mla_queries(qa_ref[...], wqb_ref, g_qa_ref[...], g_nope_ref[...], g_rope_ref[...],
                      cos, sina, sinb)
    c, kp = _mla_latent(kvc_ref[...], misc_ref[...], g_kva_ref[...], g_krope_ref[...], cos, sina, sinb)
    c_ref[...] = c
    kpe_ref[...] = kp
    g_kn = g_knope_ref[...]
    for h in range(MLA_HEADS):
        qn, qp = qs[h]
        qabs_ref[h] = _dot_nt((qn * g_kn * MLA_SCALE).astype(BF16),
                              wk_ref[:, h * QK_NOPE:(h + 1) * QK_NOPE])
        qpe_ref[h] = qp * MLA_SCALE


def _mla_dec_prep(qa, kvc, misc, tabs, wqb, wk, gains):
    n = qa.shape[0]
    args = (qa, kvc, misc, tabs, wqb, wk) + tuple(gains)
    return pl.pallas_call(
        _mla_dec_prep_body,
        grid=(1,),
        in_specs=[_full(a.shape) for a in args],
        out_specs=[_full((MLA_HEADS, n, KV_LORA)), _full((MLA_HEADS, n, LANE)),
                   _full((n, KV_LORA)), _full((n, LANE))],
        out_shape=[
            jax.ShapeDtypeStruct((MLA_HEADS, n, KV_LORA), F32),
            jax.ShapeDtypeStruct((MLA_HEADS, n, LANE), F32),
            jax.ShapeDtypeStruct((n, KV_LORA), F32),
            jax.ShapeDtypeStruct((n, LANE), F32),
        ],
        compiler_params=_cparams(("arbitrary",)),
        name="mla_dec_prep",
    )(*args)


def _mla_dec_attn_body(pt_ref, qabs_ref, qpe_ref, wkt_ref, ckv_hbm, kpe_hbm,
                       m_out, l_out, acc_out, lhs_ref, cbuf, kbuf, c16, sem_c, sem_k, *, blocks_per_seq):
    npg = PAGES_PER_STEP
    b = pl.program_id(0)
    total_blocks = pl.num_programs(0) * blocks_per_seq
    nk = MLA_HEADS * QK_NOPE
    keys = npg * PAGE_SIZE

    def block_copies(gb, page_of):
        slot = gb % DEC_RING
        cps = []
        for i in range(npg):
            page = page_of(gb * npg + i)
            cps.append(pltpu.make_async_copy(ckv_hbm.at[page], cbuf.at[slot, i], sem_c.at[slot]))
            cps.append(pltpu.make_async_copy(kpe_hbm.at[page], kbuf.at[slot, i], sem_k.at[slot]))
        return cps

    def start_block(gb):
        for cp in block_copies(gb, lambda idx: pt_ref[idx]):
            cp.start()

    def wait_block(gb):
        for cp in block_copies(gb, lambda idx: 0):
            cp.wait()

    @pl.when(b == 0)
    def _():
        lhs_ref[0:nk, :] = wkt_ref[...]
        for gb in range(DEC_RING - 1):
            start_block(gb)

    lhs_ref[nk:nk + BF16_ROWS, :] = qabs_ref[0]
    qpe = qpe_ref[0]

    def scores(j):
        slot = (b * blocks_per_seq + j) % DEC_RING
        c16[j % 2] = cbuf[slot].reshape(keys, KV_LORA).astype(BF16)
        kpt = jnp.concatenate([kbuf[slot, i] for i in range(npg)], axis=1).astype(BF16)
        kt = _dot_nt(lhs_ref[...], c16[j % 2])
        ssq = [jnp.sum(jnp.square(kt[h * QK_NOPE:(h + 1) * QK_NOPE]), axis=0, keepdims=True)
               for h in range(MLA_HEADS)]
        ssq = jnp.concatenate(ssq + [jnp.ones((SUBLANE - MLA_HEADS, keys), F32)], axis=0)
        pe = _dot(qpe, kpt)
        return kt[nk:nk + SUBLANE] * lax.rsqrt(ssq * (1.0 / QK_NOPE) + EPS) + pe[0:SUBLANE]

    def accumulate(j, s, carry):
        m_old, l_old, acc_old = carry
        m_new = jnp.maximum(m_old, jnp.max(s, axis=-1, keepdims=True))
        p = jnp.exp(s - m_new)
        corr = jnp.exp(m_old - m_new)
        l_new = l_old * corr + jnp.sum(p, axis=-1, keepdims=True)
        acc_new = acc_old * corr + _dot(p.astype(BF16), c16[j % 2])
        return m_new, l_new, acc_new

    def fetch_and_score(j, after):
        gb = b * blocks_per_seq + j
        nxt = gb + (DEC_RING - 1)

        @pl.when(nxt < total_blocks)
        def _():
            start_block(nxt)

        wait_block(gb)
        s_cur = scores(j)
        return s_cur, after()

    def body(j, carry):
        s_prev, state = carry
        return fetch_and_score(j, lambda: accumulate(j - 1, s_prev, state))

    init = (jnp.full((SUBLANE, 1), -jnp.inf, F32), jnp.zeros((SUBLANE, 1), F32),
            jnp.zeros((SUBLANE, KV_LORA), F32))
    s_last, state = lax.fori_loop(1, blocks_per_seq, body, fetch_and_score(0, lambda: init))
    m_fin, l_fin, acc_fin = accumulate(blocks_per_seq - 1, s_last, state)
    m_out[0] = jnp.broadcast_to(m_fin, m_out.shape[1:])
    l_out[0] = jnp.broadcast_to(l_fin, l_out.shape[1:])
    acc_out[0] = acc_fin


def _mla_dec_attn(page_table, ckv_pool, kpe_pool, qabs, qpe, wkt):
    n, n_pages = page_table.shape
    npg = PAGES_PER_STEP
    nk = MLA_HEADS * QK_NOPE

    assert n_pages % npg == 0 and n * (n_pages // npg) >= DEC_RING
    seq = lambda shape: pl.BlockSpec((1,) + shape, lambda b, pt: (b, 0, 0))
    grid_spec = pltpu.PrefetchScalarGridSpec(
        num_scalar_prefetch=1,
        grid=(n,),
        in_specs=[seq((BF16_ROWS, KV_LORA)), seq((BF16_ROWS, QK_ROPE)),
                  pl.BlockSpec(wkt.shape, lambda b, pt: (0, 0)),
                  pl.BlockSpec(memory_space=pl.ANY), pl.BlockSpec(memory_space=pl.ANY)],
        out_specs=[seq((SUBLANE, LANE)), seq((SUBLANE, LANE)), seq((SUBLANE, KV_LORA))],
        scratch_shapes=[
            pltpu.VMEM((nk + BF16_ROWS, KV_LORA), BF16),
            pltpu.VMEM((DEC_RING, npg, PAGE_SIZE, KV_LORA), F32),
            pltpu.VMEM((DEC_RING, npg, QK_ROPE, PAGE_SIZE), F32),
            pltpu.VMEM((2, npg * PAGE_SIZE, KV_LORA), BF16),
            pltpu.SemaphoreType.DMA((DEC_RING,)),
            pltpu.SemaphoreType.DMA((DEC_RING,)),
        ],
    )
    return pl.pallas_call(
        functools.partial(_mla_dec_attn_body, blocks_per_seq=n_pages // npg),
        grid_spec=grid_spec,
        out_shape=[
            jax.ShapeDtypeStruct((n, SUBLANE, LANE), F32),
            jax.ShapeDtypeStruct((n, SUBLANE, LANE), F32),
            jax.ShapeDtypeStruct((n, SUBLANE, KV_LORA), F32),
        ],
        compiler_params=_cparams(("arbitrary",)),
        name="mla_dec_attn",
    )(page_table.reshape(-1), qabs, qpe, wkt, ckv_pool, kpe_pool)


def _mla_dec_finish_body(m_ref, l_ref, acc_ref, qabs_ref, qpe_ref, c_ref, kpe_ref, wk_ref, wv_ref,
                         y_ref):
    c = c_ref[...]
    kp = kpe_ref[...]
    cb = c.astype(BF16)
    kn = _dot(cb, wk_ref[...])
    for h in range(MLA_HEADS):
        knh = kn[:, h * QK_NOPE:(h + 1) * QK_NOPE]
        r = lax.rsqrt(jnp.sum(knh * knh, axis=-1, keepdims=True) * (1.0 / QK_NOPE) + EPS)
        s_new = (r * jnp.sum(qabs_ref[h] * c, axis=-1, keepdims=True)
                 + jnp.sum(qpe_ref[h] * kp, axis=-1, keepdims=True))
        m_old = m_ref[h][:, 0:1]
        m_new = jnp.maximum(m_old, s_new)
        p = jnp.exp(s_new - m_new)
        corr = jnp.exp(m_old - m_new)
        l = l_ref[h][:, 0:1] * corr + p
        lat = (acc_ref[h] * corr + p * c) / l
        y_ref[:, h * V_HEAD:(h + 1) * V_HEAD] = _dot(
            lat.astype(BF16), wv_ref[:, h * V_HEAD:(h + 1) * V_HEAD]).astype(BF16)


def _mla_dec_finish(m, l, acc, qabs, qpe, c, kpe, wk, wv):
    n = c.shape[0]
    args = (m, l, acc, qabs, qpe, c, kpe, wk, wv)
    return pl.pallas_call(
        _mla_dec_finish_body,
        grid=(1,),
        in_specs=[_full(a.shape) for a in args],
        out_specs=_full((n, MLA_HEADS * V_HEAD)),
        out_shape=jax.ShapeDtypeStruct((n, MLA_HEADS * V_HEAD), BF16),
        compiler_params=_cparams(("arbitrary",)),
        name="mla_dec_finish",
    )(*args)


def _mix_q_body(x_ref, yg_ref, ym_ref, wmix_ref, g_mem_ref, wq_ref, g_qn_ref, x1_ref, q_ref):
    x1 = (x_ref[...] + _dot(yg_ref[...], wmix_ref[0:GDN_V, :])
          + _dot(ym_ref[...], wmix_ref[GDN_V:, :]))
    x1_ref[...] = x1
    q = _dot(_rms(x1, g_mem_ref[...]).astype(BF16), wq_ref[...])
    g_qn = g_qn_ref[...]
    for h in range(MEM_HEADS):
        sl = slice(h * MEM_HD, (h + 1) * MEM_HD)
        q_ref[:, sl] = (_rms(q[:, sl], g_qn) * (MEM_HD ** -0.5)).astype(BF16)


def _mix_q(x, yg, ym, wmix, g_mem, wq, g_qn, tm):
    t = x.shape[0]
    row = lambda n: pl.BlockSpec((tm, n), lambda i: (i, 0))
    return pl.pallas_call(
        _mix_q_body,
        grid=(t // tm,),
        in_specs=[row(D_MODEL), row(GDN_V), row(MLA_HEADS * V_HEAD), _full(wmix.shape),
                  _full(g_mem.shape), _full(wq.shape), _full(g_qn.shape)],
        out_specs=[row(D_MODEL), row(MEM_DIM)],
        out_shape=[jax.ShapeDtypeStruct((t, D_MODEL), F32), jax.ShapeDtypeStruct((t, MEM_DIM), BF16)],
        compiler_params=_cparams(("parallel",)),
        name="mix_q",
    )(x, yg, ym, wmix, g_mem, wq, g_qn)


def _mem_attn_body(q_ref, k_ref, v_ref, o_ref):
    q = q_ref[0]
    for h in range(MEM_HEADS):
        sl = slice(h * MEM_HD, (h + 1) * MEM_HD)
        s = _dot_nt(q[:, sl], k_ref[0, :, sl].astype(BF16))
        p = jnp.exp(s - jnp.max(s, axis=-1, keepdims=True))
        p = p / jnp.sum(p, axis=-1, keepdims=True)
        o_ref[0, :, sl] = _dot(p.astype(BF16), v_ref[0, :, sl].astype(BF16)).astype(BF16)


def _mem_attn(q, k, v, tq):
    nb, t, _ = q.shape
    m = k.shape[1]
    return pl.pallas_call(
        _mem_attn_body,
        grid=(nb, t // tq),
        in_specs=[pl.BlockSpec((1, tq, MEM_DIM), lambda i, j: (i, j, 0)),
                  pl.BlockSpec((1, m, MEM_DIM), lambda i, j: (i, 0, 0)),
                  pl.BlockSpec((1, m, MEM_DIM), lambda i, j: (i, 0, 0))],
        out_specs=pl.BlockSpec((1, tq, MEM_DIM), lambda i, j: (i, j, 0)),
        out_shape=jax.ShapeDtypeStruct((nb, t, MEM_DIM), BF16),
        compiler_params=_cparams(("parallel", "arbitrary")),
        name="mem_attn",
    )(q, k, v)


def _mem_attn_dec_body(q_ref, k_ref, v_ref, o_ref):
    q = q_ref[...]
    s = jnp.sum(k_ref[0] * q, axis=-1, keepdims=True)
    p = jnp.exp(s - jnp.max(s, axis=0, keepdims=True))
    p = p / jnp.sum(p, axis=0, keepdims=True)
    o_ref[...] = jnp.sum(p * v_ref[0], axis=0, keepdims=True)


def _mem_attn_dec(q, k, v):
    n, m, nh, hd = k.shape
    kv_spec = pl.BlockSpec((1, m, nh, hd), lambda i: (i, 0, 0, 0))
    return pl.pallas_call(
        _mem_attn_dec_body,
        grid=(n,),
        in_specs=[pl.BlockSpec((1, nh, hd), lambda i: (i, 0, 0)), kv_spec, kv_spec],
        out_specs=pl.BlockSpec((1, nh, hd), lambda i: (i, 0, 0)),
        out_shape=jax.ShapeDtypeStruct((n, nh, hd), F32),
        compiler_params=_cparams(("parallel",)),
        name="mem_attn_dec",
    )(q, k, v)


def _out_ffn_body(x1_ref, o_ref, wo_ref, g_ffn_ref, wg_ref, wu_ref, wd_ref, y_ref):
    x2 = x1_ref[...] + _dot(o_ref[...], wo_ref[...])
    h = _rms(x2, g_ffn_ref[...]).astype(BF16)
    act = (_silu(_dot(h, wg_ref[...])) * _dot(h, wu_ref[...])).astype(BF16)
    y_ref[...] = x2 + _dot(act, wd_ref[...])


def _out_ffn(x1, o, wo, g_ffn, wg, wu, wd, tm):
    t = x1.shape[0]
    row = lambda n: pl.BlockSpec((tm, n), lambda i: (i, 0))
    const = lambda a: pl.BlockSpec(a.shape, lambda i: (0,) * a.ndim, pipeline_mode=pl.Buffered(1))
    return pl.pallas_call(
        _out_ffn_body,
        grid=(t // tm,),
        in_specs=[row(D_MODEL), row(MEM_DIM), const(wo), const(g_ffn), const(wg), const(wu), const(wd)],
        out_specs=row(D_MODEL),
        out_shape=jax.ShapeDtypeStruct((t, D_MODEL), F32),
        compiler_params=_cparams(("parallel",)),
        name="out_ffn",
    )(x1, o, wo, g_ffn, wg, wu, wd)


def _mem_kv_body(mem_ref, wk_ref, wv_ref, g_ref, k_ref, v_ref):
    mb = mem_ref[...].astype(BF16)
    k = _dot(mb, wk_ref[...])
    g = g_ref[...]
    for h in range(MEM_HEADS):
        sl = slice(h * MEM_HD, (h + 1) * MEM_HD)
        k_ref[:, sl] = _rms(k[:, sl], g)
    v_ref[...] = _dot(mb, wv_ref[...])


def _mem_kv(mem, wk, wv, g, tm):
    t = mem.shape[0]
    row = lambda n: pl.BlockSpec((tm, n), lambda i: (i, 0))
    return pl.pallas_call(
        _mem_kv_body,
        grid=(t // tm,),
        in_specs=[row(D_MODEL), _full(wk.shape), _full(wv.shape), _full(g.shape)],
        out_specs=[row(MEM_DIM), row(MEM_DIM)],
        out_shape=[jax.ShapeDtypeStruct((t, MEM_DIM), F32)] * 2,
        compiler_params=_cparams(("parallel",)),
        name="mem_kv",
    )(mem, wk, wv, g)


def _row(v):
    return v.reshape(1, -1).astype(F32)


def _pad_lanes(v, width=LANE):
    return jnp.pad(v, ((0, 0), (0, width - v.shape[1])))


def _rope_tables(pos, rows):
    half = QK_ROPE // 2
    inv = ROPE_THETA ** (-jnp.arange(half, dtype=F32) / half)
    ang = pos.astype(F32)[:, None] * inv[None, :]
    cos, sin = jnp.cos(ang), jnp.sin(ang)
    zero = jnp.zeros_like(cos)
    tabs = jnp.stack([
        _pad_lanes(jnp.concatenate([cos, cos], axis=1)),
        _pad_lanes(jnp.concatenate([-sin, zero], axis=1)),
        _pad_lanes(jnp.concatenate([zero, sin], axis=1)),
    ])
    return jnp.broadcast_to(tabs, (3, rows, LANE)) if tabs.shape[1] == 1 else tabs


def kernel(x_prompt, x_sample, cache_mla_ckv, cache_mla_kpe, cache_mem_k, cache_mem_v, state_gdn_S, state_gdn_conv, page_table, mem_prompt, norm_mix_g, w_in, gdn_conv_w, gdn_A_log, gdn_dt_bias, gdn_out_norm_g, mla_q_a_norm_g, mla_w_q_b, mla_kv_a_norm_g, mla_w_kv_b, mla_qn_nope_g, mla_qn_rope_g, mla_kn_nope_g, mla_kn_rope_g, w_mix_out, norm_mem_g, mem_wq, mem_wk, mem_wv, mem_wo, mem_qn_g, mem_kn_g, norm_ffn_g, ffn_w_gate, ffn_w_up, ffn_w_down):
    depth = w_in.shape[0]
    assert depth == 1, "single-layer trunk"
    bsz, seq, _ = x_prompt.shape
    nseq, dseq, _ = x_sample.shape
    assert dseq == 1, "one new token per decode sequence"
    past_len = page_table.shape[1] * PAGE_SIZE
    n_tok = bsz * seq

    w = w_in[0]
    o_gz = CONV_DIM
    o_b = o_gz + GDN_V
    o_a = o_b + GDN_HEADS
    o_qa = o_a + GDN_HEADS
    o_c = o_qa + Q_LORA
    o_kpe = o_c + KV_LORA
    misc_w = _pad_lanes(jnp.concatenate([w[:, o_kpe:o_kpe + QK_ROPE], w[:, o_b:o_qa]], axis=1))
    w_in_p = jnp.concatenate([w[:, :o_b], w[:, o_qa:o_kpe], misc_w], axis=1).astype(BF16)

    wqb = mla_w_q_b[0].reshape(Q_LORA, MLA_HEADS, QK_NOPE + QK_ROPE)
    wqb = jnp.pad(wqb, ((0, 0), (0, 0), (0, QK_PAD - QK_NOPE - QK_ROPE)))
    wqb = wqb.reshape(Q_LORA, MLA_HEADS * QK_PAD).astype(BF16)
    wkvb = mla_w_kv_b[0].reshape(KV_LORA, MLA_HEADS, QK_NOPE + V_HEAD)
    wk = wkvb[:, :, :QK_NOPE].reshape(KV_LORA, MLA_HEADS * QK_NOPE).astype(BF16)
    wv = wkvb[:, :, QK_NOPE:].reshape(KV_LORA, MLA_HEADS * V_HEAD).astype(BF16)
    wkt = wk.T

    lanep = jnp.zeros((2, LANE), F32)
    lanep = lanep.at[0, MISC_A:MISC_A + GDN_HEADS].set(gdn_A_log[0])
    lanep = lanep.at[1, MISC_A:MISC_A + GDN_HEADS].set(gdn_dt_bias[0])
    rowp = jnp.zeros((2 * SUBLANE,), F32)
    rowp = rowp.at[GDN_HEADS:2 * GDN_HEADS].set(gdn_A_log[0])
    rowp = rowp.at[SUBLANE + GDN_HEADS:SUBLANE + 2 * GDN_HEADS].set(gdn_dt_bias[0])
    rowp = jnp.broadcast_to(rowp[:, None], (2 * SUBLANE, GDN_TB))

    g_mix = _row(norm_mix_g[0])
    g_out = _row(gdn_out_norm_g[0])
    mla_gains = (_row(mla_q_a_norm_g[0]), _row(mla_qn_nope_g[0]), _pad_lanes(_row(mla_qn_rope_g[0])),
                 _row(mla_kv_a_norm_g[0]), _row(mla_kn_nope_g[0]), _pad_lanes(_row(mla_kn_rope_g[0])))
    wmix = w_mix_out[0].astype(BF16)
    wq_mem = mem_wq[0].astype(BF16)
    wk_mem = mem_wk[0].astype(BF16)
    wv_mem = mem_wv[0].astype(BF16)
    wo_mem = mem_wo[0].astype(BF16)
    wg = ffn_w_gate[0].astype(BF16)
    wu = ffn_w_up[0].astype(BF16)
    wd = ffn_w_down[0].astype(BF16)
    g_mem = _row(norm_mem_g[0])
    g_ffn = _row(norm_ffn_g[0])
    g_qn = _row(mem_qn_g[0])
    g_kn = _row(mem_kn_g[0])
    conv_w = gdn_conv_w[0]

    xp = x_prompt.reshape(n_tok, D_MODEL)
    qkv, gz, qa, kvc, misc = _in_proj(xp, g_mix, w_in_p, 256)
    qkv3 = qkv.reshape(bsz, seq, CONV_DIM)
    grow = jnp.swapaxes(misc.reshape(bsz, seq, LANE)[:, :, MISC_B:MISC_B + SUBLANE], 1, 2)
    y_gdn, p_s = _gdn_prompt(qkv3, misc.reshape(bsz, seq, LANE), grow, gz.reshape(bsz, seq, GDN_V),
                             conv_w, lanep, rowp, g_out)
    p_conv = qkv3[:, seq - (CONV_W - 1):, :]

    tabs_p = _rope_tables(jnp.arange(seq), seq)
    q_full, k_full, v_full, p_c, p_kpe = _mla_prep(qa, kvc, misc, tabs_p, wqb, wk, wv, mla_gains, 256)
    y_mla = _mla_flash(q_full, k_full, v_full, bsz, seq)

    mem_k, mem_v = _mem_kv(mem_prompt.reshape(-1, D_MODEL), wk_mem, wv_mem, g_kn, 256)
    n_mem = mem_prompt.shape[1]
    x1, q_mem = _mix_q(xp, y_gdn.reshape(n_tok, GDN_V), y_mla, wmix, g_mem, wq_mem, g_qn, 256)
    o_mem = _mem_attn(q_mem.reshape(bsz, seq, MEM_DIM), mem_k.reshape(bsz, n_mem, MEM_DIM),
                      mem_v.reshape(bsz, n_mem, MEM_DIM), 256)
    y_prompt = _out_ffn(x1, o_mem.reshape(n_tok, MEM_DIM), wo_mem, g_ffn, wg, wu, wd, 256)

    xs = x_sample.reshape(nseq, D_MODEL)
    qkv_s, gz_s, qa_s, kvc_s, misc_s = _in_proj(xs, g_mix, w_in_p, nseq)
    conv_prev = state_gdn_conv[0]
    y_gdn_s, s_new = _gdn_decode(qkv_s, jnp.swapaxes(conv_prev, 0, 1), misc_s, gz_s, state_gdn_S[0],
                                 conv_w, lanep, g_out)
    s_conv = jnp.concatenate([conv_prev[:, 1:, :], qkv_s[:, None, :]], axis=1)

    tabs_s = _rope_tables(jnp.full((1,), past_len), nseq)
    qabs, qpe, c_new, kpe_new = _mla_dec_prep(qa_s, kvc_s, misc_s, tabs_s, wqb, wk, mla_gains)
    pad_rows = lambda a: jnp.pad(jnp.swapaxes(a, 0, 1), ((0, 0), (0, BF16_ROWS - MLA_HEADS), (0, 0)))
    m_run, l_run, acc_run = _mla_dec_attn(
        page_table, cache_mla_ckv[0], jnp.swapaxes(cache_mla_kpe[0], 1, 2),
        pad_rows(qabs).astype(BF16), pad_rows(qpe[:, :, :QK_ROPE]).astype(BF16), wkt)
    heads_first = lambda a: jnp.swapaxes(a[:, :MLA_HEADS], 0, 1)
    y_mla_s = _mla_dec_finish(heads_first(m_run), heads_first(l_run), heads_first(acc_run),
                              qabs, qpe, c_new, kpe_new, wk, wv)

    x1_s, q_mem_s = _mix_q(xs, y_gdn_s, y_mla_s, wmix, g_mem, wq_mem, g_qn, nseq)
    o_mem_s = _mem_attn_dec(q_mem_s.astype(F32).reshape(nseq, MEM_HEADS, MEM_HD),
                            cache_mem_k[0], cache_mem_v[0])
    y_sample = _out_ffn(x1_s, o_mem_s.reshape(nseq, MEM_DIM).astype(BF16),
                        wo_mem, g_ffn, wg, wu, wd, nseq)

    return (
        y_prompt.reshape(bsz, seq, D_MODEL),
        y_sample.reshape(nseq, 1, D_MODEL),
        p_c.reshape(1, bsz, seq, KV_LORA),
        p_kpe.reshape(1, bsz, seq, QK_ROPE),
        mem_k.reshape(1, bsz, n_mem, MEM_HEADS, MEM_HD),
        mem_v.reshape(1, bsz, n_mem, MEM_HEADS, MEM_HD),
        p_s[None],
        p_conv[None],
        c_new.reshape(1, nseq, 1, KV_LORA),
        kpe_new[:, :QK_ROPE].reshape(1, nseq, 1, QK_ROPE),
        s_new[None],
        s_conv[None],
    )
```

```python
import functools

import jax
import jax.numpy as jnp
from jax import lax
from jax.experimental import pallas as pl
from jax.experimental.pallas import tpu as pltpu

F32 = jnp.float32
BF16 = jnp.bfloat16

D_MODEL = 1024
GDN_HEADS = 4
GDN_DK = 128
GDN_DV = 128
CONV_W = 4
CHUNK = 64
MLA_HEADS = 4
Q_LORA = 384
KV_LORA = 256
QK_NOPE = 128
QK_ROPE = 64
V_HEAD = 128
ROPE_THETA = 10000.0
PAGE_SIZE = 128
MEM_HEADS = 4
MEM_HD = 128
EPS = 1e-6
GDN_QK = GDN_HEADS * GDN_DK
GDN_V = GDN_HEADS * GDN_DV
CONV_DIM = 2 * GDN_QK + GDN_V
MLA_SCALE = (QK_NOPE + QK_ROPE) ** -0.5
MEM_DIM = MEM_HEADS * MEM_HD

LANE = 128
SUBLANE = 8
BF16_ROWS = 16
QK_PAD = 2 * LANE
MISC_B = QK_ROPE
MISC_A = QK_ROPE + GDN_HEADS
VMEM_LIMIT = 56 * 1024 * 1024

TOKEN_TILE = 256
MIX_TILE = 512
GDN_TILE = 2 * CHUNK
GDN_TB = 2 * GDN_TILE
PAGES_PER_STEP = 32
DEC_RING = 3


def _cparams(sem):
    return pltpu.CompilerParams(dimension_semantics=sem, vmem_limit_bytes=VMEM_LIMIT)


def _full(shape):
    n = len(shape)
    return pl.BlockSpec(shape, lambda *_: (0,) * n)


def _dot(a, b):
    return jnp.dot(a, b, preferred_element_type=F32)


def _dot_nt(a, b):
    return lax.dot_general(a, b, (((1,), (1,)), ((), ())), preferred_element_type=F32)


def _dot_tn(a, b):
    return lax.dot_general(a, b, (((0,), (0,)), ((), ())), preferred_element_type=F32)


def _split3(x):
    hi = x.astype(BF16)
    r = x - hi.astype(F32)
    mid = r.astype(BF16)
    lo = (r - mid.astype(F32)).astype(BF16)
    return hi, mid, lo


def _dot01_left(m01, x):
    hi, mid, lo = _split3(x)
    return _dot(m01, hi) + _dot(m01, mid) + _dot(m01, lo)


def _dot01_right(x, m01):
    hi, mid, lo = _split3(x)
    return _dot(hi, m01) + _dot(mid, m01) + _dot(lo, m01)


def _rms(x, g, n=None):
    n = x.shape[-1] if n is None else n
    ms = jnp.sum(x * x, axis=-1, keepdims=True) * (1.0 / n)
    return x * lax.rsqrt(ms + EPS) * g


def _silu(x):
    return x * jax.nn.sigmoid(x)


def _softplus(x):
    return jnp.maximum(x, 0.0) + jnp.log1p(jnp.exp(-jnp.abs(x)))


def _iota(shape, dim):
    return lax.broadcasted_iota(jnp.int32, shape, dim)


def _block_id(i, size):
    assert size & (size - 1) == 0
    return lax.shift_right_logical(i, size.bit_length() - 1)


def _in_proj_body(x_ref, g_ref, w_ref, qkv_ref, gz_ref, qa_ref, kvc_ref, misc_ref):
    h = _rms(x_ref[...], g_ref[...]).astype(BF16)
    off = 0
    for ref in (qkv_ref, gz_ref, qa_ref, kvc_ref, misc_ref):
        n = ref.shape[-1]
        ref[...] = _dot(h, w_ref[:, off:off + n])
        off += n


def _in_proj(x, g, w, tm):
    t = x.shape[0]
    widths = (CONV_DIM, GDN_V, Q_LORA, KV_LORA, LANE)
    return pl.pallas_call(
        _in_proj_body,
        grid=(t // tm,),
        in_specs=[pl.BlockSpec((tm, D_MODEL), lambda i: (i, 0)), _full(g.shape), _full(w.shape)],
        out_specs=[pl.BlockSpec((tm, n), lambda i: (i, 0)) for n in widths],
        out_shape=[jax.ShapeDtypeStruct((t, n), F32) for n in widths],
        compiler_params=_cparams(("parallel",)),
        name="in_proj",
    )(x, g, w)


def _gate_lane_values(misc, lanep):
    lane = _iota(misc.shape, 1)
    is_a = (lane >= MISC_A) & (lane < MISC_A + GDN_HEADS)
    beta_l = jax.nn.sigmoid(misc)
    g_l = -jnp.exp(lanep[0:1, :]) * _softplus(jnp.where(is_a, misc, 0.0) + lanep[1:2, :])
    return beta_l, g_l


def _expand_matrix(first_lane):
    r = _iota((LANE, GDN_V), 0)
    c = _iota((LANE, GDN_V), 1)
    return jnp.where(r == first_lane + _block_id(c, GDN_DV), 1.0, 0.0).astype(BF16)


def _gdn_gate_out(o, z, gout):
    return _rms(o, gout) * _silu(z)


def _tri_inverse(a_list, eye, same16, same32):
    def mm(xs, ys):
        return [_dot(x.astype(BF16), y.astype(BF16)) for x, y in zip(xs, ys)]

    def add(ts, us):
        return [t + u for t, u in zip(ts, us)]

    def sub(ts, us):
        return [t - u for t, u in zip(ts, us)]

    ad = [jnp.where(same16, a, 0.0) for a in a_list]
    t = [eye - x for x in ad]
    p = mm(ad, ad)
    t = add(t, mm(t, p))
    p = mm(p, p)
    t = add(t, mm(t, p))
    p = mm(p, p)
    t = add(t, mm(t, p))
    b1 = [jnp.where(same32, a, 0.0) - x for a, x in zip(a_list, ad)]
    t = sub(t, mm(t, mm(b1, t)))
    b2 = [jnp.where(same32, 0.0, a) for a in a_list]
    t = sub(t, mm(t, mm(b2, t)))
    return t


def _gdn_prompt_body(qkv_ref, misc_ref, grow_ref, gz_ref, convw_ref, lanep_ref, rowp_ref, gout_ref,
                     y_ref, s_out_ref, xe_ref, s_ref):
    tb = GDN_TB
    t_idx = pl.program_id(1)

    @pl.when(t_idx == 0)
    def _():
        s_ref[...] = jnp.zeros(s_ref.shape, F32)
        xe_ref[0:SUBLANE, :] = jnp.zeros((SUBLANE, CONV_DIM), F32)

    x = qkv_ref[0]
    xe_ref[SUBLANE:SUBLANE + tb, :] = x
    w = convw_ref[...]
    conv = x * w[CONV_W - 1:CONV_W, :]
    for i in range(CONV_W - 1):
        lo = SUBLANE - (CONV_W - 1) + i
        conv = conv + xe_ref[lo:lo + tb, :] * w[i:i + 1, :]
    xe_ref[0:SUBLANE, :] = x[tb - SUBLANE:tb, :]

    tile = GDN_TILE
    tiles = range(tb // tile)
    rows = [slice(t * tile, (t + 1) * tile) for t in tiles]
    ri = _iota((tile, tile), 0)
    ci = _iota((tile, tile), 1)
    same64 = _block_id(ri, CHUNK) == _block_id(ci, CHUNK)
    lower = same64 & (ci <= ri)
    strict = same64 & (ci < ri)
    same16 = _block_id(ri, 16) == _block_id(ci, 16)
    same32 = _block_id(ri, 32) == _block_id(ci, 32)
    eye = jnp.where(ri == ci, 1.0, 0.0)
    l_incl = jnp.where(lower, 1.0, 0.0).astype(BF16)
    u_incl = jnp.where(same64 & (ri <= ci), 1.0, 0.0).astype(BF16)
    u_strict = jnp.where(same64 & (ci > ri), 1.0, 0.0).astype(BF16)

    beta_l, g_l = _gate_lane_values(misc_ref[0], lanep_ref[...])
    e_b = _expand_matrix(MISC_B)
    e_a = _expand_matrix(MISC_A)
    beta_bc = _dot01_right(beta_l, e_b)
    gc_col = jnp.concatenate([_dot01_left(l_incl, g_l[r]) for r in rows], axis=0)
    gt_col = jnp.concatenate([_dot01_left(u_strict, g_l[r]) for r in rows], axis=0)
    gc_bc = _dot01_right(gc_col, e_a)
    gt_bc = _dot01_right(gt_col, e_a)
    egc = jnp.exp(gc_bc)
    etail = jnp.exp(gt_bc)

    grow = grow_ref[0]
    rowi = _iota(grow.shape, 0)
    g_r = -jnp.exp(rowp_ref[0:SUBLANE, :]) * _softplus(
        jnp.where(rowi >= GDN_HEADS, grow, 0.0) + rowp_ref[SUBLANE:2 * SUBLANE, :])
    gcr = [_dot01_right(g_r[:, r], u_incl) for r in rows]

    gout = gout_ref[...]
    gz = gz_ref[0]
    heads = range(GDN_HEADS)
    sls = [slice(h * GDN_DK, (h + 1) * GDN_DK) for h in heads]
    units = [(t, h) for t in tiles for h in heads]
    qh, kh, kb, vb, dm = {}, {}, {}, {}, {}
    for h in heads:
        q = _silu(conv[:, h * GDN_DK:(h + 1) * GDN_DK])
        k = _silu(conv[:, GDN_QK + h * GDN_DK:GDN_QK + (h + 1) * GDN_DK])
        v = _silu(conv[:, 2 * GDN_QK + h * GDN_DV:2 * GDN_QK + (h + 1) * GDN_DV])
        q = q * lax.rsqrt(jnp.sum(q * q, axis=-1, keepdims=True) + EPS) * (GDN_DK ** -0.5)
        k = k * lax.rsqrt(jnp.sum(k * k, axis=-1, keepdims=True) + EPS)
        bb = beta_bc[:, sls[h]]
        for t in tiles:
            qh[t, h] = q[rows[t]]
            kh[t, h] = k[rows[t]]
            kb[t, h] = (k * bb)[rows[t]]
            vb[t, h] = (v * bb)[rows[t]]
            diff = gc_bc[rows[t], sls[h]] - gcr[t][GDN_HEADS + h:GDN_HEADS + h + 1, :]
            dm[t, h] = jnp.where(lower, jnp.exp(jnp.where(lower, diff, 0.0)), 0.0)
    p = {u: _dot_nt(jnp.concatenate([kb[u], qh[u]], axis=0).astype(BF16), kh[u].astype(BF16))
         for u in units}
    a_mat = [jnp.where(strict, p[u][:tile] * dm[u], 0.0) for u in units]
    attn = {u: (p[u][tile:] * dm[u]).astype(BF16) for u in units}
    t_mat = dict(zip(units, _tri_inverse(a_mat, eye, same16, same32)))
    egc_u = {(t, h): egc[rows[t], sls[h]] for t, h in units}
    uw = {u: _dot(t_mat[u].astype(BF16),
                  jnp.concatenate([vb[u], kb[u] * egc_u[u]], axis=1).astype(BF16)) for u in units}
    qd = {u: qh[u] * egc_u[u] for u in units}
    kt = {(t, h): (kh[t, h] * etail[rows[t], sls[h]]).astype(BF16) for t, h in units}
    s = [s_ref[h] for h in heads]
    o_inter = {u: [] for u in units}
    v_new = {u: [] for u in units}
    for t in tiles:
        for c in range(tile // CHUNK):
            rs = slice(c * CHUNK, (c + 1) * CHUNK)
            last = t * tile + (c + 1) * CHUNK - 1
            wq = [_dot(jnp.concatenate([uw[t, h][rs, GDN_DV:], qd[t, h][rs]], axis=0).astype(BF16),
                       s[h].astype(BF16)) for h in heads]
            vn = [uw[t, h][rs, :GDN_DV] - wq[h][:CHUNK] for h in heads]
            egl = [jnp.exp(gc_bc[last:last + 1, sls[h]]) for h in heads]
            s = [s[h] * egl[h] + _dot_tn(kt[t, h][rs], vn[h].astype(BF16)) for h in heads]
            for h in heads:
                o_inter[t, h].append(wq[h][CHUNK:])
                v_new[t, h].append(vn[h])
    o = {u: jnp.concatenate(o_inter[u], axis=0)
         + _dot(attn[u], jnp.concatenate(v_new[u], axis=0).astype(BF16)) for u in units}
    for h in heads:
        s_ref[h] = s[h]
        for t in tiles:
            y_ref[0, rows[t], sls[h]] = _gdn_gate_out(o[t, h], gz[rows[t], sls[h]], gout).astype(BF16)

    @pl.when(t_idx == pl.num_programs(1) - 1)
    def _():
        s_out_ref[0] = s_ref[...]


def _gdn_prompt(qkv, misc, grow, gz, convw, lanep, rowp, gout):
    b, t, _ = qkv.shape
    tb = GDN_TB
    return pl.pallas_call(
        _gdn_prompt_body,
        grid=(b, t // tb),
        in_specs=[
            pl.BlockSpec((1, tb, CONV_DIM), lambda i, j: (i, j, 0)),
            pl.BlockSpec((1, tb, LANE), lambda i, j: (i, j, 0)),
            pl.BlockSpec((1, SUBLANE, tb), lambda i, j: (i, 0, j)),
            pl.BlockSpec((1, tb, GDN_V), lambda i, j: (i, j, 0)),
            _full(convw.shape), _full(lanep.shape), _full(rowp.shape), _full(gout.shape),
        ],
        out_specs=[
            pl.BlockSpec((1, tb, GDN_V), lambda i, j: (i, j, 0)),
            pl.BlockSpec((1, GDN_HEADS, GDN_DK, GDN_DV), lambda i, j: (i, 0, 0, 0)),
        ],
        out_shape=[
            jax.ShapeDtypeStruct((b, t, GDN_V), BF16),
            jax.ShapeDtypeStruct((b, GDN_HEADS, GDN_DK, GDN_DV), F32),
        ],
        scratch_shapes=[
            pltpu.VMEM((SUBLANE + tb, CONV_DIM), F32),
            pltpu.VMEM((GDN_HEADS, GDN_DK, GDN_DV), F32),
        ],
        compiler_params=_cparams(("parallel", "arbitrary")),
        name="gdn_prompt",
    )(qkv, misc, grow, gz, convw, lanep, rowp, gout)


GDN_DEC_SEQS = 8


def _gdn_decode_body(qkv_ref, cprev_ref, misc_ref, gz_ref, s_in_ref, convw_ref, lanep_ref, gout_ref,
                     y_ref, s_out_ref, o_ref):
    ns = GDN_DEC_SEQS
    w = convw_ref[...]
    conv = qkv_ref[...] * w[CONV_W - 1:CONV_W, :]
    for i in range(CONV_W - 1):
        conv = conv + cprev_ref[i] * w[i:i + 1, :]
    conv = _silu(conv)
    beta_l, g_l = _gate_lane_values(misc_ref[...], lanep_ref[...])
    beta_bc = _dot01_right(beta_l, _expand_matrix(MISC_B))
    eg = jnp.exp(_dot01_right(g_l, _expand_matrix(MISC_A)))
    row0 = jnp.where(_iota((SUBLANE, LANE), 0) == 0, 1.0, 0.0).astype(BF16)
    units = [(h, i) for h in range(GDN_HEADS) for i in range(ns)]
    qn, kn, vv = {}, {}, {}
    for h in range(GDN_HEADS):
        qh = conv[:, h * GDN_DK:(h + 1) * GDN_DK]
        kh = conv[:, GDN_QK + h * GDN_DK:GDN_QK + (h + 1) * GDN_DK]
        qn[h] = qh * lax.rsqrt(jnp.sum(qh * qh, axis=-1, keepdims=True) + EPS) * (GDN_DK ** -0.5)
        kn[h] = kh * lax.rsqrt(jnp.sum(kh * kh, axis=-1, keepdims=True) + EPS)
        vv[h] = conv[:, 2 * GDN_QK + h * GDN_DV:2 * GDN_QK + (h + 1) * GDN_DV]
    sl = lambda h: slice(h * GDN_DK, (h + 1) * GDN_DK)
    k8 = {u: jnp.broadcast_to(kn[u[0]][u[1]:u[1] + 1, :], (SUBLANE, GDN_DK)) for u in units}
    s_dec = {(h, i): s_in_ref[i, h] * eg[i:i + 1, sl(h)] for h, i in units}
    kv = {u: _dot(k8[u].astype(BF16), s_dec[u].astype(BF16))[0:1, :] for u in units}
    kcol = {}
    for u in units:
        k_hi, k_mid, k_lo = _split3(k8[u])
        kcol[u] = _dot_tn(k_hi, row0) + _dot_tn(k_mid, row0) + _dot_tn(k_lo, row0)
    s_new = {}
    for h, i in units:
        delta = (vv[h][i:i + 1, :] - kv[h, i]) * beta_bc[i:i + 1, sl(h)]
        s_new[h, i] = s_dec[h, i] + kcol[h, i] * delta
        s_out_ref[i, h] = s_new[h, i]
    for h, i in units:
        q8 = jnp.broadcast_to(qn[h][i:i + 1, :], (SUBLANE, GDN_DK))
        o_ref[i:i + 1, sl(h)] = _dot(q8.astype(BF16), s_new[h, i].astype(BF16))[0:1, :]
    gout = gout_ref[...]
    gz = gz_ref[...]
    for h in range(GDN_HEADS):
        sl = slice(h * GDN_DV, (h + 1) * GDN_DV)
        y_ref[:, sl] = _gdn_gate_out(o_ref[:, sl], gz[:, sl], gout).astype(BF16)


def _gdn_decode(qkv, cprev, misc, gz, s_in, convw, lanep, gout):
    n = qkv.shape[0]
    ns = GDN_DEC_SEQS
    state_spec = pl.BlockSpec((ns, GDN_HEADS, GDN_DK, GDN_DV), lambda i: (i, 0, 0, 0))
    return pl.pallas_call(
        _gdn_decode_body,
        grid=(n // ns,),
        in_specs=[
            pl.BlockSpec((ns, CONV_DIM), lambda i: (i, 0)),
            pl.BlockSpec((CONV_W - 1, ns, CONV_DIM), lambda i: (0, i, 0)),
            pl.BlockSpec((ns, LANE), lambda i: (i, 0)),
            pl.BlockSpec((ns, GDN_V), lambda i: (i, 0)),
            state_spec,
            _full(convw.shape), _full(lanep.shape), _full(gout.shape),
        ],
        out_specs=[pl.BlockSpec((ns, GDN_V), lambda i: (i, 0)), state_spec],
        out_shape=[
            jax.ShapeDtypeStruct((n, GDN_V), BF16),
            jax.ShapeDtypeStruct(s_in.shape, F32),
        ],
        scratch_shapes=[pltpu.VMEM((ns, GDN_V), F32)],
        compiler_params=_cparams(("parallel",)),
        name="gdn_decode",
    )(qkv, cprev, misc, gz, s_in, convw, lanep, gout)


def _rope_padded(x, cos, sina, sinb):
    return (x * cos + pltpu.roll(x, LANE - QK_ROPE // 2, axis=1) * sina
            + pltpu.roll(x, QK_ROPE // 2, axis=1) * sinb)


def _mla_queries(qa, wqb_ref, g_qa, g_nope, g_rope, cos, sina, sinb):
    q = _dot(_rms(qa, g_qa).astype(BF16), wqb_ref[...])
    out = []
    for h in range(MLA_HEADS):
        qn = _rms(q[:, h * QK_PAD:h * QK_PAD + QK_NOPE], g_nope)
        qp = _rms(q[:, h * QK_PAD + QK_NOPE:(h + 1) * QK_PAD], g_rope, QK_ROPE)
        out.append((qn, _rope_padded(qp, cos, sina, sinb)))
    return out


def _mla_latent(kvc, misc, g_kva, g_krope, cos, sina, sinb):
    c = _rms(kvc, g_kva)
    lane = _iota(misc.shape, 1)
    kp = _rms(jnp.where(lane < QK_ROPE, misc, 0.0), g_krope, QK_ROPE)
    return c, _rope_padded(kp, cos, sina, sinb)


def _mla_prep_body(qa_ref, kvc_ref, misc_ref, tab_ref, wqb_ref, wk_ref, wv_ref,
                   g_qa_ref, g_nope_ref, g_rope_ref, g_kva_ref, g_knope_ref, g_krope_ref,
                   q_ref, k_ref, v_ref, c_ref, kpe_ref):
    cos, sina, sinb = tab_ref[0], tab_ref[1], tab_ref[2]
    qs = _mla_queries(qa_ref[...], wqb_ref, g_qa_ref[...], g_nope_ref[...], g_rope_ref[...],
                      cos, sina, sinb)
    c, kp = _mla_latent(kvc_ref[...], misc_ref[...], g_kva_ref[...], g_krope_ref[...], cos, sina, sinb)
    c_ref[...] = c
    kpe_ref[...] = kp[:, :QK_ROPE]
    cb = c.astype(BF16)
    kn = _dot(cb, wk_ref[...])
    v = _dot(cb, wv_ref[...])
    g_kn = g_knope_ref[...]
    kp16 = kp.astype(BF16)
    for h in range(MLA_HEADS):
        qn, qp = qs[h]
        q_ref[h, :, 0:QK_NOPE] = (qn * MLA_SCALE).astype(BF16)
        q_ref[h, :, QK_NOPE:QK_PAD] = (qp * MLA_SCALE).astype(BF16)
        k_ref[h, :, 0:QK_NOPE] = _rms(kn[:, h * QK_NOPE:(h + 1) * QK_NOPE], g_kn).astype(BF16)
        k_ref[h, :, QK_NOPE:QK_PAD] = kp16
        v_ref[h] = v[:, h * V_HEAD:(h + 1) * V_HEAD].astype(BF16)


def _mla_prep(qa, kvc, misc, tabs, wqb, wk, wv, gains, tm):
    t = qa.shape[0]
    n_tab = tabs.shape[1] // tm
    row = lambda n: pl.BlockSpec((tm, n), lambda i: (i, 0))
    head = lambda n: pl.BlockSpec((MLA_HEADS, tm, n), lambda i: (0, i, 0))
    return pl.pallas_call(
        _mla_prep_body,
        grid=(t // tm,),
        in_specs=[row(Q_LORA), row(KV_LORA), row(LANE),
                  pl.BlockSpec((3, tm, LANE), lambda i: (0, i % n_tab, 0)),
                  _full(wqb.shape), _full(wk.shape), _full(wv.shape)]
                 + [_full(g.shape) for g in gains],
        out_specs=[head(QK_PAD), head(QK_PAD), head(V_HEAD), row(KV_LORA), row(QK_ROPE)],
        out_shape=[
            jax.ShapeDtypeStruct((MLA_HEADS, t, QK_PAD), BF16),
            jax.ShapeDtypeStruct((MLA_HEADS, t, QK_PAD), BF16),
            jax.ShapeDtypeStruct((MLA_HEADS, t, V_HEAD), BF16),
            jax.ShapeDtypeStruct((t, KV_LORA), F32),
            jax.ShapeDtypeStruct((t, QK_ROPE), F32),
        ],
        compiler_params=_cparams(("parallel",)),
        name="mla_prep",
    )(qa, kvc, misc, tabs, wqb, wk, wv, *gains)


FLASH_T = 512


FLASH_HEADS = 2


def _flash_body(q_ref, k_ref, v_ref, o_ref):
    tq = FLASH_T
    qi = pl.program_id(2)
    heads = range(FLASH_HEADS)
    q = [q_ref[h] for h in heads]

    def block(j, carry, masked):
        start = pl.multiple_of(j * tq, tq)
        s = [_dot_nt(q[h], k_ref[h, pl.ds(start, tq), :]) for h in heads]
        if masked:
            keep = _iota(s[0].shape, 1) <= _iota(s[0].shape, 0)
            s = [jnp.where(keep, x, -jnp.inf) for x in s]
        m_new = [jnp.maximum(carry[h][0], jnp.max(s[h], axis=-1, keepdims=True)) for h in heads]
        p = [jnp.exp(s[h] - m_new[h]) for h in heads]
        pv = [_dot(p[h].astype(BF16), v_ref[h, pl.ds(start, tq), :]) for h in heads]
        out = []
        for h in heads:
            m, l, acc = carry[h]
            corr = jnp.exp(m - m_new[h])
            out.append((m_new[h], l * corr + jnp.sum(p[h], axis=-1, keepdims=True),
                        acc * corr + pv[h]))
        return tuple(out)

    init = tuple((jnp.full((tq, 1), -jnp.inf, F32), jnp.zeros((tq, 1), F32),
                  jnp.zeros((tq, V_HEAD), F32)) for _ in heads)
    carry = lax.fori_loop(0, qi, lambda j, c: block(j, c, False), init)
    carry = block(qi, carry, True)
    for h in heads:
        _, l, acc = carry[h]
        o_ref[:, h * V_HEAD:(h + 1) * V_HEAD] = (acc / l).astype(BF16)


def _mla_flash(q, k, v, b, t):
    tq = FLASH_T
    nq = t // tq
    nh = FLASH_HEADS
    return pl.pallas_call(
        _flash_body,
        grid=(b, MLA_HEADS // nh, nq),
        in_specs=[
            pl.BlockSpec((nh, tq, QK_PAD), lambda i, h, j: (h, i * nq + j, 0)),
            pl.BlockSpec((nh, t, QK_PAD), lambda i, h, j: (h, i, 0)),
            pl.BlockSpec((nh, t, V_HEAD), lambda i, h, j: (h, i, 0)),
        ],
        out_specs=pl.BlockSpec((tq, nh * V_HEAD), lambda i, h, j: (i * nq + j, h)),
        out_shape=jax.ShapeDtypeStruct((b * t, MLA_HEADS * V_HEAD), BF16),
        compiler_params=_cparams(("parallel", "parallel", "arbitrary")),
        name="mla_flash",
    )(q, k, v)


def _mla_dec_prep_body(qa_ref, kvc_ref, misc_ref, tab_ref, wqb_ref, wk_ref,
                       g_qa_ref, g_nope_ref, g_rope_ref, g_kva_ref, g_knope_ref, g_krope_ref,
                       qabs_ref, qpe_ref, c_ref, kpe_ref):
    cos, sina, sinb = tab_ref[0], tab_ref[1], tab_ref[2]
    qs = _mla_queries(qa_ref[...], wqb_ref, g_qa_ref[...], g_nope_ref[...], g_rope_ref[...],
                      cos, sina, sinb)
    c, kp = _mla_latent(kvc_ref[...], misc_ref[...], g_kva_ref[...], g_krope_ref[...], cos, sina, sinb)
    c_ref[...] = c
    kpe_ref[...] = kp
    g_kn = g_knope_ref[...]
    for h in range(MLA_HEADS):
        qn, qp = qs[h]
        qabs_ref[h] = _dot_nt((qn * g_kn * MLA_SCALE).astype(BF16),
                              wk_ref[:, h * QK_NOPE:(h + 1) * QK_NOPE])
        qpe_ref[h] = qp * MLA_SCALE


def _mla_dec_prep(qa, kvc, misc, tabs, wqb, wk, gains):
    n = qa.shape[0]
    args = (qa, kvc, misc, tabs, wqb, wk) + tuple(gains)
    return pl.pallas_call(
        _mla_dec_prep_body,
        grid=(1,),
        in_specs=[_full(a.shape) for a in args],
        out_specs=[_full((MLA_HEADS, n, KV_LORA)), _full((MLA_HEADS, n, LANE)),
                   _full((n, KV_LORA)), _full((n, LANE))],
        out_shape=[
            jax.ShapeDtypeStruct((MLA_HEADS, n, KV_LORA), F32),
            jax.ShapeDtypeStruct((MLA_HEADS, n, LANE), F32),
            jax.ShapeDtypeStruct((n, KV_LORA), F32),
            jax.ShapeDtypeStruct((n, LANE), F32),
        ],
        compiler_params=_cparams(("arbitrary",)),
        name="mla_dec_prep",
    )(*args)


def _mla_dec_attn_body(pt_ref, qabs_ref, qpe_ref, wkt_ref, ckv_hbm, kpe_hbm,
                       m_out, l_out, acc_out, lhs_ref, cbuf, kbuf, c16, sem_c, sem_k, *, blocks_per_seq):
    npg = PAGES_PER_STEP
    b = pl.program_id(0)
    total_blocks = pl.num_programs(0) * blocks_per_seq
    nk = MLA_HEADS * QK_NOPE
    keys = npg * PAGE_SIZE

    def block_copies(gb, page_of):
        slot = gb % DEC_RING
        cps = []
        for i in range(npg):
            page = page_of(gb * npg + i)
            cps.append(pltpu.make_async_copy(ckv_hbm.at[page], cbuf.at[slot, i], sem_c.at[slot]))
            cps.append(pltpu.make_async_copy(kpe_hbm.at[page], kbuf.at[slot, i], sem_k.at[slot]))
        return cps

    def start_block(gb):
        for cp in block_copies(gb, lambda idx: pt_ref[idx]):
            cp.start()

    def wait_block(gb):
        for cp in block_copies(gb, lambda idx: 0):
            cp.wait()

    @pl.when(b == 0)
    def _():
        lhs_ref[0:nk, :] = wkt_ref[...]
        for gb in range(DEC_RING - 1):
            start_block(gb)

    lhs_ref[nk:nk + BF16_ROWS, :] = qabs_ref[0]
    qpe = qpe_ref[0]

    def scores(j):
        slot = (b * blocks_per_seq + j) % DEC_RING
        c16[j % 2] = cbuf[slot].reshape(keys, KV_LORA).astype(BF16)
        kpt = jnp.concatenate([kbuf[slot, i] for i in range(npg)], axis=1).astype(BF16)
        kt = _dot_nt(lhs_ref[...], c16[j % 2])
        ssq = [jnp.sum(jnp.square(kt[h * QK_NOPE:(h + 1) * QK_NOPE]), axis=0, keepdims=True)
               for h in range(MLA_HEADS)]
        ssq = jnp.concatenate(ssq + [jnp.ones((SUBLANE - MLA_HEADS, keys), F32)], axis=0)
        pe = _dot(qpe, kpt)
        return kt[nk:nk + SUBLANE] * lax.rsqrt(ssq * (1.0 / QK_NOPE) + EPS) + pe[0:SUBLANE]

    def accumulate(j, s, carry):
        m_old, l_old, acc_old = carry
        m_new = jnp.maximum(m_old, jnp.max(s, axis=-1, keepdims=True))
        p = jnp.exp(s - m_new)
        corr = jnp.exp(m_old - m_new)
        l_new = l_old * corr + jnp.sum(p, axis=-1, keepdims=True)
        acc_new = acc_old * corr + _dot(p.astype(BF16), c16[j % 2])
        return m_new, l_new, acc_new

    def fetch_and_score(j, after):
        gb = b * blocks_per_seq + j
        nxt = gb + (DEC_RING - 1)

        @pl.when(nxt < total_blocks)
        def _():
            start_block(nxt)

        wait_block(gb)
        s_cur = scores(j)
        return s_cur, after()

    def body(j, carry):
        s_prev, state = carry
        return fetch_and_score(j, lambda: accumulate(j - 1, s_prev, state))

    init = (jnp.full((SUBLANE, 1), -jnp.inf, F32), jnp.zeros((SUBLANE, 1), F32),
            jnp.zeros((SUBLANE, KV_LORA), F32))
    s_last, state = lax.fori_loop(1, blocks_per_seq, body, fetch_and_score(0, lambda: init))
    m_fin, l_fin, acc_fin = accumulate(blocks_per_seq - 1, s_last, state)
    m_out[0] = jnp.broadcast_to(m_fin, m_out.shape[1:])
    l_out[0] = jnp.broadcast_to(l_fin, l_out.shape[1:])
    acc_out[0] = acc_fin


def _mla_dec_attn(page_table, ckv_pool, kpe_pool, qabs, qpe, wkt):
    n, n_pages = page_table.shape
    npg = PAGES_PER_STEP
    nk = MLA_HEADS * QK_NOPE

    assert n_pages % npg == 0 and n * (n_pages // npg) >= DEC_RING
    seq = lambda shape: pl.BlockSpec((1,) + shape, lambda b, pt: (b, 0, 0))
    grid_spec = pltpu.PrefetchScalarGridSpec(
        num_scalar_prefetch=1,
        grid=(n,),
        in_specs=[seq((BF16_ROWS, KV_LORA)), seq((BF16_ROWS, QK_ROPE)),
                  pl.BlockSpec(wkt.shape, lambda b, pt: (0, 0)),
                  pl.BlockSpec(memory_space=pl.ANY), pl.BlockSpec(memory_space=pl.ANY)],
        out_specs=[seq((SUBLANE, LANE)), seq((SUBLANE, LANE)), seq((SUBLANE, KV_LORA))],
        scratch_shapes=[
            pltpu.VMEM((nk + BF16_ROWS, KV_LORA), BF16),
            pltpu.VMEM((DEC_RING, npg, PAGE_SIZE, KV_LORA), F32),
            pltpu.VMEM((DEC_RING, npg, QK_ROPE, PAGE_SIZE), F32),
            pltpu.VMEM((2, npg * PAGE_SIZE, KV_LORA), BF16),
            pltpu.SemaphoreType.DMA((DEC_RING,)),
            pltpu.SemaphoreType.DMA((DEC_RING,)),
        ],
    )
    return pl.pallas_call(
        functools.partial(_mla_dec_attn_body, blocks_per_seq=n_pages // npg),
        grid_spec=grid_spec,
        out_shape=[
            jax.ShapeDtypeStruct((n, SUBLANE, LANE), F32),
            jax.ShapeDtypeStruct((n, SUBLANE, LANE), F32),
            jax.ShapeDtypeStruct((n, SUBLANE, KV_LORA), F32),
        ],
        compiler_params=_cparams(("arbitrary",)),
        name="mla_dec_attn",
    )(page_table.reshape(-1), qabs, qpe, wkt, ckv_pool, kpe_pool)


def _mla_dec_finish_body(m_ref, l_ref, acc_ref, qabs_ref, qpe_ref, c_ref, kpe_ref, wk_ref, wv_ref,
                         y_ref):
    c = c_ref[...]
    kp = kpe_ref[...]
    cb = c.astype(BF16)
    kn = _dot(cb, wk_ref[...])
    for h in range(MLA_HEADS):
        knh = kn[:, h * QK_NOPE:(h + 1) * QK_NOPE]
        r = lax.rsqrt(jnp.sum(knh * knh, axis=-1, keepdims=True) * (1.0 / QK_NOPE) + EPS)
        s_new = (r * jnp.sum(qabs_ref[h] * c, axis=-1, keepdims=True)
                 + jnp.sum(qpe_ref[h] * kp, axis=-1, keepdims=True))
        m_old = m_ref[h][:, 0:1]
        m_new = jnp.maximum(m_old, s_new)
        p = jnp.exp(s_new - m_new)
        corr = jnp.exp(m_old - m_new)
        l = l_ref[h][:, 0:1] * corr + p
        lat = (acc_ref[h] * corr + p * c) / l
        y_ref[:, h * V_HEAD:(h + 1) * V_HEAD] = _dot(
            lat.astype(BF16), wv_ref[:, h * V_HEAD:(h + 1) * V_HEAD]).astype(BF16)


def _mla_dec_finish(m, l, acc, qabs, qpe, c, kpe, wk, wv):
    n = c.shape[0]
    args = (m, l, acc, qabs, qpe, c, kpe, wk, wv)
    return pl.pallas_call(
        _mla_dec_finish_body,
        grid=(1,),
        in_specs=[_full(a.shape) for a in args],
        out_specs=_full((n, MLA_HEADS * V_HEAD)),
        out_shape=jax.ShapeDtypeStruct((n, MLA_HEADS * V_HEAD), BF16),
        compiler_params=_cparams(("arbitrary",)),
        name="mla_dec_finish",
    )(*args)


def _mix_q_body(x_ref, yg_ref, ym_ref, wmix_ref, g_mem_ref, wq_ref, g_qn_ref, x1_ref, q_ref):
    x1 = (x_ref[...] + _dot(yg_ref[...], wmix_ref[0:GDN_V, :])
          + _dot(ym_ref[...], wmix_ref[GDN_V:, :]))
    x1_ref[...] = x1
    q = _dot(_rms(x1, g_mem_ref[...]).astype(BF16), wq_ref[...])
    g_qn = g_qn_ref[...]
    for h in range(MEM_HEADS):
        sl = slice(h * MEM_HD, (h + 1) * MEM_HD)
        q_ref[:, sl] = (_rms(q[:, sl], g_qn) * (MEM_HD ** -0.5)).astype(BF16)


def _mix_q(x, yg, ym, wmix, g_mem, wq, g_qn, tm):
    t = x.shape[0]
    row = lambda n: pl.BlockSpec((tm, n), lambda i: (i, 0))
    return pl.pallas_call(
        _mix_q_body,
        grid=(t // tm,),
        in_specs=[row(D_MODEL), row(GDN_V), row(MLA_HEADS * V_HEAD), _full(wmix.shape),
                  _full(g_mem.shape), _full(wq.shape), _full(g_qn.shape)],
        out_specs=[row(D_MODEL), row(MEM_DIM)],
        out_shape=[jax.ShapeDtypeStruct((t, D_MODEL), F32), jax.ShapeDtypeStruct((t, MEM_DIM), BF16)],
        compiler_params=_cparams(("parallel",)),
        name="mix_q",
    )(x, yg, ym, wmix, g_mem, wq, g_qn)


def _mem_attn_body(q_ref, k_ref, v_ref, o_ref):
    q = q_ref[0]
    heads = range(MEM_HEADS)
    sls = [slice(h * MEM_HD, (h + 1) * MEM_HD) for h in heads]
    s = [_dot_nt(q[:, sls[h]], k_ref[0, :, sls[h]].astype(BF16)) for h in heads]
    p = [jnp.exp(s[h] - jnp.max(s[h], axis=-1, keepdims=True)) for h in heads]
    o = [_dot(p[h].astype(BF16), v_ref[0, :, sls[h]].astype(BF16)) for h in heads]
    for h in heads:
        o_ref[0, :, sls[h]] = (o[h] / jnp.sum(p[h], axis=-1, keepdims=True)).astype(BF16)


def _mem_attn(q, k, v, tq):
    nb, t, _ = q.shape
    m = k.shape[1]
    return pl.pallas_call(
        _mem_attn_body,
        grid=(nb, t // tq),
        in_specs=[pl.BlockSpec((1, tq, MEM_DIM), lambda i, j: (i, j, 0)),
                  pl.BlockSpec((1, m, MEM_DIM), lambda i, j: (i, 0, 0)),
                  pl.BlockSpec((1, m, MEM_DIM), lambda i, j: (i, 0, 0))],
        out_specs=pl.BlockSpec((1, tq, MEM_DIM), lambda i, j: (i, j, 0)),
        out_shape=jax.ShapeDtypeStruct((nb, t, MEM_DIM), BF16),
        compiler_params=_cparams(("parallel", "arbitrary")),
        name="mem_attn",
    )(q, k, v)


def _mem_attn_dec_body(q_ref, k_ref, v_ref, o_ref):
    q = q_ref[...]
    s = jnp.sum(k_ref[0] * q, axis=-1, keepdims=True)
    p = jnp.exp(s - jnp.max(s, axis=0, keepdims=True))
    o_ref[...] = jnp.sum(p * v_ref[0], axis=0, keepdims=True) / jnp.sum(p, axis=0, keepdims=True)


def _mem_attn_dec(q, k, v):
    n, m, nh, hd = k.shape
    kv_spec = pl.BlockSpec((1, m, nh, hd), lambda i: (i, 0, 0, 0))
    return pl.pallas_call(
        _mem_attn_dec_body,
        grid=(n,),
        in_specs=[pl.BlockSpec((1, nh, hd), lambda i: (i, 0, 0)), kv_spec, kv_spec],
        out_specs=pl.BlockSpec((1, nh, hd), lambda i: (i, 0, 0)),
        out_shape=jax.ShapeDtypeStruct((n, nh, hd), F32),
        compiler_params=_cparams(("parallel",)),
        name="mem_attn_dec",
    )(q, k, v)


def _out_ffn_body(x1_ref, o_ref, wo_ref, g_ffn_ref, wg_ref, wu_ref, wd_ref, y_ref):
    x2 = x1_ref[...] + _dot(o_ref[...], wo_ref[...])
    h = _rms(x2, g_ffn_ref[...]).astype(BF16)
    act = (_silu(_dot(h, wg_ref[...])) * _dot(h, wu_ref[...])).astype(BF16)
    y_ref[...] = x2 + _dot(act, wd_ref[...])


def _out_ffn(x1, o, wo, g_ffn, wg, wu, wd, tm):
    t = x1.shape[0]
    row = lambda n: pl.BlockSpec((tm, n), lambda i: (i, 0))
    const = lambda a: pl.BlockSpec(a.shape, lambda i: (0,) * a.ndim, pipeline_mode=pl.Buffered(1))
    return pl.pallas_call(
        _out_ffn_body,
        grid=(t // tm,),
        in_specs=[row(D_MODEL), row(MEM_DIM), const(wo), const(g_ffn), const(wg), const(wu), const(wd)],
        out_specs=row(D_MODEL),
        out_shape=jax.ShapeDtypeStruct((t, D_MODEL), F32),
        compiler_params=_cparams(("parallel",)),
        name="out_ffn",
    )(x1, o, wo, g_ffn, wg, wu, wd)


def _mem_kv_body(mem_ref, wk_ref, wv_ref, g_ref, k_ref, v_ref):
    mb = mem_ref[...].astype(BF16)
    k = _dot(mb, wk_ref[...])
    g = g_ref[...]
    for h in range(MEM_HEADS):
        sl = slice(h * MEM_HD, (h + 1) * MEM_HD)
        k_ref[:, sl] = _rms(k[:, sl], g)
    v_ref[...] = _dot(mb, wv_ref[...])


def _mem_kv(mem, wk, wv, g, tm):
    t = mem.shape[0]
    row = lambda n: pl.BlockSpec((tm, n), lambda i: (i, 0))
    return pl.pallas_call(
        _mem_kv_body,
        grid=(t // tm,),
        in_specs=[row(D_MODEL), _full(wk.shape), _full(wv.shape), _full(g.shape)],
        out_specs=[row(MEM_DIM), row(MEM_DIM)],
        out_shape=[jax.ShapeDtypeStruct((t, MEM_DIM), F32)] * 2,
        compiler_params=_cparams(("parallel",)),
        name="mem_kv",
    )(mem, wk, wv, g)


def _row(v):
    return v.reshape(1, -1).astype(F32)


def _pad_lanes(v, width=LANE):
    return jnp.pad(v, ((0, 0), (0, width - v.shape[1])))


def _rope_tables(pos, rows):
    half = QK_ROPE // 2
    inv = ROPE_THETA ** (-jnp.arange(half, dtype=F32) / half)
    ang = pos.astype(F32)[:, None] * inv[None, :]
    cos, sin = jnp.cos(ang), jnp.sin(ang)
    zero = jnp.zeros_like(cos)
    tabs = jnp.stack([
        _pad_lanes(jnp.concatenate([cos, cos], axis=1)),
        _pad_lanes(jnp.concatenate([-sin, zero], axis=1)),
        _pad_lanes(jnp.concatenate([zero, sin], axis=1)),
    ])
    return jnp.broadcast_to(tabs, (3, rows, LANE)) if tabs.shape[1] == 1 else tabs


def kernel(x_prompt, x_sample, cache_mla_ckv, cache_mla_kpe, cache_mem_k, cache_mem_v, state_gdn_S, state_gdn_conv, page_table, mem_prompt, norm_mix_g, w_in, gdn_conv_w, gdn_A_log, gdn_dt_bias, gdn_out_norm_g, mla_q_a_norm_g, mla_w_q_b, mla_kv_a_norm_g, mla_w_kv_b, mla_qn_nope_g, mla_qn_rope_g, mla_kn_nope_g, mla_kn_rope_g, w_mix_out, norm_mem_g, mem_wq, mem_wk, mem_wv, mem_wo, mem_qn_g, mem_kn_g, norm_ffn_g, ffn_w_gate, ffn_w_up, ffn_w_down):
    depth = w_in.shape[0]
    assert depth == 1, "single-layer trunk"
    bsz, seq, _ = x_prompt.shape
    nseq, dseq, _ = x_sample.shape
    assert dseq == 1, "one new token per decode sequence"
    past_len = page_table.shape[1] * PAGE_SIZE
    n_tok = bsz * seq

    w = w_in[0]
    o_gz = CONV_DIM
    o_b = o_gz + GDN_V
    o_a = o_b + GDN_HEADS
    o_qa = o_a + GDN_HEADS
    o_c = o_qa + Q_LORA
    o_kpe = o_c + KV_LORA
    misc_w = _pad_lanes(jnp.concatenate([w[:, o_kpe:o_kpe + QK_ROPE], w[:, o_b:o_qa]], axis=1))
    w_in_p = jnp.concatenate([w[:, :o_b], w[:, o_qa:o_kpe], misc_w], axis=1).astype(BF16)

    wqb = mla_w_q_b[0].reshape(Q_LORA, MLA_HEADS, QK_NOPE + QK_ROPE)
    wqb = jnp.pad(wqb, ((0, 0), (0, 0), (0, QK_PAD - QK_NOPE - QK_ROPE)))
    wqb = wqb.reshape(Q_LORA, MLA_HEADS * QK_PAD).astype(BF16)
    wkvb = mla_w_kv_b[0].reshape(KV_LORA, MLA_HEADS, QK_NOPE + V_HEAD)
    wk = wkvb[:, :, :QK_NOPE].reshape(KV_LORA, MLA_HEADS * QK_NOPE).astype(BF16)
    wv = wkvb[:, :, QK_NOPE:].reshape(KV_LORA, MLA_HEADS * V_HEAD).astype(BF16)
    wkt = wk.T

    lanep = jnp.zeros((2, LANE), F32)
    lanep = lanep.at[0, MISC_A:MISC_A + GDN_HEADS].set(gdn_A_log[0])
    lanep = lanep.at[1, MISC_A:MISC_A + GDN_HEADS].set(gdn_dt_bias[0])
    rowp = jnp.zeros((2 * SUBLANE,), F32)
    rowp = rowp.at[GDN_HEADS:2 * GDN_HEADS].set(gdn_A_log[0])
    rowp = rowp.at[SUBLANE + GDN_HEADS:SUBLANE + 2 * GDN_HEADS].set(gdn_dt_bias[0])
    rowp = jnp.broadcast_to(rowp[:, None], (2 * SUBLANE, GDN_TB))

    g_mix = _row(norm_mix_g[0])
    g_out = _row(gdn_out_norm_g[0])
    mla_gains = (_row(mla_q_a_norm_g[0]), _row(mla_qn_nope_g[0]), _pad_lanes(_row(mla_qn_rope_g[0])),
                 _row(mla_kv_a_norm_g[0]), _row(mla_kn_nope_g[0]), _pad_lanes(_row(mla_kn_rope_g[0])))
    wmix = w_mix_out[0].astype(BF16)
    wq_mem = mem_wq[0].astype(BF16)
    wk_mem = mem_wk[0].astype(BF16)
    wv_mem = mem_wv[0].astype(BF16)
    wo_mem = mem_wo[0].astype(BF16)
    wg = ffn_w_gate[0].astype(BF16)
    wu = ffn_w_up[0].astype(BF16)
    wd = ffn_w_down[0].astype(BF16)
    g_mem = _row(norm_mem_g[0])
    g_ffn = _row(norm_ffn_g[0])
    g_qn = _row(mem_qn_g[0])
    g_kn = _row(mem_kn_g[0])
    conv_w = gdn_conv_w[0]

    xp = x_prompt.reshape(n_tok, D_MODEL)
    qkv, gz, qa, kvc, misc = _in_proj(xp, g_mix, w_in_p, TOKEN_TILE)
    qkv3 = qkv.reshape(bsz, seq, CONV_DIM)
    grow = jnp.swapaxes(misc.reshape(bsz, seq, LANE)[:, :, MISC_B:MISC_B + SUBLANE], 1, 2)
    y_gdn, p_s = _gdn_prompt(qkv3, misc.reshape(bsz, seq, LANE), grow, gz.reshape(bsz, seq, GDN_V),
                             conv_w, lanep, rowp, g_out)
    p_conv = qkv3[:, seq - (CONV_W - 1):, :]

    tabs_p = _rope_tables(jnp.arange(seq), seq)
    q_full, k_full, v_full, p_c, p_kpe = _mla_prep(qa, kvc, misc, tabs_p, wqb, wk, wv, mla_gains, TOKEN_TILE)
    y_mla = _mla_flash(q_full, k_full, v_full, bsz, seq)

    mem_k, mem_v = _mem_kv(mem_prompt.reshape(-1, D_MODEL), wk_mem, wv_mem, g_kn, TOKEN_TILE)
    n_mem = mem_prompt.shape[1]
    x1, q_mem = _mix_q(xp, y_gdn.reshape(n_tok, GDN_V), y_mla, wmix, g_mem, wq_mem, g_qn, MIX_TILE)
    o_mem = _mem_attn(q_mem.reshape(bsz, seq, MEM_DIM), mem_k.reshape(bsz, n_mem, MEM_DIM),
                      mem_v.reshape(bsz, n_mem, MEM_DIM), TOKEN_TILE)
    y_prompt = _out_ffn(x1, o_mem.reshape(n_tok, MEM_DIM), wo_mem, g_ffn, wg, wu, wd, TOKEN_TILE)

    xs = x_sample.reshape(nseq, D_MODEL)
    qkv_s, gz_s, qa_s, kvc_s, misc_s = _in_proj(xs, g_mix, w_in_p, nseq)
    conv_prev = state_gdn_conv[0]
    y_gdn_s, s_new = _gdn_decode(qkv_s, jnp.swapaxes(conv_prev, 0, 1), misc_s, gz_s, state_gdn_S[0],
                                 conv_w, lanep, g_out)
    s_conv = jnp.concatenate([conv_prev[:, 1:, :], qkv_s[:, None, :]], axis=1)

    tabs_s = _rope_tables(jnp.full((1,), past_len), nseq)
    qabs, qpe, c_new, kpe_new = _mla_dec_prep(qa_s, kvc_s, misc_s, tabs_s, wqb, wk, mla_gains)
    pad_rows = lambda a: jnp.pad(jnp.swapaxes(a, 0, 1), ((0, 0), (0, BF16_ROWS - MLA_HEADS), (0, 0)))
    m_run, l_run, acc_run = _mla_dec_attn(
        page_table, cache_mla_ckv[0], jnp.swapaxes(cache_mla_kpe[0], 1, 2),
        pad_rows(qabs).astype(BF16), pad_rows(qpe[:, :, :QK_ROPE]).astype(BF16), wkt)
    heads_first = lambda a: jnp.swapaxes(a[:, :MLA_HEADS], 0, 1)
    y_mla_s = _mla_dec_finish(heads_first(m_run), heads_first(l_run), heads_first(acc_run),
                              qabs, qpe, c_new, kpe_new, wk, wv)

    x1_s, q_mem_s = _mix_q(xs, y_gdn_s, y_mla_s, wmix, g_mem, wq_mem, g_qn, nseq)
    o_mem_s = _mem_attn_dec(q_mem_s.astype(F32).reshape(nseq, MEM_HEADS, MEM_HD),
                            cache_mem_k[0], cache_mem_v[0])
    y_sample = _out_ffn(x1_s, o_mem_s.reshape(nseq, MEM_DIM).astype(BF16),
                        wo_mem, g_ffn, wg, wu, wd, nseq)

    return (
        y_prompt.reshape(bsz, seq, D_MODEL),
        y_sample.reshape(nseq, 1, D_MODEL),
        p_c.reshape(1, bsz, seq, KV_LORA),
        p_kpe.reshape(1, bsz, seq, QK_ROPE),
        mem_k.reshape(1, bsz, n_mem, MEM_HEADS, MEM_HD),
        mem_v.reshape(1, bsz, n_mem, MEM_HEADS, MEM_HD),
        p_s[None],
        p_conv[None],
        c_new.reshape(1, nseq, 1, KV_LORA),
        kpe_new[:, :QK_ROPE].reshape(1, nseq, 1, QK_ROPE),
        s_new[None],
        s_conv[None],
    )
```

```python
import functools

import jax
import jax.numpy as jnp
from jax import lax
from jax.experimental import pallas as pl
from jax.experimental.pallas import tpu as pltpu

F32 = jnp.float32
BF16 = jnp.bfloat16

D_MODEL = 1024
GDN_HEADS = 4
GDN_DK = 128
GDN_DV = 128
CONV_W = 4
CHUNK = 64
MLA_HEADS = 4
Q_LORA = 384
KV_LORA = 256
QK_NOPE = 128
QK_ROPE = 64
V_HEAD = 128
ROPE_THETA = 10000.0
PAGE_SIZE = 128
MEM_HEADS = 4
MEM_HD = 128
EPS = 1e-6
GDN_QK = GDN_HEADS * GDN_DK
GDN_V = GDN_HEADS * GDN_DV
CONV_DIM = 2 * GDN_QK + GDN_V
MLA_SCALE = (QK_NOPE + QK_ROPE) ** -0.5
MEM_DIM = MEM_HEADS * MEM_HD

LANE = 128
SUBLANE = 8
BF16_ROWS = 16
QK_PAD = 2 * LANE
MISC_B = QK_ROPE
MISC_A = QK_ROPE + GDN_HEADS
VMEM_LIMIT = 56 * 1024 * 1024

TOKEN_TILE = 256
MIX_TILE = 512
GDN_TILE = 2 * CHUNK
GDN_TB = 2 * GDN_TILE
PAGES_PER_STEP = 32
DEC_RING = 4


def _cparams(sem):
    return pltpu.CompilerParams(dimension_semantics=sem, vmem_limit_bytes=VMEM_LIMIT)


def _full(shape):
    n = len(shape)
    return pl.BlockSpec(shape, lambda *_: (0,) * n)


def _dot(a, b):
    return jnp.dot(a, b, preferred_element_type=F32)


def _dot_nt(a, b):
    return lax.dot_general(a, b, (((1,), (1,)), ((), ())), preferred_element_type=F32)


def _dot_tn(a, b):
    return lax.dot_general(a, b, (((0,), (0,)), ((), ())), preferred_element_type=F32)


def _split3(x):
    hi = x.astype(BF16)
    r = x - hi.astype(F32)
    mid = r.astype(BF16)
    lo = (r - mid.astype(F32)).astype(BF16)
    return hi, mid, lo


def _dot01_left(m01, x):
    hi, mid, lo = _split3(x)
    return _dot(m01, hi) + _dot(m01, mid) + _dot(m01, lo)


def _dot01_right(x, m01):
    hi, mid, lo = _split3(x)
    return _dot(hi, m01) + _dot(mid, m01) + _dot(lo, m01)


def _rms(x, g, n=None):
    n = x.shape[-1] if n is None else n
    ms = jnp.sum(x * x, axis=-1, keepdims=True) * (1.0 / n)
    return x * lax.rsqrt(ms + EPS) * g


def _silu(x):
    return x * jax.nn.sigmoid(x)


def _softplus(x):
    return jnp.maximum(x, 0.0) + jnp.log1p(jnp.exp(-jnp.abs(x)))


def _iota(shape, dim):
    return lax.broadcasted_iota(jnp.int32, shape, dim)


def _block_id(i, size):
    assert size & (size - 1) == 0
    return lax.shift_right_logical(i, size.bit_length() - 1)


def _in_proj_body(x_ref, g_ref, w_ref, qkv_ref, gz_ref, qa_ref, kvc_ref, misc_ref):
    h = _rms(x_ref[...], g_ref[...]).astype(BF16)
    off = 0
    for ref in (qkv_ref, gz_ref, qa_ref, kvc_ref, misc_ref):
        n = ref.shape[-1]
        ref[...] = _dot(h, w_ref[:, off:off + n])
        off += n


def _in_proj(x, g, w, tm):
    t = x.shape[0]
    widths = (CONV_DIM, GDN_V, Q_LORA, KV_LORA, LANE)
    return pl.pallas_call(
        _in_proj_body,
        grid=(t // tm,),
        in_specs=[pl.BlockSpec((tm, D_MODEL), lambda i: (i, 0)), _full(g.shape), _full(w.shape)],
        out_specs=[pl.BlockSpec((tm, n), lambda i: (i, 0)) for n in widths],
        out_shape=[jax.ShapeDtypeStruct((t, n), F32) for n in widths],
        compiler_params=_cparams(("parallel",)),
        name="in_proj",
    )(x, g, w)


def _qk_l2norm(x, scale):
    return x * (lax.rsqrt(jnp.sum(x * x, axis=-1, keepdims=True) + EPS) * scale)


def _in_proj_conv_body(x_ref, g_ref, w_ref, convw_ref, qkv_ref, tail_ref, gz_ref, qa_ref, kvc_ref,
                       misc_ref, xe_ref, *, tiles_per_seq):
    tm = x_ref.shape[0]
    @pl.when(pl.program_id(0) % tiles_per_seq == 0)
    def _():
        xe_ref[0:SUBLANE, :] = jnp.zeros((SUBLANE, CONV_DIM), F32)

    h = _rms(x_ref[...], g_ref[...]).astype(BF16)
    w = convw_ref[...]
    pair = 2 * GDN_DK
    blocks = [slice(blk * pair, (blk + 1) * pair) for blk in range(CONV_DIM // pair)]
    for sl in blocks:
        xe_ref[SUBLANE:SUBLANE + tm, sl] = _dot(h, w_ref[:, sl])
    off = CONV_DIM
    for ref in (gz_ref, qa_ref, kvc_ref, misc_ref):
        n = ref.shape[-1]
        ref[...] = _dot(h, w_ref[:, off:off + n])
        off += n
    for blk, sl in enumerate(blocks):
        conv = xe_ref[SUBLANE:SUBLANE + tm, sl] * w[CONV_W - 1:CONV_W, sl]
        for i in range(CONV_W - 1):
            lo = SUBLANE - (CONV_W - 1) + i
            conv = conv + xe_ref[lo:lo + tm, sl] * w[i:i + 1, sl]
        act = _silu(conv)
        for half in range(2):
            head_blk = 2 * blk + half
            hs = slice(half * GDN_DK, (half + 1) * GDN_DK)
            out_sl = slice(head_blk * GDN_DK, (head_blk + 1) * GDN_DK)
            if head_blk < GDN_HEADS:
                qkv_ref[:, out_sl] = _qk_l2norm(act[:, hs], GDN_DK ** -0.5)
            elif head_blk < 2 * GDN_HEADS:
                qkv_ref[:, out_sl] = _qk_l2norm(act[:, hs], 1.0)
            else:
                qkv_ref[:, out_sl] = act[:, hs]
    tail = xe_ref[tm:SUBLANE + tm, :]
    tail_ref[...] = tail
    xe_ref[0:SUBLANE, :] = tail


def _in_proj_conv(x, g, w, convw, tm, tiles_per_seq):
    t = x.shape[0]
    widths = (CONV_DIM, CONV_DIM, GDN_V, Q_LORA, KV_LORA, LANE)
    rows = (tm, SUBLANE, tm, tm, tm, tm)
    return pl.pallas_call(
        functools.partial(_in_proj_conv_body, tiles_per_seq=tiles_per_seq),
        grid=(t // tm,),
        in_specs=[pl.BlockSpec((tm, D_MODEL), lambda i: (i, 0)), _full(g.shape), _full(w.shape),
                  _full(convw.shape)],
        out_specs=[pl.BlockSpec((r, n), lambda i: (i, 0)) for r, n in zip(rows, widths)],
        out_shape=[jax.ShapeDtypeStruct((t // tm * r, n), F32) for r, n in zip(rows, widths)],
        scratch_shapes=[pltpu.VMEM((SUBLANE + tm, CONV_DIM), F32)],
        compiler_params=_cparams(("arbitrary",)),
        name="in_proj_conv",
    )(x, g, w, convw)


def _gate_lane_values(misc, lanep):
    lane = _iota(misc.shape, 1)
    is_a = (lane >= MISC_A) & (lane < MISC_A + GDN_HEADS)
    beta_l = jax.nn.sigmoid(misc)
    g_l = -jnp.exp(lanep[0:1, :]) * _softplus(jnp.where(is_a, misc, 0.0) + lanep[1:2, :])
    return beta_l, g_l


def _expand_matrix(first_lane):
    r = _iota((LANE, GDN_V), 0)
    c = _iota((LANE, GDN_V), 1)
    return jnp.where(r == first_lane + _block_id(c, GDN_DV), 1.0, 0.0).astype(BF16)


def _gdn_gate_out(o, z, gout):
    return _rms(o, gout) * _silu(z)


def _tri_inverse(a_list, eye, same16, same32):
    def mm(xs, ys):
        return [_dot(x.astype(BF16), y.astype(BF16)) for x, y in zip(xs, ys)]

    def add(ts, us):
        return [t + u for t, u in zip(ts, us)]

    def sub(ts, us):
        return [t - u for t, u in zip(ts, us)]

    ad = [jnp.where(same16, a, 0.0) for a in a_list]
    t = [eye - x for x in ad]
    p = mm(ad, ad)
    t = add(t, mm(t, p))
    p = mm(p, p)
    t = add(t, mm(t, p))
    p = mm(p, p)
    t = add(t, mm(t, p))
    b1 = [jnp.where(same32, a, 0.0) - x for a, x in zip(a_list, ad)]
    t = sub(t, mm(t, mm(b1, t)))
    b2 = [jnp.where(same32, 0.0, a) for a in a_list]
    t = sub(t, mm(t, mm(b2, t)))
    return t


def _gdn_prompt_body(qkv_ref, misc_ref, grow_ref, gz_ref, lanep_ref, rowp_ref, gout_ref,
                     y_ref, s_out_ref, s_ref):
    tb = GDN_TB
    t_idx = pl.program_id(1)

    @pl.when(t_idx == 0)
    def _():
        s_ref[...] = jnp.zeros(s_ref.shape, F32)

    tile = GDN_TILE
    tiles = range(tb // tile)
    rows = [slice(t * tile, (t + 1) * tile) for t in tiles]
    ri = _iota((tile, tile), 0)
    ci = _iota((tile, tile), 1)
    same64 = _block_id(ri, CHUNK) == _block_id(ci, CHUNK)
    lower = same64 & (ci <= ri)
    strict = same64 & (ci < ri)
    same16 = _block_id(ri, 16) == _block_id(ci, 16)
    same32 = _block_id(ri, 32) == _block_id(ci, 32)
    eye = jnp.where(ri == ci, 1.0, 0.0)
    l_incl = jnp.where(lower, 1.0, 0.0).astype(BF16)
    u_incl = jnp.where(same64 & (ri <= ci), 1.0, 0.0).astype(BF16)
    u_strict = jnp.where(same64 & (ci > ri), 1.0, 0.0).astype(BF16)

    beta_l, g_l = _gate_lane_values(misc_ref[0], lanep_ref[...])
    e_b = _expand_matrix(MISC_B)
    e_a = _expand_matrix(MISC_A)
    beta_bc = _dot01_right(beta_l, e_b)
    gc_col = jnp.concatenate([_dot01_left(l_incl, g_l[r]) for r in rows], axis=0)
    gt_col = jnp.concatenate([_dot01_left(u_strict, g_l[r]) for r in rows], axis=0)
    gc_bc = _dot01_right(gc_col, e_a)
    gt_bc = _dot01_right(gt_col, e_a)
    egc = jnp.exp(gc_bc)
    etail = jnp.exp(gt_bc)

    grow = grow_ref[0]
    rowi = _iota(grow.shape, 0)
    g_r = -jnp.exp(rowp_ref[0:SUBLANE, :]) * _softplus(
        jnp.where(rowi >= GDN_HEADS, grow, 0.0) + rowp_ref[SUBLANE:2 * SUBLANE, :])
    gcr = [_dot01_right(g_r[:, r], u_incl) for r in rows]

    gout = gout_ref[...]
    gz = gz_ref[0]
    heads = range(GDN_HEADS)
    sls = [slice(h * GDN_DK, (h + 1) * GDN_DK) for h in heads]
    units = [(t, h) for t in tiles for h in heads]
    qh, kh, kb, vb, dm = {}, {}, {}, {}, {}
    for h in heads:
        q = qkv_ref[0, :, h * GDN_DK:(h + 1) * GDN_DK]
        k = qkv_ref[0, :, GDN_QK + h * GDN_DK:GDN_QK + (h + 1) * GDN_DK]
        v = qkv_ref[0, :, 2 * GDN_QK + h * GDN_DV:2 * GDN_QK + (h + 1) * GDN_DV]
        bb = beta_bc[:, sls[h]]
        for t in tiles:
            qh[t, h] = q[rows[t]]
            kh[t, h] = k[rows[t]]
            kb[t, h] = (k * bb)[rows[t]]
            vb[t, h] = (v * bb)[rows[t]]
            diff = gc_bc[rows[t], sls[h]] - gcr[t][GDN_HEADS + h:GDN_HEADS + h + 1, :]
            dm[t, h] = jnp.where(lower, jnp.exp(jnp.where(lower, diff, 0.0)), 0.0)
    p = {u: _dot_nt(jnp.concatenate([kb[u], qh[u]], axis=0).astype(BF16), kh[u].astype(BF16))
         for u in units}
    a_mat = [jnp.where(strict, p[u][:tile] * dm[u], 0.0) for u in units]
    attn = {u: (p[u][tile:] * dm[u]).astype(BF16) for u in units}
    t_mat = dict(zip(units, _tri_inverse(a_mat, eye, same16, same32)))
    egc_u = {(t, h): egc[rows[t], sls[h]] for t, h in units}
    uw = {u: _dot(t_mat[u].astype(BF16),
                  jnp.concatenate([vb[u], kb[u] * egc_u[u]], axis=1).astype(BF16)) for u in units}
    qd = {u: qh[u] * egc_u[u] for u in units}
    kt = {(t, h): (kh[t, h] * etail[rows[t], sls[h]]).astype(BF16) for t, h in units}
    s = [s_ref[h] for h in heads]
    o_inter = {u: [] for u in units}
    v_new = {u: [] for u in units}
    for t in tiles:
        for c in range(tile // CHUNK):
            rs = slice(c * CHUNK, (c + 1) * CHUNK)
            last = t * tile + (c + 1) * CHUNK - 1
            wq = [_dot(jnp.concatenate([uw[t, h][rs, GDN_DV:], qd[t, h][rs]], axis=0).astype(BF16),
                       s[h].astype(BF16)) for h in heads]
            vn = [uw[t, h][rs, :GDN_DV] - wq[h][:CHUNK] for h in heads]
            egl = [jnp.exp(gc_bc[last:last + 1, sls[h]]) for h in heads]
            s = [s[h] * egl[h] + _dot_tn(kt[t, h][rs], vn[h].astype(BF16)) for h in heads]
            for h in heads:
                o_inter[t, h].append(wq[h][CHUNK:])
                v_new[t, h].append(vn[h])
    o = {u: jnp.concatenate(o_inter[u], axis=0)
         + _dot(attn[u], jnp.concatenate(v_new[u], axis=0).astype(BF16)) for u in units}
    for h in heads:
        s_ref[h] = s[h]
        for t in tiles:
            y_ref[0, rows[t], sls[h]] = _gdn_gate_out(o[t, h], gz[rows[t], sls[h]], gout).astype(BF16)

    @pl.when(t_idx == pl.num_programs(1) - 1)
    def _():
        s_out_ref[0] = s_ref[...]


def _gdn_prompt(qkv, misc, grow, gz, lanep, rowp, gout):
    b, t, _ = qkv.shape
    tb = GDN_TB
    return pl.pallas_call(
        _gdn_prompt_body,
        grid=(b, t // tb),
        in_specs=[
            pl.BlockSpec((1, tb, CONV_DIM), lambda i, j: (i, j, 0)),
            pl.BlockSpec((1, tb, LANE), lambda i, j: (i, j, 0)),
            pl.BlockSpec((1, SUBLANE, tb), lambda i, j: (i, 0, j)),
            pl.BlockSpec((1, tb, GDN_V), lambda i, j: (i, j, 0)),
            _full(lanep.shape), _full(rowp.shape), _full(gout.shape),
        ],
        out_specs=[
            pl.BlockSpec((1, tb, GDN_V), lambda i, j: (i, j, 0)),
            pl.BlockSpec((1, GDN_HEADS, GDN_DK, GDN_DV), lambda i, j: (i, 0, 0, 0)),
        ],
        out_shape=[
            jax.ShapeDtypeStruct((b, t, GDN_V), BF16),
            jax.ShapeDtypeStruct((b, GDN_HEADS, GDN_DK, GDN_DV), F32),
        ],
        scratch_shapes=[pltpu.VMEM((GDN_HEADS, GDN_DK, GDN_DV), F32)],
        compiler_params=_cparams(("parallel", "arbitrary")),
        name="gdn_prompt",
    )(qkv, misc, grow, gz, lanep, rowp, gout)


GDN_DEC_SEQS = 8


def _gdn_decode_body(qkv_ref, cprev_ref, misc_ref, gz_ref, s_in_ref, convw_ref, lanep_ref, gout_ref,
                     y_ref, s_out_ref, o_ref):
    ns = GDN_DEC_SEQS
    w = convw_ref[...]
    conv = qkv_ref[...] * w[CONV_W - 1:CONV_W, :]
    for i in range(CONV_W - 1):
        conv = conv + cprev_ref[i] * w[i:i + 1, :]
    conv = _silu(conv)
    beta_l, g_l = _gate_lane_values(misc_ref[...], lanep_ref[...])
    beta_bc = _dot01_right(beta_l, _expand_matrix(MISC_B))
    eg = jnp.exp(_dot01_right(g_l, _expand_matrix(MISC_A)))
    row0 = jnp.where(_iota((SUBLANE, LANE), 0) == 0, 1.0, 0.0).astype(BF16)
    units = [(h, i) for h in range(GDN_HEADS) for i in range(ns)]
    qn, kn, vv = {}, {}, {}
    for h in range(GDN_HEADS):
        qh = conv[:, h * GDN_DK:(h + 1) * GDN_DK]
        kh = conv[:, GDN_QK + h * GDN_DK:GDN_QK + (h + 1) * GDN_DK]
        qn[h] = qh * lax.rsqrt(jnp.sum(qh * qh, axis=-1, keepdims=True) + EPS) * (GDN_DK ** -0.5)
        kn[h] = kh * lax.rsqrt(jnp.sum(kh * kh, axis=-1, keepdims=True) + EPS)
        vv[h] = conv[:, 2 * GDN_QK + h * GDN_DV:2 * GDN_QK + (h + 1) * GDN_DV]
    sl = lambda h: slice(h * GDN_DK, (h + 1) * GDN_DK)
    k8 = {u: jnp.broadcast_to(kn[u[0]][u[1]:u[1] + 1, :], (SUBLANE, GDN_DK)) for u in units}
    s_dec = {(h, i): s_in_ref[i, h] * eg[i:i + 1, sl(h)] for h, i in units}
    kv = {u: _dot(k8[u].astype(BF16), s_dec[u].astype(BF16))[0:1, :] for u in units}
    kcol = {}
    for u in units:
        k_hi, k_mid, k_lo = _split3(k8[u])
        kcol[u] = _dot_tn(k_hi, row0) + _dot_tn(k_mid, row0) + _dot_tn(k_lo, row0)
    s_new = {}
    for h, i in units:
        delta = (vv[h][i:i + 1, :] - kv[h, i]) * beta_bc[i:i + 1, sl(h)]
        s_new[h, i] = s_dec[h, i] + kcol[h, i] * delta
        s_out_ref[i, h] = s_new[h, i]
    for h, i in units:
        q8 = jnp.broadcast_to(qn[h][i:i + 1, :], (SUBLANE, GDN_DK))
        o_ref[i:i + 1, sl(h)] = _dot(q8.astype(BF16), s_new[h, i].astype(BF16))[0:1, :]
    gout = gout_ref[...]
    gz = gz_ref[...]
    for h in range(GDN_HEADS):
        sl = slice(h * GDN_DV, (h + 1) * GDN_DV)
        y_ref[:, sl] = _gdn_gate_out(o_ref[:, sl], gz[:, sl], gout).astype(BF16)


def _gdn_decode(qkv, cprev, misc, gz, s_in, convw, lanep, gout):
    n = qkv.shape[0]
    ns = GDN_DEC_SEQS
    state_spec = pl.BlockSpec((ns, GDN_HEADS, GDN_DK, GDN_DV), lambda i: (i, 0, 0, 0))
    return pl.pallas_call(
        _gdn_decode_body,
        grid=(n // ns,),
        in_specs=[
            pl.BlockSpec((ns, CONV_DIM), lambda i: (i, 0)),
            pl.BlockSpec((CONV_W - 1, ns, CONV_DIM), lambda i: (0, i, 0)),
            pl.BlockSpec((ns, LANE), lambda i: (i, 0)),
            pl.BlockSpec((ns, GDN_V), lambda i: (i, 0)),
            state_spec,
            _full(convw.shape), _full(lanep.shape), _full(gout.shape),
        ],
        out_specs=[pl.BlockSpec((ns, GDN_V), lambda i: (i, 0)), state_spec],
        out_shape=[
            jax.ShapeDtypeStruct((n, GDN_V), BF16),
            jax.ShapeDtypeStruct(s_in.shape, F32),
        ],
        scratch_shapes=[pltpu.VMEM((ns, GDN_V), F32)],
        compiler_params=_cparams(("parallel",)),
        name="gdn_decode",
    )(qkv, cprev, misc, gz, s_in, convw, lanep, gout)


def _rope_padded(x, cos, sina, sinb):
    return (x * cos + pltpu.roll(x, LANE - QK_ROPE // 2, axis=1) * sina
            + pltpu.roll(x, QK_ROPE // 2, axis=1) * sinb)


def _mla_queries(qa, wqb_ref, g_qa, g_nope, g_rope, cos, sina, sinb):
    q = _dot(_rms(qa, g_qa).astype(BF16), wqb_ref[...])
    out = []
    for h in range(MLA_HEADS):
        qn = _rms(q[:, h * QK_PAD:h * QK_PAD + QK_NOPE], g_nope)
        qp = _rms(q[:, h * QK_PAD + QK_NOPE:(h + 1) * QK_PAD], g_rope, QK_ROPE)
        out.append((qn, _rope_padded(qp, cos, sina, sinb)))
    return out


def _mla_latent(kvc, misc, g_kva, g_krope, cos, sina, sinb):
    c = _rms(kvc, g_kva)
    lane = _iota(misc.shape, 1)
    kp = _rms(jnp.where(lane < QK_ROPE, misc, 0.0), g_krope, QK_ROPE)
    return c, _rope_padded(kp, cos, sina, sinb)


def _mla_prep_body(qa_ref, kvc_ref, misc_ref, tab_ref, wqb_ref, wk_ref, wvt_ref,
                   g_qa_ref, g_nope_ref, g_rope_ref, g_kva_ref, g_knope_ref, g_krope_ref,
                   q_ref, k_ref, vt_ref, c_ref, kpe_ref):
    cos, sina, sinb = tab_ref[0], tab_ref[1], tab_ref[2]
    qs = _mla_queries(qa_ref[...], wqb_ref, g_qa_ref[...], g_nope_ref[...], g_rope_ref[...],
                      cos, sina, sinb)
    c, kp = _mla_latent(kvc_ref[...], misc_ref[...], g_kva_ref[...], g_krope_ref[...], cos, sina, sinb)
    c_ref[...] = c
    kpe_ref[...] = kp[:, :QK_ROPE]
    cb = c.astype(BF16)
    kn = _dot(cb, wk_ref[...])
    vt = _dot_nt(wvt_ref[...], cb)
    g_kn = g_knope_ref[...]
    kp16 = kp.astype(BF16)
    for h in range(MLA_HEADS):
        qn, qp = qs[h]
        q_ref[h, :, 0:QK_NOPE] = (qn * MLA_SCALE).astype(BF16)
        q_ref[h, :, QK_NOPE:QK_PAD] = (qp * MLA_SCALE).astype(BF16)
        k_ref[h, :, 0:QK_NOPE] = _rms(kn[:, h * QK_NOPE:(h + 1) * QK_NOPE], g_kn).astype(BF16)
        k_ref[h, :, QK_NOPE:QK_PAD] = kp16
        vt_ref[h] = vt[h * V_HEAD:(h + 1) * V_HEAD, :].astype(BF16)


def _mla_prep(qa, kvc, misc, tabs, wqb, wk, wvt, gains, tm):
    t = qa.shape[0]
    n_tab = tabs.shape[1] // tm
    row = lambda n: pl.BlockSpec((tm, n), lambda i: (i, 0))
    head = lambda n: pl.BlockSpec((MLA_HEADS, tm, n), lambda i: (0, i, 0))
    return pl.pallas_call(
        _mla_prep_body,
        grid=(t // tm,),
        in_specs=[row(Q_LORA), row(KV_LORA), row(LANE),
                  pl.BlockSpec((3, tm, LANE), lambda i: (0, i % n_tab, 0)),
                  _full(wqb.shape), _full(wk.shape), _full(wvt.shape)]
                 + [_full(g.shape) for g in gains],
        out_specs=[head(QK_PAD), head(QK_PAD),
                   pl.BlockSpec((MLA_HEADS, V_HEAD, tm), lambda i: (0, 0, i)),
                   row(KV_LORA), row(QK_ROPE)],
        out_shape=[
            jax.ShapeDtypeStruct((MLA_HEADS, t, QK_PAD), BF16),
            jax.ShapeDtypeStruct((MLA_HEADS, t, QK_PAD), BF16),
            jax.ShapeDtypeStruct((MLA_HEADS, V_HEAD, t), BF16),
            jax.ShapeDtypeStruct((t, KV_LORA), F32),
            jax.ShapeDtypeStruct((t, QK_ROPE), F32),
        ],
        compiler_params=_cparams(("parallel",)),
        name="mla_prep",
    )(qa, kvc, misc, tabs, wqb, wk, wvt, *gains)


FLASH_T = 512


FLASH_HEADS = 2


def _flash_body(q_ref, k_ref, vt_ref, o_ref):
    tq = FLASH_T
    qi = pl.program_id(2)
    heads = range(FLASH_HEADS)
    q = [q_ref[h] for h in heads]

    def block(j, carry, masked):
        start = pl.multiple_of(j * tq, tq)
        s = [_dot_nt(k_ref[h, pl.ds(start, tq), :], q[h]) for h in heads]
        if masked:
            keep = _iota(s[0].shape, 0) <= _iota(s[0].shape, 1)
            s = [jnp.where(keep, x, -jnp.inf) for x in s]
        m_new = [jnp.maximum(carry[h][0], jnp.max(s[h], axis=0, keepdims=True)) for h in heads]
        p = [jnp.exp(s[h] - m_new[h]) for h in heads]
        pv = [_dot(vt_ref[h, :, pl.ds(start, tq)], p[h].astype(BF16)) for h in heads]
        out = []
        for h in heads:
            m, l, acc = carry[h]
            corr = jnp.exp(m - m_new[h])
            out.append((m_new[h], l * corr + jnp.sum(p[h], axis=0, keepdims=True),
                        acc * corr + pv[h]))
        return tuple(out)

    init = tuple((jnp.full((1, tq), -jnp.inf, F32), jnp.zeros((1, tq), F32),
                  jnp.zeros((V_HEAD, tq), F32)) for _ in heads)
    carry = lax.fori_loop(0, qi, lambda j, c: block(j, c, False), init)
    carry = block(qi, carry, True)
    for h in heads:
        _, l, acc = carry[h]
        o_ref[h * V_HEAD:(h + 1) * V_HEAD, :] = (acc / l).astype(BF16)


def _mla_flash(q, k, vt, b, t):
    tq = FLASH_T
    nq = t // tq
    nh = FLASH_HEADS
    return pl.pallas_call(
        _flash_body,
        grid=(b, MLA_HEADS // nh, nq),
        in_specs=[
            pl.BlockSpec((nh, tq, QK_PAD), lambda i, h, j: (h, i * nq + j, 0)),
            pl.BlockSpec((nh, t, QK_PAD), lambda i, h, j: (h, i, 0)),
            pl.BlockSpec((nh, V_HEAD, t), lambda i, h, j: (h, 0, i)),
        ],
        out_specs=pl.BlockSpec((nh * V_HEAD, tq), lambda i, h, j: (h, i * nq + j)),
        out_shape=jax.ShapeDtypeStruct((MLA_HEADS * V_HEAD, b * t), BF16),
        compiler_params=_cparams(("parallel", "parallel", "arbitrary")),
        name="mla_flash",
    )(q, k, vt)


def _mla_dec_prep_body(qa_ref, kvc_ref, misc_ref, tab_ref, wqb_ref, wk_ref,
                       g_qa_ref, g_nope_ref, g_rope_ref, g_kva_ref, g_knope_ref, g_krope_ref,
                       qabs_ref, qpe_ref, c_ref, kpe_ref):
    cos, sina, sinb = tab_ref[0], tab_ref[1], tab_ref[2]
    qs = _mla_queries(qa_ref[...], wqb_ref, g_qa_ref[...], g_nope_ref[...], g_rope_ref[...],
                      cos, sina, sinb)
    c, kp = _mla_latent(kvc_ref[...], misc_ref[...], g_kva_ref[...], g_krope_ref[...], cos, sina, sinb)
    c_ref[...] = c
    kpe_ref[...] = kp
    g_kn = g_knope_ref[...]
    for h in range(MLA_HEADS):
        qn, qp = qs[h]
        qabs_ref[h] = _dot_nt((qn * g_kn * MLA_SCALE).astype(BF16),
                              wk_ref[:, h * QK_NOPE:(h + 1) * QK_NOPE])
        qpe_ref[h] = qp * MLA_SCALE


def _mla_dec_prep(qa, kvc, misc, tabs, wqb, wk, gains):
    n = qa.shape[0]
    args = (qa, kvc, misc, tabs, wqb, wk) + tuple(gains)
    return pl.pallas_call(
        _mla_dec_prep_body,
        grid=(1,),
        in_specs=[_full(a.shape) for a in args],
        out_specs=[_full((MLA_HEADS, n, KV_LORA)), _full((MLA_HEADS, n, LANE)),
                   _full((n, KV_LORA)), _full((n, LANE))],
        out_shape=[
            jax.ShapeDtypeStruct((MLA_HEADS, n, KV_LORA), F32),
            jax.ShapeDtypeStruct((MLA_HEADS, n, LANE), F32),
            jax.ShapeDtypeStruct((n, KV_LORA), F32),
            jax.ShapeDtypeStruct((n, LANE), F32),
        ],
        compiler_params=_cparams(("arbitrary",)),
        name="mla_dec_prep",
    )(*args)


def _mla_dec_attn_body(pt_ref, qabs_ref, qpe_ref, wkt_ref, ckv_hbm, kpe_hbm,
                       m_out, l_out, acc_out, lhs_ref, cbuf, kbuf, c16, sem_c, sem_k, *, blocks_per_seq):
    npg = PAGES_PER_STEP
    b = pl.program_id(0)
    total_blocks = pl.num_programs(0) * blocks_per_seq
    nk = MLA_HEADS * QK_NOPE
    keys = npg * PAGE_SIZE

    def block_copies(gb, page_of):
        slot = gb % DEC_RING
        cps = []
        for i in range(npg):
            page = page_of(gb * npg + i)
            cps.append(pltpu.make_async_copy(ckv_hbm.at[page], cbuf.at[slot, i], sem_c.at[slot]))
            cps.append(pltpu.make_async_copy(kpe_hbm.at[page], kbuf.at[slot, i], sem_k.at[slot]))
        return cps

    def start_block(gb):
        for cp in block_copies(gb, lambda idx: pt_ref[idx]):
            cp.start()

    def wait_block(gb):
        for cp in block_copies(gb, lambda idx: 0):
            cp.wait()

    @pl.when(b == 0)
    def _():
        lhs_ref[0:nk, :] = wkt_ref[...]
        for gb in range(DEC_RING - 1):
            start_block(gb)

    lhs_ref[nk:nk + BF16_ROWS, :] = qabs_ref[0]
    qpe = qpe_ref[0]

    def scores(j):
        slot = (b * blocks_per_seq + j) % DEC_RING
        c16[j % 2] = cbuf[slot].reshape(keys, KV_LORA).astype(BF16)
        kpt = jnp.concatenate([kbuf[slot, i] for i in range(npg)], axis=1).astype(BF16)
        kt = _dot_nt(lhs_ref[...], c16[j % 2])
        ssq = [jnp.sum(jnp.square(kt[h * QK_NOPE:(h + 1) * QK_NOPE]), axis=0, keepdims=True)
               for h in range(MLA_HEADS)]
        ssq = jnp.concatenate(ssq + [jnp.ones((SUBLANE - MLA_HEADS, keys), F32)], axis=0)
        pe = _dot(qpe, kpt)
        return kt[nk:nk + SUBLANE] * lax.rsqrt(ssq * (1.0 / QK_NOPE) + EPS) + pe[0:SUBLANE]

    def accumulate(j, s, carry):
        m_old, l_old, acc_old = carry
        m_new = jnp.maximum(m_old, jnp.max(s, axis=-1, keepdims=True))
        p = jnp.exp(s - m_new)
        corr = jnp.exp(m_old - m_new)
        l_new = l_old * corr + jnp.sum(p, axis=-1, keepdims=True)
        acc_new = acc_old * corr + _dot(p.astype(BF16), c16[j % 2])
        return m_new, l_new, acc_new

    def fetch_and_score(j, after):
        gb = b * blocks_per_seq + j
        nxt = gb + (DEC_RING - 1)

        @pl.when(nxt < total_blocks)
        def _():
            start_block(nxt)

        wait_block(gb)
        s_cur = scores(j)
        return s_cur, after()

    def body(j, carry):
        s_prev, state = carry
        return fetch_and_score(j, lambda: accumulate(j - 1, s_prev, state))

    init = (jnp.full((SUBLANE, 1), -jnp.inf, F32), jnp.zeros((SUBLANE, 1), F32),
            jnp.zeros((SUBLANE, KV_LORA), F32))
    s_last, state = lax.fori_loop(1, blocks_per_seq, body, fetch_and_score(0, lambda: init))
    m_fin, l_fin, acc_fin = accumulate(blocks_per_seq - 1, s_last, state)
    m_out[0] = jnp.broadcast_to(m_fin, m_out.shape[1:])
    l_out[0] = jnp.broadcast_to(l_fin, l_out.shape[1:])
    acc_out[0] = acc_fin


def _mla_dec_attn(page_table, ckv_pool, kpe_pool, qabs, qpe, wkt):
    n, n_pages = page_table.shape
    npg = PAGES_PER_STEP
    nk = MLA_HEADS * QK_NOPE

    assert n_pages % npg == 0 and n * (n_pages // npg) >= DEC_RING
    seq = lambda shape: pl.BlockSpec((1,) + shape, lambda b, pt: (b, 0, 0))
    grid_spec = pltpu.PrefetchScalarGridSpec(
        num_scalar_prefetch=1,
        grid=(n,),
        in_specs=[seq((BF16_ROWS, KV_LORA)), seq((BF16_ROWS, QK_ROPE)),
                  pl.BlockSpec(wkt.shape, lambda b, pt: (0, 0)),
                  pl.BlockSpec(memory_space=pl.ANY), pl.BlockSpec(memory_space=pl.ANY)],
        out_specs=[seq((SUBLANE, LANE)), seq((SUBLANE, LANE)), seq((SUBLANE, KV_LORA))],
        scratch_shapes=[
            pltpu.VMEM((nk + BF16_ROWS, KV_LORA), BF16),
            pltpu.VMEM((DEC_RING, npg, PAGE_SIZE, KV_LORA), F32),
            pltpu.VMEM((DEC_RING, npg, QK_ROPE, PAGE_SIZE), F32),
            pltpu.VMEM((2, npg * PAGE_SIZE, KV_LORA), BF16),
            pltpu.SemaphoreType.DMA((DEC_RING,)),
            pltpu.SemaphoreType.DMA((DEC_RING,)),
        ],
    )
    return pl.pallas_call(
        functools.partial(_mla_dec_attn_body, blocks_per_seq=n_pages // npg),
        grid_spec=grid_spec,
        out_shape=[
            jax.ShapeDtypeStruct((n, SUBLANE, LANE), F32),
            jax.ShapeDtypeStruct((n, SUBLANE, LANE), F32),
            jax.ShapeDtypeStruct((n, SUBLANE, KV_LORA), F32),
        ],
        compiler_params=_cparams(("arbitrary",)),
        name="mla_dec_attn",
    )(page_table.reshape(-1), qabs, qpe, wkt, ckv_pool, kpe_pool)


def _mla_dec_finish_body(m_ref, l_ref, acc_ref, qabs_ref, qpe_ref, c_ref, kpe_ref, wk_ref, wv_ref,
                         y_ref):
    c = c_ref[...]
    kp = kpe_ref[...]
    cb = c.astype(BF16)
    kn = _dot(cb, wk_ref[...])
    for h in range(MLA_HEADS):
        knh = kn[:, h * QK_NOPE:(h + 1) * QK_NOPE]
        r = lax.rsqrt(jnp.sum(knh * knh, axis=-1, keepdims=True) * (1.0 / QK_NOPE) + EPS)
        s_new = (r * jnp.sum(qabs_ref[h] * c, axis=-1, keepdims=True)
                 + jnp.sum(qpe_ref[h] * kp, axis=-1, keepdims=True))
        m_old = m_ref[h][:, 0:1]
        m_new = jnp.maximum(m_old, s_new)
        p = jnp.exp(s_new - m_new)
        corr = jnp.exp(m_old - m_new)
        l = l_ref[h][:, 0:1] * corr + p
        lat = (acc_ref[h] * corr + p * c) / l
        y_ref[:, h * V_HEAD:(h + 1) * V_HEAD] = _dot(
            lat.astype(BF16), wv_ref[:, h * V_HEAD:(h + 1) * V_HEAD]).astype(BF16)


def _mla_dec_finish(m, l, acc, qabs, qpe, c, kpe, wk, wv):
    n = c.shape[0]
    args = (m, l, acc, qabs, qpe, c, kpe, wk, wv)
    return pl.pallas_call(
        _mla_dec_finish_body,
        grid=(1,),
        in_specs=[_full(a.shape) for a in args],
        out_specs=_full((n, MLA_HEADS * V_HEAD)),
        out_shape=jax.ShapeDtypeStruct((n, MLA_HEADS * V_HEAD), BF16),
        compiler_params=_cparams(("arbitrary",)),
        name="mla_dec_finish",
    )(*args)


def _mix_q_body(x_ref, yg_ref, ym_ref, wmix_ref, g_mem_ref, wq_ref, g_qn_ref, x1_ref, q_ref, *,
                mla_transposed):
    mla_dot = _dot_tn if mla_transposed else _dot
    x1 = (x_ref[...] + _dot(yg_ref[...], wmix_ref[0:GDN_V, :])
          + mla_dot(ym_ref[...], wmix_ref[GDN_V:, :]))
    x1_ref[...] = x1
    q = _dot(_rms(x1, g_mem_ref[...]).astype(BF16), wq_ref[...])
    g_qn = g_qn_ref[...]
    for h in range(MEM_HEADS):
        sl = slice(h * MEM_HD, (h + 1) * MEM_HD)
        q_ref[:, sl] = (_rms(q[:, sl], g_qn) * (MEM_HD ** -0.5)).astype(BF16)


def _mix_q(x, yg, ym, wmix, g_mem, wq, g_qn, tm, mla_transposed=False):
    t = x.shape[0]
    row = lambda n: pl.BlockSpec((tm, n), lambda i: (i, 0))
    mla_width = MLA_HEADS * V_HEAD
    ym_spec = pl.BlockSpec((mla_width, tm), lambda i: (0, i)) if mla_transposed else row(mla_width)
    return pl.pallas_call(
        functools.partial(_mix_q_body, mla_transposed=mla_transposed),
        grid=(t // tm,),
        in_specs=[row(D_MODEL), row(GDN_V), ym_spec, _full(wmix.shape),
                  _full(g_mem.shape), _full(wq.shape), _full(g_qn.shape)],
        out_specs=[row(D_MODEL), row(MEM_DIM)],
        out_shape=[jax.ShapeDtypeStruct((t, D_MODEL), F32), jax.ShapeDtypeStruct((t, MEM_DIM), BF16)],
        compiler_params=_cparams(("parallel",)),
        name="mix_q",
    )(x, yg, ym, wmix, g_mem, wq, g_qn)


def _mem_attn_body(q_ref, k_ref, v_ref, o_ref):
    q = q_ref[0]
    heads = range(MEM_HEADS)
    sls = [slice(h * MEM_HD, (h + 1) * MEM_HD) for h in heads]
    s = [_dot_nt(q[:, sls[h]], k_ref[0, :, sls[h]].astype(BF16)) for h in heads]
    p = [jnp.exp(s[h] - jnp.max(s[h], axis=-1, keepdims=True)) for h in heads]
    o = [_dot(p[h].astype(BF16), v_ref[0, :, sls[h]].astype(BF16)) for h in heads]
    for h in heads:
        o_ref[0, :, sls[h]] = (o[h] / jnp.sum(p[h], axis=-1, keepdims=True)).astype(BF16)


def _mem_attn(q, k, v, tq):
    nb, t, _ = q.shape
    m = k.shape[1]
    return pl.pallas_call(
        _mem_attn_body,
        grid=(nb, t // tq),
        in_specs=[pl.BlockSpec((1, tq, MEM_DIM), lambda i, j: (i, j, 0)),
                  pl.BlockSpec((1, m, MEM_DIM), lambda i, j: (i, 0, 0)),
                  pl.BlockSpec((1, m, MEM_DIM), lambda i, j: (i, 0, 0))],
        out_specs=pl.BlockSpec((1, tq, MEM_DIM), lambda i, j: (i, j, 0)),
        out_shape=jax.ShapeDtypeStruct((nb, t, MEM_DIM), BF16),
        compiler_params=_cparams(("parallel", "arbitrary")),
        name="mem_attn",
    )(q, k, v)


def _mem_attn_dec_body(q_ref, k_ref, v_ref, o_ref):
    q = q_ref[...]
    s = jnp.sum(k_ref[0] * q, axis=-1, keepdims=True)
    p = jnp.exp(s - jnp.max(s, axis=0, keepdims=True))
    o_ref[...] = jnp.sum(p * v_ref[0], axis=0, keepdims=True) / jnp.sum(p, axis=0, keepdims=True)


def _mem_attn_dec(q, k, v):
    n, m, nh, hd = k.shape
    kv_spec = pl.BlockSpec((1, m, nh, hd), lambda i: (i, 0, 0, 0))
    return pl.pallas_call(
        _mem_attn_dec_body,
        grid=(n,),
        in_specs=[pl.BlockSpec((1, nh, hd), lambda i: (i, 0, 0)), kv_spec, kv_spec],
        out_specs=pl.BlockSpec((1, nh, hd), lambda i: (i, 0, 0)),
        out_shape=jax.ShapeDtypeStruct((n, nh, hd), F32),
        compiler_params=_cparams(("parallel",)),
        name="mem_attn_dec",
    )(q, k, v)


def _out_ffn_body(x1_ref, o_ref, wo_ref, g_ffn_ref, wg_ref, wu_ref, wd_ref, y_ref):
    x2 = x1_ref[...] + _dot(o_ref[...], wo_ref[...])
    h = _rms(x2, g_ffn_ref[...]).astype(BF16)
    act = (_silu(_dot(h, wg_ref[...])) * _dot(h, wu_ref[...])).astype(BF16)
    y_ref[...] = x2 + _dot(act, wd_ref[...])


def _out_ffn(x1, o, wo, g_ffn, wg, wu, wd, tm):
    t = x1.shape[0]
    row = lambda n: pl.BlockSpec((tm, n), lambda i: (i, 0))
    const = lambda a: pl.BlockSpec(a.shape, lambda i: (0,) * a.ndim, pipeline_mode=pl.Buffered(1))
    return pl.pallas_call(
        _out_ffn_body,
        grid=(t // tm,),
        in_specs=[row(D_MODEL), row(MEM_DIM), const(wo), const(g_ffn), const(wg), const(wu), const(wd)],
        out_specs=row(D_MODEL),
        out_shape=jax.ShapeDtypeStruct((t, D_MODEL), F32),
        compiler_params=_cparams(("parallel",)),
        name="out_ffn",
    )(x1, o, wo, g_ffn, wg, wu, wd)


def _mem_kv_body(mem_ref, wk_ref, wv_ref, g_ref, k_ref, v_ref):
    mb = mem_ref[...].astype(BF16)
    k = _dot(mb, wk_ref[...])
    g = g_ref[...]
    for h in range(MEM_HEADS):
        sl = slice(h * MEM_HD, (h + 1) * MEM_HD)
        k_ref[:, sl] = _rms(k[:, sl], g)
    v_ref[...] = _dot(mb, wv_ref[...])


def _mem_kv(mem, wk, wv, g, tm):
    t = mem.shape[0]
    row = lambda n: pl.BlockSpec((tm, n), lambda i: (i, 0))
    return pl.pallas_call(
        _mem_kv_body,
        grid=(t // tm,),
        in_specs=[row(D_MODEL), _full(wk.shape), _full(wv.shape), _full(g.shape)],
        out_specs=[row(MEM_DIM), row(MEM_DIM)],
        out_shape=[jax.ShapeDtypeStruct((t, MEM_DIM), F32)] * 2,
        compiler_params=_cparams(("parallel",)),
        name="mem_kv",
    )(mem, wk, wv, g)


def _row(v):
    return v.reshape(1, -1).astype(F32)


def _pad_lanes(v, width=LANE):
    return jnp.pad(v, ((0, 0), (0, width - v.shape[1])))


def _rope_tables(pos, rows):
    half = QK_ROPE // 2
    inv = ROPE_THETA ** (-jnp.arange(half, dtype=F32) / half)
    ang = pos.astype(F32)[:, None] * inv[None, :]
    cos, sin = jnp.cos(ang), jnp.sin(ang)
    zero = jnp.zeros_like(cos)
    tabs = jnp.stack([
        _pad_lanes(jnp.concatenate([cos, cos], axis=1)),
        _pad_lanes(jnp.concatenate([-sin, zero], axis=1)),
        _pad_lanes(jnp.concatenate([zero, sin], axis=1)),
    ])
    return jnp.broadcast_to(tabs, (3, rows, LANE)) if tabs.shape[1] == 1 else tabs


def kernel(x_prompt, x_sample, cache_mla_ckv, cache_mla_kpe, cache_mem_k, cache_mem_v, state_gdn_S, state_gdn_conv, page_table, mem_prompt, norm_mix_g, w_in, gdn_conv_w, gdn_A_log, gdn_dt_bias, gdn_out_norm_g, mla_q_a_norm_g, mla_w_q_b, mla_kv_a_norm_g, mla_w_kv_b, mla_qn_nope_g, mla_qn_rope_g, mla_kn_nope_g, mla_kn_rope_g, w_mix_out, norm_mem_g, mem_wq, mem_wk, mem_wv, mem_wo, mem_qn_g, mem_kn_g, norm_ffn_g, ffn_w_gate, ffn_w_up, ffn_w_down):
    depth = w_in.shape[0]
    assert depth == 1, "single-layer trunk"
    bsz, seq, _ = x_prompt.shape
    nseq, dseq, _ = x_sample.shape
    assert dseq == 1, "one new token per decode sequence"
    past_len = page_table.shape[1] * PAGE_SIZE
    n_tok = bsz * seq

    w = w_in[0]
    o_gz = CONV_DIM
    o_b = o_gz + GDN_V
    o_a = o_b + GDN_HEADS
    o_qa = o_a + GDN_HEADS
    o_c = o_qa + Q_LORA
    o_kpe = o_c + KV_LORA
    misc_w = _pad_lanes(jnp.concatenate([w[:, o_kpe:o_kpe + QK_ROPE], w[:, o_b:o_qa]], axis=1))
    w_in_p = jnp.concatenate([w[:, :o_b], w[:, o_qa:o_kpe], misc_w], axis=1).astype(BF16)

    wqb = mla_w_q_b[0].reshape(Q_LORA, MLA_HEADS, QK_NOPE + QK_ROPE)
    wqb = jnp.pad(wqb, ((0, 0), (0, 0), (0, QK_PAD - QK_NOPE - QK_ROPE)))
    wqb = wqb.reshape(Q_LORA, MLA_HEADS * QK_PAD).astype(BF16)
    wkvb = mla_w_kv_b[0].reshape(KV_LORA, MLA_HEADS, QK_NOPE + V_HEAD)
    wk = wkvb[:, :, :QK_NOPE].reshape(KV_LORA, MLA_HEADS * QK_NOPE).astype(BF16)
    wv = wkvb[:, :, QK_NOPE:].reshape(KV_LORA, MLA_HEADS * V_HEAD).astype(BF16)
    wkt = wk.T

    lanep = jnp.zeros((2, LANE), F32)
    lanep = lanep.at[0, MISC_A:MISC_A + GDN_HEADS].set(gdn_A_log[0])
    lanep = lanep.at[1, MISC_A:MISC_A + GDN_HEADS].set(gdn_dt_bias[0])
    rowp = jnp.zeros((2 * SUBLANE,), F32)
    rowp = rowp.at[GDN_HEADS:2 * GDN_HEADS].set(gdn_A_log[0])
    rowp = rowp.at[SUBLANE + GDN_HEADS:SUBLANE + 2 * GDN_HEADS].set(gdn_dt_bias[0])
    rowp = jnp.broadcast_to(rowp[:, None], (2 * SUBLANE, GDN_TB))

    g_mix = _row(norm_mix_g[0])
    g_out = _row(gdn_out_norm_g[0])
    mla_gains = (_row(mla_q_a_norm_g[0]), _row(mla_qn_nope_g[0]), _pad_lanes(_row(mla_qn_rope_g[0])),
                 _row(mla_kv_a_norm_g[0]), _row(mla_kn_nope_g[0]), _pad_lanes(_row(mla_kn_rope_g[0])))
    wmix = w_mix_out[0].astype(BF16)
    wq_mem = mem_wq[0].astype(BF16)
    wk_mem = mem_wk[0].astype(BF16)
    wv_mem = mem_wv[0].astype(BF16)
    wo_mem = mem_wo[0].astype(BF16)
    wg = ffn_w_gate[0].astype(BF16)
    wu = ffn_w_up[0].astype(BF16)
    wd = ffn_w_down[0].astype(BF16)
    g_mem = _row(norm_mem_g[0])
    g_ffn = _row(norm_ffn_g[0])
    g_qn = _row(mem_qn_g[0])
    g_kn = _row(mem_kn_g[0])
    conv_w = gdn_conv_w[0]

    xp = x_prompt.reshape(n_tok, D_MODEL)
    tiles_per_seq = seq // TOKEN_TILE
    qkv_act, qkv_tail, gz, qa, kvc, misc = _in_proj_conv(xp, g_mix, w_in_p, conv_w, TOKEN_TILE,
                                                         tiles_per_seq)
    grow = jnp.swapaxes(misc.reshape(bsz, seq, LANE)[:, :, MISC_B:MISC_B + SUBLANE], 1, 2)
    y_gdn, p_s = _gdn_prompt(qkv_act.reshape(bsz, seq, CONV_DIM), misc.reshape(bsz, seq, LANE), grow,
                             gz.reshape(bsz, seq, GDN_V), lanep, rowp, g_out)
    p_conv = qkv_tail.reshape(bsz, tiles_per_seq, SUBLANE, CONV_DIM)[:, -1, SUBLANE - (CONV_W - 1):, :]

    tabs_p = _rope_tables(jnp.arange(seq), seq)
    q_full, k_full, vt_full, p_c, p_kpe = _mla_prep(qa, kvc, misc, tabs_p, wqb, wk, wv.T, mla_gains,
                                                    TOKEN_TILE)
    y_mla_t = _mla_flash(q_full, k_full, vt_full, bsz, seq)

    mem_k, mem_v = _mem_kv(mem_prompt.reshape(-1, D_MODEL), wk_mem, wv_mem, g_kn, TOKEN_TILE)
    n_mem = mem_prompt.shape[1]
    x1, q_mem = _mix_q(xp, y_gdn.reshape(n_tok, GDN_V), y_mla_t, wmix, g_mem, wq_mem, g_qn, MIX_TILE,
                       mla_transposed=True)
    o_mem = _mem_attn(q_mem.reshape(bsz, seq, MEM_DIM), mem_k.reshape(bsz, n_mem, MEM_DIM),
                      mem_v.reshape(bsz, n_mem, MEM_DIM), TOKEN_TILE)
    y_prompt = _out_ffn(x1, o_mem.reshape(n_tok, MEM_DIM), wo_mem, g_ffn, wg, wu, wd, TOKEN_TILE)

    xs = x_sample.reshape(nseq, D_MODEL)
    qkv_s, gz_s, qa_s, kvc_s, misc_s = _in_proj(xs, g_mix, w_in_p, nseq)
    conv_prev = state_gdn_conv[0]
    y_gdn_s, s_new = _gdn_decode(qkv_s, jnp.swapaxes(conv_prev, 0, 1), misc_s, gz_s, state_gdn_S[0],
                                 conv_w, lanep, g_out)
    s_conv = jnp.concatenate([conv_prev[:, 1:, :], qkv_s[:, None, :]], axis=1)

    tabs_s = _rope_tables(jnp.full((1,), past_len), nseq)
    qabs, qpe, c_new, kpe_new = _mla_dec_prep(qa_s, kvc_s, misc_s, tabs_s, wqb, wk, mla_gains)
    pad_rows = lambda a: jnp.pad(jnp.swapaxes(a, 0, 1), ((0, 0), (0, BF16_ROWS - MLA_HEADS), (0, 0)))
    m_run, l_run, acc_run = _mla_dec_attn(
        page_table, cache_mla_ckv[0], jnp.swapaxes(cache_mla_kpe[0], 1, 2),
        pad_rows(qabs).astype(BF16), pad_rows(qpe[:, :, :QK_ROPE]).astype(BF16), wkt)
    heads_first = lambda a: jnp.swapaxes(a[:, :MLA_HEADS], 0, 1)
    y_mla_s = _mla_dec_finish(heads_first(m_run), heads_first(l_run), heads_first(acc_run),
                              qabs, qpe, c_new, kpe_new, wk, wv)

    x1_s, q_mem_s = _mix_q(xs, y_gdn_s, y_mla_s, wmix, g_mem, wq_mem, g_qn, nseq)
    o_mem_s = _mem_attn_dec(q_mem_s.astype(F32).reshape(nseq, MEM_HEADS, MEM_HD),
                            cache_mem_k[0], cache_mem_v[0])
    y_sample = _out_ffn(x1_s, o_mem_s.reshape(nseq, MEM_DIM).astype(BF16),
                        wo_mem, g_ffn, wg, wu, wd, nseq)

    return (
        y_prompt.reshape(bsz, seq, D_MODEL),
        y_sample.reshape(nseq, 1, D_MODEL),
        p_c.reshape(1, bsz, seq, KV_LORA),
        p_kpe.reshape(1, bsz, seq, QK_ROPE),
        mem_k.reshape(1, bsz, n_mem, MEM_HEADS, MEM_HD),
        mem_v.reshape(1, bsz, n_mem, MEM_HEADS, MEM_HD),
        p_s[None],
        p_conv[None],
        c_new.reshape(1, nseq, 1, KV_LORA),
        kpe_new[:, :QK_ROPE].reshape(1, nseq, 1, QK_ROPE),
        s_new[None],
        s_conv[None],
    )
```

```python
import functools

import jax
import jax.numpy as jnp
from jax import lax
from jax.experimental import pallas as pl
from jax.experimental.pallas import tpu as pltpu

F32 = jnp.float32
BF16 = jnp.bfloat16

D_MODEL = 1024
GDN_HEADS = 4
GDN_DK = 128
GDN_DV = 128
CONV_W = 4
CHUNK = 64
MLA_HEADS = 4
Q_LORA = 384
KV_LORA = 256
QK_NOPE = 128
QK_ROPE = 64
V_HEAD = 128
ROPE_THETA = 10000.0
PAGE_SIZE = 128
MEM_HEADS = 4
MEM_HD = 128
EPS = 1e-6
GDN_QK = GDN_HEADS * GDN_DK
GDN_V = GDN_HEADS * GDN_DV
CONV_DIM = 2 * GDN_QK + GDN_V
MLA_SCALE = (QK_NOPE + QK_ROPE) ** -0.5
MEM_DIM = MEM_HEADS * MEM_HD

LANE = 128
SUBLANE = 8
BF16_ROWS = 16
QK_PAD = 2 * LANE
MISC_B = QK_ROPE
MISC_A = QK_ROPE + GDN_HEADS
VMEM_LIMIT = 56 * 1024 * 1024

TOKEN_TILE = 256
MIX_TILE = 512
GDN_TILE = 2 * CHUNK
GDN_TB = 2 * GDN_TILE
PAGES_PER_STEP = 32
DEC_RING = 4


def _cparams(sem):
    return pltpu.CompilerParams(dimension_semantics=sem, vmem_limit_bytes=VMEM_LIMIT)


def _full(shape):
    n = len(shape)
    return pl.BlockSpec(shape, lambda *_: (0,) * n)


def _dot(a, b):
    return jnp.dot(a, b, preferred_element_type=F32)


def _dot_nt(a, b):
    return lax.dot_general(a, b, (((1,), (1,)), ((), ())), preferred_element_type=F32)


def _dot_tn(a, b):
    return lax.dot_general(a, b, (((0,), (0,)), ((), ())), preferred_element_type=F32)


def _split3(x):
    hi = x.astype(BF16)
    r = x - hi.astype(F32)
    mid = r.astype(BF16)
    lo = (r - mid.astype(F32)).astype(BF16)
    return hi, mid, lo


def _dot01_left(m01, x):
    hi, mid, lo = _split3(x)
    return _dot(m01, hi) + _dot(m01, mid) + _dot(m01, lo)


def _dot01_right(x, m01):
    hi, mid, lo = _split3(x)
    return _dot(hi, m01) + _dot(mid, m01) + _dot(lo, m01)


def _rms(x, g, n=None):
    n = x.shape[-1] if n is None else n
    ms = jnp.sum(x * x, axis=-1, keepdims=True) * (1.0 / n)
    return x * lax.rsqrt(ms + EPS) * g


def _silu(x):
    return x * jax.nn.sigmoid(x)


def _softplus(x):
    return jnp.maximum(x, 0.0) + jnp.log1p(jnp.exp(-jnp.abs(x)))


def _iota(shape, dim):
    return lax.broadcasted_iota(jnp.int32, shape, dim)


def _block_id(i, size):
    assert size & (size - 1) == 0
    return lax.shift_right_logical(i, size.bit_length() - 1)


def _in_proj_body(x_ref, g_ref, w_ref, qkv_ref, gz_ref, qa_ref, kvc_ref, misc_ref):
    h = _rms(x_ref[...], g_ref[...]).astype(BF16)
    off = 0
    for ref in (qkv_ref, gz_ref, qa_ref, kvc_ref, misc_ref):
        n = ref.shape[-1]
        ref[...] = _dot(h, w_ref[:, off:off + n])
        off += n


def _in_proj(x, g, w, tm):
    t = x.shape[0]
    widths = (CONV_DIM, GDN_V, Q_LORA, KV_LORA, LANE)
    return pl.pallas_call(
        _in_proj_body,
        grid=(t // tm,),
        in_specs=[pl.BlockSpec((tm, D_MODEL), lambda i: (i, 0)), _full(g.shape), _full(w.shape)],
        out_specs=[pl.BlockSpec((tm, n), lambda i: (i, 0)) for n in widths],
        out_shape=[jax.ShapeDtypeStruct((t, n), F32) for n in widths],
        compiler_params=_cparams(("parallel",)),
        name="in_proj",
    )(x, g, w)


def _qk_l2norm(x, scale):
    return x * (lax.rsqrt(jnp.sum(x * x, axis=-1, keepdims=True) + EPS) * scale)


def _in_proj_conv_body(x_ref, g_ref, w_ref, convw_ref, qkv_ref, tail_ref, gz_ref, qa_ref, kvc_ref,
                       misc_ref, xe_ref, *, tiles_per_seq):
    tm = x_ref.shape[0]
    @pl.when(pl.program_id(0) % tiles_per_seq == 0)
    def _():
        xe_ref[0:SUBLANE, :] = jnp.zeros((SUBLANE, CONV_DIM), F32)

    h = _rms(x_ref[...], g_ref[...]).astype(BF16)
    w = convw_ref[...]
    pair = 2 * GDN_DK
    blocks = [slice(blk * pair, (blk + 1) * pair) for blk in range(CONV_DIM // pair)]
    for sl in blocks:
        xe_ref[SUBLANE:SUBLANE + tm, sl] = _dot(h, w_ref[:, sl])
    off = CONV_DIM
    for ref in (gz_ref, qa_ref, kvc_ref, misc_ref):
        n = ref.shape[-1]
        ref[...] = _dot(h, w_ref[:, off:off + n])
        off += n
    for blk, sl in enumerate(blocks):
        conv = xe_ref[SUBLANE:SUBLANE + tm, sl] * w[CONV_W - 1:CONV_W, sl]
        for i in range(CONV_W - 1):
            lo = SUBLANE - (CONV_W - 1) + i
            conv = conv + xe_ref[lo:lo + tm, sl] * w[i:i + 1, sl]
        act = _silu(conv)
        for half in range(2):
            head_blk = 2 * blk + half
            hs = slice(half * GDN_DK, (half + 1) * GDN_DK)
            out_sl = slice(head_blk * GDN_DK, (head_blk + 1) * GDN_DK)
            if head_blk < GDN_HEADS:
                qkv_ref[:, out_sl] = _qk_l2norm(act[:, hs], GDN_DK ** -0.5)
            elif head_blk < 2 * GDN_HEADS:
                qkv_ref[:, out_sl] = _qk_l2norm(act[:, hs], 1.0)
            else:
                qkv_ref[:, out_sl] = act[:, hs]
    tail = xe_ref[tm:SUBLANE + tm, :]
    tail_ref[...] = tail
    xe_ref[0:SUBLANE, :] = tail


def _in_proj_conv(x, g, w, convw, tm, tiles_per_seq):
    t = x.shape[0]
    widths = (CONV_DIM, CONV_DIM, GDN_V, Q_LORA, KV_LORA, LANE)
    rows = (tm, SUBLANE, tm, tm, tm, tm)
    return pl.pallas_call(
        functools.partial(_in_proj_conv_body, tiles_per_seq=tiles_per_seq),
        grid=(t // tm,),
        in_specs=[pl.BlockSpec((tm, D_MODEL), lambda i: (i, 0)), _full(g.shape), _full(w.shape),
                  _full(convw.shape)],
        out_specs=[pl.BlockSpec((r, n), lambda i: (i, 0)) for r, n in zip(rows, widths)],
        out_shape=[jax.ShapeDtypeStruct((t // tm * r, n), F32) for r, n in zip(rows, widths)],
        scratch_shapes=[pltpu.VMEM((SUBLANE + tm, CONV_DIM), F32)],
        compiler_params=_cparams(("arbitrary",)),
        name="in_proj_conv",
    )(x, g, w, convw)


def _gate_lane_values(misc, lanep):
    lane = _iota(misc.shape, 1)
    is_a = (lane >= MISC_A) & (lane < MISC_A + GDN_HEADS)
    beta_l = jax.nn.sigmoid(misc)
    g_l = -jnp.exp(lanep[0:1, :]) * _softplus(jnp.where(is_a, misc, 0.0) + lanep[1:2, :])
    return beta_l, g_l


def _expand_matrix(first_lane):
    r = _iota((LANE, GDN_V), 0)
    c = _iota((LANE, GDN_V), 1)
    return jnp.where(r == first_lane + _block_id(c, GDN_DV), 1.0, 0.0).astype(BF16)


def _gdn_gate_out(o, z, gout):
    return _rms(o, gout) * _silu(z)


def _tri_inverse(a_list, eye, same16, same32):
    def mm(xs, ys):
        return [_dot(x.astype(BF16), y.astype(BF16)) for x, y in zip(xs, ys)]

    def add(ts, us):
        return [t + u for t, u in zip(ts, us)]

    def sub(ts, us):
        return [t - u for t, u in zip(ts, us)]

    ad = [jnp.where(same16, a, 0.0) for a in a_list]
    t = [eye - x for x in ad]
    p = mm(ad, ad)
    t = add(t, mm(t, p))
    p = mm(p, p)
    t = add(t, mm(t, p))
    p = mm(p, p)
    t = add(t, mm(t, p))
    b1 = [jnp.where(same32, a, 0.0) - x for a, x in zip(a_list, ad)]
    t = sub(t, mm(t, mm(b1, t)))
    b2 = [jnp.where(same32, 0.0, a) for a in a_list]
    t = sub(t, mm(t, mm(b2, t)))
    return t


def _gdn_prompt_body(qkv_ref, misc_ref, grow_ref, gz_ref, lanep_ref, rowp_ref, gout_ref,
                     y_ref, s_out_ref, s_ref):
    tb = GDN_TB
    t_idx = pl.program_id(1)

    @pl.when(t_idx == 0)
    def _():
        s_ref[...] = jnp.zeros(s_ref.shape, F32)

    tile = GDN_TILE
    tiles = range(tb // tile)
    rows = [slice(t * tile, (t + 1) * tile) for t in tiles]
    ri = _iota((tile, tile), 0)
    ci = _iota((tile, tile), 1)
    same64 = _block_id(ri, CHUNK) == _block_id(ci, CHUNK)
    lower = same64 & (ci <= ri)
    strict = same64 & (ci < ri)
    same16 = _block_id(ri, 16) == _block_id(ci, 16)
    same32 = _block_id(ri, 32) == _block_id(ci, 32)
    eye = jnp.where(ri == ci, 1.0, 0.0)
    l_incl = jnp.where(lower, 1.0, 0.0).astype(BF16)
    u_incl = jnp.where(same64 & (ri <= ci), 1.0, 0.0).astype(BF16)
    u_strict = jnp.where(same64 & (ci > ri), 1.0, 0.0).astype(BF16)

    beta_l, g_l = _gate_lane_values(misc_ref[0], lanep_ref[...])
    e_b = _expand_matrix(MISC_B)
    e_a = _expand_matrix(MISC_A)
    beta_bc = _dot01_right(beta_l, e_b)
    gc_col = jnp.concatenate([_dot01_left(l_incl, g_l[r]) for r in rows], axis=0)
    gt_col = jnp.concatenate([_dot01_left(u_strict, g_l[r]) for r in rows], axis=0)
    gc_bc = _dot01_right(gc_col, e_a)
    gt_bc = _dot01_right(gt_col, e_a)
    egc = jnp.exp(gc_bc)
    etail = jnp.exp(gt_bc)

    grow = grow_ref[0]
    rowi = _iota(grow.shape, 0)
    g_r = -jnp.exp(rowp_ref[0:SUBLANE, :]) * _softplus(
        jnp.where(rowi >= GDN_HEADS, grow, 0.0) + rowp_ref[SUBLANE:2 * SUBLANE, :])
    gcr = [_dot01_right(g_r[:, r], u_incl) for r in rows]

    gout = gout_ref[...]
    gz = gz_ref[0]
    heads = range(GDN_HEADS)
    sls = [slice(h * GDN_DK, (h + 1) * GDN_DK) for h in heads]
    units = [(t, h) for t in tiles for h in heads]
    qh, kh, kb, vb, dm = {}, {}, {}, {}, {}
    for h in heads:
        q = qkv_ref[0, :, h * GDN_DK:(h + 1) * GDN_DK]
        k = qkv_ref[0, :, GDN_QK + h * GDN_DK:GDN_QK + (h + 1) * GDN_DK]
        v = qkv_ref[0, :, 2 * GDN_QK + h * GDN_DV:2 * GDN_QK + (h + 1) * GDN_DV]
        bb = beta_bc[:, sls[h]]
        for t in tiles:
            qh[t, h] = q[rows[t]]
            kh[t, h] = k[rows[t]]
            kb[t, h] = (k * bb)[rows[t]]
            vb[t, h] = (v * bb)[rows[t]]
            diff = gc_bc[rows[t], sls[h]] - gcr[t][GDN_HEADS + h:GDN_HEADS + h + 1, :]
            dm[t, h] = jnp.where(lower, jnp.exp(jnp.where(lower, diff, 0.0)), 0.0)
    p = {u: _dot_nt(jnp.concatenate([kb[u], qh[u]], axis=0).astype(BF16), kh[u].astype(BF16))
         for u in units}
    a_mat = [jnp.where(strict, p[u][:tile] * dm[u], 0.0) for u in units]
    attn = {u: (p[u][tile:] * dm[u]).astype(BF16) for u in units}
    t_mat = dict(zip(units, _tri_inverse(a_mat, eye, same16, same32)))
    egc_u = {(t, h): egc[rows[t], sls[h]] for t, h in units}
    uw = {u: _dot(t_mat[u].astype(BF16),
                  jnp.concatenate([vb[u], kb[u] * egc_u[u]], axis=1).astype(BF16)) for u in units}
    qd = {u: qh[u] * egc_u[u] for u in units}
    kt = {(t, h): (kh[t, h] * etail[rows[t], sls[h]]).astype(BF16) for t, h in units}
    s = [s_ref[h] for h in heads]
    o_inter = {u: [] for u in units}
    v_new = {u: [] for u in units}
    for t in tiles:
        for c in range(tile // CHUNK):
            rs = slice(c * CHUNK, (c + 1) * CHUNK)
            last = t * tile + (c + 1) * CHUNK - 1
            wq = [_dot(jnp.concatenate([uw[t, h][rs, GDN_DV:], qd[t, h][rs]], axis=0).astype(BF16),
                       s[h].astype(BF16)) for h in heads]
            vn = [uw[t, h][rs, :GDN_DV] - wq[h][:CHUNK] for h in heads]
            egl = [jnp.exp(gc_bc[last:last + 1, sls[h]]) for h in heads]
            s = [s[h] * egl[h] + _dot_tn(kt[t, h][rs], vn[h].astype(BF16)) for h in heads]
            for h in heads:
                o_inter[t, h].append(wq[h][CHUNK:])
                v_new[t, h].append(vn[h])
    o = {u: jnp.concatenate(o_inter[u], axis=0)
         + _dot(attn[u], jnp.concatenate(v_new[u], axis=0).astype(BF16)) for u in units}
    for h in heads:
        s_ref[h] = s[h]
        for t in tiles:
            y_ref[0, rows[t], sls[h]] = _gdn_gate_out(o[t, h], gz[rows[t], sls[h]], gout).astype(BF16)

    @pl.when(t_idx == pl.num_programs(1) - 1)
    def _():
        s_out_ref[0] = s_ref[...]


def _gdn_prompt(qkv, misc, grow, gz, lanep, rowp, gout):
    b, t, _ = qkv.shape
    tb = GDN_TB
    return pl.pallas_call(
        _gdn_prompt_body,
        grid=(b, t // tb),
        in_specs=[
            pl.BlockSpec((1, tb, CONV_DIM), lambda i, j: (i, j, 0)),
            pl.BlockSpec((1, tb, LANE), lambda i, j: (i, j, 0)),
            pl.BlockSpec((1, SUBLANE, tb), lambda i, j: (i, 0, j)),
            pl.BlockSpec((1, tb, GDN_V), lambda i, j: (i, j, 0)),
            _full(lanep.shape), _full(rowp.shape), _full(gout.shape),
        ],
        out_specs=[
            pl.BlockSpec((1, tb, GDN_V), lambda i, j: (i, j, 0)),
            pl.BlockSpec((1, GDN_HEADS, GDN_DK, GDN_DV), lambda i, j: (i, 0, 0, 0)),
        ],
        out_shape=[
            jax.ShapeDtypeStruct((b, t, GDN_V), BF16),
            jax.ShapeDtypeStruct((b, GDN_HEADS, GDN_DK, GDN_DV), F32),
        ],
        scratch_shapes=[pltpu.VMEM((GDN_HEADS, GDN_DK, GDN_DV), F32)],
        compiler_params=_cparams(("parallel", "arbitrary")),
        name="gdn_prompt",
    )(qkv, misc, grow, gz, lanep, rowp, gout)


GDN_DEC_SEQS = 8


def _gdn_decode_body(qkv_ref, cprev_ref, misc_ref, gz_ref, s_in_ref, convw_ref, lanep_ref, gout_ref,
                     y_ref, s_out_ref, o_ref):
    ns = GDN_DEC_SEQS
    w = convw_ref[...]
    conv = qkv_ref[...] * w[CONV_W - 1:CONV_W, :]
    for i in range(CONV_W - 1):
        conv = conv + cprev_ref[i] * w[i:i + 1, :]
    conv = _silu(conv)
    beta_l, g_l = _gate_lane_values(misc_ref[...], lanep_ref[...])
    beta_bc = _dot01_right(beta_l, _expand_matrix(MISC_B))
    eg = jnp.exp(_dot01_right(g_l, _expand_matrix(MISC_A)))
    row0 = jnp.where(_iota((SUBLANE, LANE), 0) == 0, 1.0, 0.0).astype(BF16)
    units = [(h, i) for h in range(GDN_HEADS) for i in range(ns)]
    qn, kn, vv = {}, {}, {}
    for h in range(GDN_HEADS):
        qh = conv[:, h * GDN_DK:(h + 1) * GDN_DK]
        kh = conv[:, GDN_QK + h * GDN_DK:GDN_QK + (h + 1) * GDN_DK]
        qn[h] = qh * lax.rsqrt(jnp.sum(qh * qh, axis=-1, keepdims=True) + EPS) * (GDN_DK ** -0.5)
        kn[h] = kh * lax.rsqrt(jnp.sum(kh * kh, axis=-1, keepdims=True) + EPS)
        vv[h] = conv[:, 2 * GDN_QK + h * GDN_DV:2 * GDN_QK + (h + 1) * GDN_DV]
    sl = lambda h: slice(h * GDN_DK, (h + 1) * GDN_DK)
    k8 = {u: jnp.broadcast_to(kn[u[0]][u[1]:u[1] + 1, :], (SUBLANE, GDN_DK)) for u in units}
    s_dec = {(h, i): s_in_ref[i, h] * eg[i:i + 1, sl(h)] for h, i in units}
    kv = {u: _dot(k8[u].astype(BF16), s_dec[u].astype(BF16))[0:1, :] for u in units}
    kcol = {}
    for u in units:
        k_hi, k_mid, k_lo = _split3(k8[u])
        kcol[u] = _dot_tn(k_hi, row0) + _dot_tn(k_mid, row0) + _dot_tn(k_lo, row0)
    s_new = {}
    for h, i in units:
        delta = (vv[h][i:i + 1, :] - kv[h, i]) * beta_bc[i:i + 1, sl(h)]
        s_new[h, i] = s_dec[h, i] + kcol[h, i] * delta
        s_out_ref[i, h] = s_new[h, i]
    for h, i in units:
        q8 = jnp.broadcast_to(qn[h][i:i + 1, :], (SUBLANE, GDN_DK))
        o_ref[i:i + 1, sl(h)] = _dot(q8.astype(BF16), s_new[h, i].astype(BF16))[0:1, :]
    gout = gout_ref[...]
    gz = gz_ref[...]
    for h in range(GDN_HEADS):
        sl = slice(h * GDN_DV, (h + 1) * GDN_DV)
        y_ref[:, sl] = _gdn_gate_out(o_ref[:, sl], gz[:, sl], gout).astype(BF16)


def _gdn_decode(qkv, cprev, misc, gz, s_in, convw, lanep, gout):
    n = qkv.shape[0]
    ns = GDN_DEC_SEQS
    state_spec = pl.BlockSpec((ns, GDN_HEADS, GDN_DK, GDN_DV), lambda i: (i, 0, 0, 0))
    return pl.pallas_call(
        _gdn_decode_body,
        grid=(n // ns,),
        in_specs=[
            pl.BlockSpec((ns, CONV_DIM), lambda i: (i, 0)),
            pl.BlockSpec((CONV_W - 1, ns, CONV_DIM), lambda i: (0, i, 0)),
            pl.BlockSpec((ns, LANE), lambda i: (i, 0)),
            pl.BlockSpec((ns, GDN_V), lambda i: (i, 0)),
            state_spec,
            _full(convw.shape), _full(lanep.shape), _full(gout.shape),
        ],
        out_specs=[pl.BlockSpec((ns, GDN_V), lambda i: (i, 0)), state_spec],
        out_shape=[
            jax.ShapeDtypeStruct((n, GDN_V), BF16),
            jax.ShapeDtypeStruct(s_in.shape, F32),
        ],
        scratch_shapes=[pltpu.VMEM((ns, GDN_V), F32)],
        compiler_params=_cparams(("parallel",)),
        name="gdn_decode",
    )(qkv, cprev, misc, gz, s_in, convw, lanep, gout)


def _rope_padded(x, cos, sina, sinb):
    return (x * cos + pltpu.roll(x, LANE - QK_ROPE // 2, axis=1) * sina
            + pltpu.roll(x, QK_ROPE // 2, axis=1) * sinb)


def _mla_queries(qa, wqb_ref, g_qa, g_nope, g_rope, cos, sina, sinb):
    q = _dot(_rms(qa, g_qa).astype(BF16), wqb_ref[...])
    out = []
    for h in range(MLA_HEADS):
        qn = _rms(q[:, h * QK_PAD:h * QK_PAD + QK_NOPE], g_nope)
        qp = _rms(q[:, h * QK_PAD + QK_NOPE:(h + 1) * QK_PAD], g_rope, QK_ROPE)
        out.append((qn, _rope_padded(qp, cos, sina, sinb)))
    return out


def _mla_latent(kvc, misc, g_kva, g_krope, cos, sina, sinb):
    c = _rms(kvc, g_kva)
    lane = _iota(misc.shape, 1)
    kp = _rms(jnp.where(lane < QK_ROPE, misc, 0.0), g_krope, QK_ROPE)
    return c, _rope_padded(kp, cos, sina, sinb)


def _mla_prep_body(qa_ref, kvc_ref, misc_ref, tab_ref, wqb_ref, wk_ref, wvt_ref,
                   g_qa_ref, g_nope_ref, g_rope_ref, g_kva_ref, g_knope_ref, g_krope_ref,
                   q_ref, k_ref, vt_ref, c_ref, kpe_ref):
    cos, sina, sinb = tab_ref[0], tab_ref[1], tab_ref[2]
    qs = _mla_queries(qa_ref[...], wqb_ref, g_qa_ref[...], g_nope_ref[...], g_rope_ref[...],
                      cos, sina, sinb)
    c, kp = _mla_latent(kvc_ref[...], misc_ref[...], g_kva_ref[...], g_krope_ref[...], cos, sina, sinb)
    c_ref[...] = c
    kpe_ref[...] = kp[:, :QK_ROPE]
    cb = c.astype(BF16)
    kn = _dot(cb, wk_ref[...])
    vt = _dot_nt(wvt_ref[...], cb)
    g_kn = g_knope_ref[...]
    kp16 = kp.astype(BF16)
    for h in range(MLA_HEADS):
        qn, qp = qs[h]
        q_ref[h, :, 0:QK_NOPE] = (qn * MLA_SCALE).astype(BF16)
        q_ref[h, :, QK_NOPE:QK_PAD] = (qp * MLA_SCALE).astype(BF16)
        k_ref[h, :, 0:QK_NOPE] = _rms(kn[:, h * QK_NOPE:(h + 1) * QK_NOPE], g_kn).astype(BF16)
        k_ref[h, :, QK_NOPE:QK_PAD] = kp16
        vt_ref[h] = vt[h * V_HEAD:(h + 1) * V_HEAD, :].astype(BF16)


def _mla_prep(qa, kvc, misc, tabs, wqb, wk, wvt, gains, tm):
    t = qa.shape[0]
    n_tab = tabs.shape[1] // tm
    row = lambda n: pl.BlockSpec((tm, n), lambda i: (i, 0))
    head = lambda n: pl.BlockSpec((MLA_HEADS, tm, n), lambda i: (0, i, 0))
    return pl.pallas_call(
        _mla_prep_body,
        grid=(t // tm,),
        in_specs=[row(Q_LORA), row(KV_LORA), row(LANE),
                  pl.BlockSpec((3, tm, LANE), lambda i: (0, i % n_tab, 0)),
                  _full(wqb.shape), _full(wk.shape), _full(wvt.shape)]
                 + [_full(g.shape) for g in gains],
        out_specs=[head(QK_PAD), head(QK_PAD),
                   pl.BlockSpec((MLA_HEADS, V_HEAD, tm), lambda i: (0, 0, i)),
                   row(KV_LORA), row(QK_ROPE)],
        out_shape=[
            jax.ShapeDtypeStruct((MLA_HEADS, t, QK_PAD), BF16),
            jax.ShapeDtypeStruct((MLA_HEADS, t, QK_PAD), BF16),
            jax.ShapeDtypeStruct((MLA_HEADS, V_HEAD, t), BF16),
            jax.ShapeDtypeStruct((t, KV_LORA), F32),
            jax.ShapeDtypeStruct((t, QK_ROPE), F32),
        ],
        compiler_params=_cparams(("parallel",)),
        name="mla_prep",
    )(qa, kvc, misc, tabs, wqb, wk, wvt, *gains)


FLASH_T = 512


FLASH_HEADS = 2


def _flash_body(q_ref, k_ref, vt_ref, o_ref):
    tq = FLASH_T
    qi = pl.program_id(2)
    heads = range(FLASH_HEADS)
    q = [q_ref[h] for h in heads]

    def block(j, carry, masked):
        start = pl.multiple_of(j * tq, tq)
        s = [_dot_nt(k_ref[h, pl.ds(start, tq), :], q[h]) for h in heads]
        if masked:
            keep = _iota(s[0].shape, 0) <= _iota(s[0].shape, 1)
            s = [jnp.where(keep, x, -jnp.inf) for x in s]
        m_new = [jnp.maximum(carry[h][0], jnp.max(s[h], axis=0, keepdims=True)) for h in heads]
        p = [jnp.exp(s[h] - m_new[h]) for h in heads]
        pv = [_dot(vt_ref[h, :, pl.ds(start, tq)], p[h].astype(BF16)) for h in heads]
        out = []
        for h in heads:
            m, l, acc = carry[h]
            corr = jnp.exp(m - m_new[h])
            out.append((m_new[h], l * corr + jnp.sum(p[h], axis=0, keepdims=True),
                        acc * corr + pv[h]))
        return tuple(out)

    init = tuple((jnp.full((1, tq), -jnp.inf, F32), jnp.zeros((1, tq), F32),
                  jnp.zeros((V_HEAD, tq), F32)) for _ in heads)
    carry = lax.fori_loop(0, qi, lambda j, c: block(j, c, False), init)
    carry = block(qi, carry, True)
    for h in heads:
        _, l, acc = carry[h]
        o_ref[h * V_HEAD:(h + 1) * V_HEAD, :] = (acc / l).astype(BF16)


def _mla_flash(q, k, vt, b, t):
    tq = FLASH_T
    nq = t // tq
    nh = FLASH_HEADS
    return pl.pallas_call(
        _flash_body,
        grid=(b, MLA_HEADS // nh, nq),
        in_specs=[
            pl.BlockSpec((nh, tq, QK_PAD), lambda i, h, j: (h, i * nq + j, 0)),
            pl.BlockSpec((nh, t, QK_PAD), lambda i, h, j: (h, i, 0)),
            pl.BlockSpec((nh, V_HEAD, t), lambda i, h, j: (h, 0, i)),
        ],
        out_specs=pl.BlockSpec((nh * V_HEAD, tq), lambda i, h, j: (h, i * nq + j)),
        out_shape=jax.ShapeDtypeStruct((MLA_HEADS * V_HEAD, b * t), BF16),
        compiler_params=_cparams(("parallel", "parallel", "arbitrary")),
        name="mla_flash",
    )(q, k, vt)


def _mla_dec_prep_body(qa_ref, kvc_ref, misc_ref, tab_ref, wqb_ref, wk_ref,
                       g_qa_ref, g_nope_ref, g_rope_ref, g_kva_ref, g_knope_ref, g_krope_ref,
                       qabs_ref, qpe_ref, c_ref, kpe_ref):
    cos, sina, sinb = tab_ref[0], tab_ref[1], tab_ref[2]
    qs = _mla_queries(qa_ref[...], wqb_ref, g_qa_ref[...], g_nope_ref[...], g_rope_ref[...],
                      cos, sina, sinb)
    c, kp = _mla_latent(kvc_ref[...], misc_ref[...], g_kva_ref[...], g_krope_ref[...], cos, sina, sinb)
    c_ref[...] = c
    kpe_ref[...] = kp
    g_kn = g_knope_ref[...]
    for h in range(MLA_HEADS):
        qn, qp = qs[h]
        qabs_ref[h] = _dot_nt((qn * g_kn * MLA_SCALE).astype(BF16),
                              wk_ref[:, h * QK_NOPE:(h + 1) * QK_NOPE])
        qpe_ref[h] = qp * MLA_SCALE


def _mla_dec_prep(qa, kvc, misc, tabs, wqb, wk, gains):
    n = qa.shape[0]
    args = (qa, kvc, misc, tabs, wqb, wk) + tuple(gains)
    return pl.pallas_call(
        _mla_dec_prep_body,
        grid=(1,),
        in_specs=[_full(a.shape) for a in args],
        out_specs=[_full((MLA_HEADS, n, KV_LORA)), _full((MLA_HEADS, n, LANE)),
                   _full((n, KV_LORA)), _full((n, LANE))],
        out_shape=[
            jax.ShapeDtypeStruct((MLA_HEADS, n, KV_LORA), F32),
            jax.ShapeDtypeStruct((MLA_HEADS, n, LANE), F32),
            jax.ShapeDtypeStruct((n, KV_LORA), F32),
            jax.ShapeDtypeStruct((n, LANE), F32),
        ],
        compiler_params=_cparams(("arbitrary",)),
        name="mla_dec_prep",
    )(*args)


def _mla_dec_attn_body(pt_ref, qabs_ref, qpe_ref, wkt_ref, ckv_hbm, kpe_hbm,
                       m_out, l_out, acc_out, lhs_ref, cbuf, kbuf, c16, sem_c, sem_k, *, blocks_per_seq):
    npg = PAGES_PER_STEP
    n_seq = qabs_ref.shape[0]
    total_blocks = n_seq * blocks_per_seq
    nk = MLA_HEADS * QK_NOPE
    keys = npg * PAGE_SIZE

    def block_copies(gb, page_of):
        slot = gb % DEC_RING
        cps = []
        for i in range(npg):
            page = page_of(gb * npg + i)
            cps.append(pltpu.make_async_copy(ckv_hbm.at[page], cbuf.at[slot, i], sem_c.at[slot]))
            cps.append(pltpu.make_async_copy(kpe_hbm.at[page], kbuf.at[slot, i], sem_k.at[slot]))
        return cps

    def start_block(gb):
        for cp in block_copies(gb, lambda idx: pt_ref[idx]):
            cp.start()

    def wait_block(gb):
        for cp in block_copies(gb, lambda idx: 0):
            cp.wait()

    lhs_ref[0:nk, :] = wkt_ref[...]
    for gb in range(DEC_RING - 1):
        start_block(gb)

    def scores(gb):
        seq = gb // blocks_per_seq

        @pl.when(gb % blocks_per_seq == 0)
        def _():
            lhs_ref[nk:nk + BF16_ROWS, :] = qabs_ref[seq]

        slot = gb % DEC_RING
        c16[gb % 2] = cbuf[slot].reshape(keys, KV_LORA).astype(BF16)
        kpt = jnp.concatenate([kbuf[slot, i] for i in range(npg)], axis=1).astype(BF16)
        kt = _dot_nt(lhs_ref[...], c16[gb % 2])
        ssq = [jnp.sum(jnp.square(kt[h * QK_NOPE:(h + 1) * QK_NOPE]), axis=0, keepdims=True)
               for h in range(MLA_HEADS)]
        ssq = jnp.concatenate(ssq + [jnp.ones((SUBLANE - MLA_HEADS, keys), F32)], axis=0)
        pe = _dot(qpe_ref[seq], kpt)
        return kt[nk:nk + SUBLANE] * lax.rsqrt(ssq * (1.0 / QK_NOPE) + EPS) + pe[0:SUBLANE]

    def accumulate(gb, s, carry):
        m_old, l_old, acc_old = carry
        m_new = jnp.maximum(m_old, jnp.max(s, axis=-1, keepdims=True))
        p = jnp.exp(s - m_new)
        corr = jnp.exp(m_old - m_new)
        l_new = l_old * corr + jnp.sum(p, axis=-1, keepdims=True)
        acc_new = acc_old * corr + _dot(p.astype(BF16), c16[gb % 2])
        return m_new, l_new, acc_new

    def finish(seq, state):
        m_fin, l_fin, acc_fin = state
        m_out[seq] = jnp.broadcast_to(m_fin, m_out.shape[1:])
        l_out[seq] = jnp.broadcast_to(l_fin, l_out.shape[1:])
        acc_out[seq] = acc_fin

    def fetch_and_score(gb, after):
        nxt = gb + (DEC_RING - 1)

        @pl.when(nxt < total_blocks)
        def _():
            start_block(nxt)

        wait_block(gb)
        s_cur = scores(gb)
        return s_cur, after()

    init = (jnp.full((SUBLANE, 1), -jnp.inf, F32), jnp.zeros((SUBLANE, 1), F32),
            jnp.zeros((SUBLANE, KV_LORA), F32))

    def body(gb, carry):
        s_prev, state = carry
        s_cur, state = fetch_and_score(gb, lambda: accumulate(gb - 1, s_prev, state))
        closes_seq = gb % blocks_per_seq == 0

        @pl.when(closes_seq)
        def _():
            finish(gb // blocks_per_seq - 1, state)

        return s_cur, tuple(jnp.where(closes_seq, i, x) for i, x in zip(init, state))

    s_last, state = lax.fori_loop(1, total_blocks, body, fetch_and_score(0, lambda: init))
    finish(n_seq - 1, accumulate(total_blocks - 1, s_last, state))


def _mla_dec_attn(page_table, ckv_pool, kpe_pool, qabs, qpe, wkt):
    n, n_pages = page_table.shape
    npg = PAGES_PER_STEP
    nk = MLA_HEADS * QK_NOPE

    assert n_pages % npg == 0 and n * (n_pages // npg) >= DEC_RING
    whole = lambda shape: pl.BlockSpec(shape, lambda i, pt: (0,) * len(shape))
    grid_spec = pltpu.PrefetchScalarGridSpec(
        num_scalar_prefetch=1,
        grid=(1,),
        in_specs=[whole(qabs.shape), whole(qpe.shape), whole(wkt.shape),
                  pl.BlockSpec(memory_space=pl.ANY), pl.BlockSpec(memory_space=pl.ANY)],
        out_specs=[whole((n, SUBLANE, LANE)), whole((n, SUBLANE, LANE)), whole((n, SUBLANE, KV_LORA))],
        scratch_shapes=[
            pltpu.VMEM((nk + BF16_ROWS, KV_LORA), BF16),
            pltpu.VMEM((DEC_RING, npg, PAGE_SIZE, KV_LORA), F32),
            pltpu.VMEM((DEC_RING, npg, QK_ROPE, PAGE_SIZE), F32),
            pltpu.VMEM((2, npg * PAGE_SIZE, KV_LORA), BF16),
            pltpu.SemaphoreType.DMA((DEC_RING,)),
            pltpu.SemaphoreType.DMA((DEC_RING,)),
        ],
    )
    return pl.pallas_call(
        functools.partial(_mla_dec_attn_body, blocks_per_seq=n_pages // npg),
        grid_spec=grid_spec,
        out_shape=[
            jax.ShapeDtypeStruct((n, SUBLANE, LANE), F32),
            jax.ShapeDtypeStruct((n, SUBLANE, LANE), F32),
            jax.ShapeDtypeStruct((n, SUBLANE, KV_LORA), F32),
        ],
        compiler_params=_cparams(("arbitrary",)),
        name="mla_dec_attn",
    )(page_table.reshape(-1), qabs, qpe, wkt, ckv_pool, kpe_pool)


def _mla_dec_finish_body(m_ref, l_ref, acc_ref, qabs_ref, qpe_ref, c_ref, kpe_ref, wk_ref, wv_ref,
                         y_ref):
    c = c_ref[...]
    kp = kpe_ref[...]
    cb = c.astype(BF16)
    kn = _dot(cb, wk_ref[...])
    for h in range(MLA_HEADS):
        knh = kn[:, h * QK_NOPE:(h + 1) * QK_NOPE]
        r = lax.rsqrt(jnp.sum(knh * knh, axis=-1, keepdims=True) * (1.0 / QK_NOPE) + EPS)
        s_new = (r * jnp.sum(qabs_ref[h] * c, axis=-1, keepdims=True)
                 + jnp.sum(qpe_ref[h] * kp, axis=-1, keepdims=True))
        m_old = m_ref[h][:, 0:1]
        m_new = jnp.maximum(m_old, s_new)
        p = jnp.exp(s_new - m_new)
        corr = jnp.exp(m_old - m_new)
        l = l_ref[h][:, 0:1] * corr + p
        lat = (acc_ref[h] * corr + p * c) / l
        y_ref[:, h * V_HEAD:(h + 1) * V_HEAD] = _dot(
            lat.astype(BF16), wv_ref[:, h * V_HEAD:(h + 1) * V_HEAD]).astype(BF16)


def _mla_dec_finish(m, l, acc, qabs, qpe, c, kpe, wk, wv):
    n = c.shape[0]
    args = (m, l, acc, qabs, qpe, c, kpe, wk, wv)
    return pl.pallas_call(
        _mla_dec_finish_body,
        grid=(1,),
        in_specs=[_full(a.shape) for a in args],
        out_specs=_full((n, MLA_HEADS * V_HEAD)),
        out_shape=jax.ShapeDtypeStruct((n, MLA_HEADS * V_HEAD), BF16),
        compiler_params=_cparams(("arbitrary",)),
        name="mla_dec_finish",
    )(*args)


def _mix_q_body(x_ref, yg_ref, ym_ref, wmix_ref, g_mem_ref, wq_ref, g_qn_ref, x1_ref, q_ref, *,
                mla_transposed):
    mla_dot = _dot_tn if mla_transposed else _dot
    x1 = (x_ref[...] + _dot(yg_ref[...], wmix_ref[0:GDN_V, :])
          + mla_dot(ym_ref[...], wmix_ref[GDN_V:, :]))
    x1_ref[...] = x1
    q = _dot(_rms(x1, g_mem_ref[...]).astype(BF16), wq_ref[...])
    g_qn = g_qn_ref[...]
    for h in range(MEM_HEADS):
        sl = slice(h * MEM_HD, (h + 1) * MEM_HD)
        q_ref[:, sl] = (_rms(q[:, sl], g_qn) * (MEM_HD ** -0.5)).astype(BF16)


def _mix_q(x, yg, ym, wmix, g_mem, wq, g_qn, tm, mla_transposed=False):
    t = x.shape[0]
    row = lambda n: pl.BlockSpec((tm, n), lambda i: (i, 0))
    mla_width = MLA_HEADS * V_HEAD
    ym_spec = pl.BlockSpec((mla_width, tm), lambda i: (0, i)) if mla_transposed else row(mla_width)
    return pl.pallas_call(
        functools.partial(_mix_q_body, mla_transposed=mla_transposed),
        grid=(t // tm,),
        in_specs=[row(D_MODEL), row(GDN_V), ym_spec, _full(wmix.shape),
                  _full(g_mem.shape), _full(wq.shape), _full(g_qn.shape)],
        out_specs=[row(D_MODEL), row(MEM_DIM)],
        out_shape=[jax.ShapeDtypeStruct((t, D_MODEL), F32), jax.ShapeDtypeStruct((t, MEM_DIM), BF16)],
        compiler_params=_cparams(("parallel",)),
        name="mix_q",
    )(x, yg, ym, wmix, g_mem, wq, g_qn)


def _mem_attn_body(q_ref, k_ref, v_ref, o_ref):
    q = q_ref[0]
    heads = range(MEM_HEADS)
    sls = [slice(h * MEM_HD, (h + 1) * MEM_HD) for h in heads]
    s = [_dot_nt(q[:, sls[h]], k_ref[0, :, sls[h]].astype(BF16)) for h in heads]
    p = [jnp.exp(s[h] - jnp.max(s[h], axis=-1, keepdims=True)) for h in heads]
    o = [_dot(p[h].astype(BF16), v_ref[0, :, sls[h]].astype(BF16)) for h in heads]
    for h in heads:
        o_ref[0, :, sls[h]] = (o[h] / jnp.sum(p[h], axis=-1, keepdims=True)).astype(BF16)


def _mem_attn(q, k, v, tq):
    nb, t, _ = q.shape
    m = k.shape[1]
    return pl.pallas_call(
        _mem_attn_body,
        grid=(nb, t // tq),
        in_specs=[pl.BlockSpec((1, tq, MEM_DIM), lambda i, j: (i, j, 0)),
                  pl.BlockSpec((1, m, MEM_DIM), lambda i, j: (i, 0, 0)),
                  pl.BlockSpec((1, m, MEM_DIM), lambda i, j: (i, 0, 0))],
        out_specs=pl.BlockSpec((1, tq, MEM_DIM), lambda i, j: (i, j, 0)),
        out_shape=jax.ShapeDtypeStruct((nb, t, MEM_DIM), BF16),
        compiler_params=_cparams(("parallel", "arbitrary")),
        name="mem_attn",
    )(q, k, v)


def _mem_attn_dec_body(q_ref, k_ref, v_ref, o_ref):
    for i in range(q_ref.shape[0]):
        q = q_ref[i:i + 1]
        s = jnp.sum(k_ref[i] * q, axis=-1, keepdims=True)
        p = jnp.exp(s - jnp.max(s, axis=0, keepdims=True))
        o_ref[i:i + 1] = (jnp.sum(p * v_ref[i], axis=0, keepdims=True)
                          / jnp.sum(p, axis=0, keepdims=True))


MEM_DEC_SEQS = 4


def _mem_attn_dec(q, k, v):
    n, m, nh, hd = k.shape
    ns = MEM_DEC_SEQS
    kv_spec = pl.BlockSpec((ns, m, nh, hd), lambda i: (i, 0, 0, 0))
    return pl.pallas_call(
        _mem_attn_dec_body,
        grid=(n // ns,),
        in_specs=[pl.BlockSpec((ns, nh, hd), lambda i: (i, 0, 0)), kv_spec, kv_spec],
        out_specs=pl.BlockSpec((ns, nh, hd), lambda i: (i, 0, 0)),
        out_shape=jax.ShapeDtypeStruct((n, nh, hd), F32),
        compiler_params=_cparams(("parallel",)),
        name="mem_attn_dec",
    )(q, k, v)


def _out_ffn_body(x1_ref, o_ref, wo_ref, g_ffn_ref, wg_ref, wu_ref, wd_ref, y_ref):
    x2 = x1_ref[...] + _dot(o_ref[...], wo_ref[...])
    h = _rms(x2, g_ffn_ref[...]).astype(BF16)
    act = (_silu(_dot(h, wg_ref[...])) * _dot(h, wu_ref[...])).astype(BF16)
    y_ref[...] = x2 + _dot(act, wd_ref[...])


def _out_ffn(x1, o, wo, g_ffn, wg, wu, wd, tm):
    t = x1.shape[0]
    row = lambda n: pl.BlockSpec((tm, n), lambda i: (i, 0))
    const = lambda a: pl.BlockSpec(a.shape, lambda i: (0,) * a.ndim, pipeline_mode=pl.Buffered(1))
    return pl.pallas_call(
        _out_ffn_body,
        grid=(t // tm,),
        in_specs=[row(D_MODEL), row(MEM_DIM), const(wo), const(g_ffn), const(wg), const(wu), const(wd)],
        out_specs=row(D_MODEL),
        out_shape=jax.ShapeDtypeStruct((t, D_MODEL), F32),
        compiler_params=_cparams(("parallel",)),
        name="out_ffn",
    )(x1, o, wo, g_ffn, wg, wu, wd)


def _mem_kv_body(mem_ref, wk_ref, wv_ref, g_ref, k_ref, v_ref):
    mb = mem_ref[...].astype(BF16)
    k = _dot(mb, wk_ref[...])
    g = g_ref[...]
    for h in range(MEM_HEADS):
        sl = slice(h * MEM_HD, (h + 1) * MEM_HD)
        k_ref[:, sl] = _rms(k[:, sl], g)
    v_ref[...] = _dot(mb, wv_ref[...])


def _mem_kv(mem, wk, wv, g, tm):
    t = mem.shape[0]
    row = lambda n: pl.BlockSpec((tm, n), lambda i: (i, 0))
    return pl.pallas_call(
        _mem_kv_body,
        grid=(t // tm,),
        in_specs=[row(D_MODEL), _full(wk.shape), _full(wv.shape), _full(g.shape)],
        out_specs=[row(MEM_DIM), row(MEM_DIM)],
        out_shape=[jax.ShapeDtypeStruct((t, MEM_DIM), F32)] * 2,
        compiler_params=_cparams(("parallel",)),
        name="mem_kv",
    )(mem, wk, wv, g)


def _row(v):
    return v.reshape(1, -1).astype(F32)


def _pad_lanes(v, width=LANE):
    return jnp.pad(v, ((0, 0), (0, width - v.shape[1])))


def _rope_tables(pos, rows):
    half = QK_ROPE // 2
    inv = ROPE_THETA ** (-jnp.arange(half, dtype=F32) / half)
    ang = pos.astype(F32)[:, None] * inv[None, :]
    cos, sin = jnp.cos(ang), jnp.sin(ang)
    zero = jnp.zeros_like(cos)
    tabs = jnp.stack([
        _pad_lanes(jnp.concatenate([cos, cos], axis=1)),
        _pad_lanes(jnp.concatenate([-sin, zero], axis=1)),
        _pad_lanes(jnp.concatenate([zero, sin], axis=1)),
    ])
    return jnp.broadcast_to(tabs, (3, rows, LANE)) if tabs.shape[1] == 1 else tabs


def kernel(x_prompt, x_sample, cache_mla_ckv, cache_mla_kpe, cache_mem_k, cache_mem_v, state_gdn_S, state_gdn_conv, page_table, mem_prompt, norm_mix_g, w_in, gdn_conv_w, gdn_A_log, gdn_dt_bias, gdn_out_norm_g, mla_q_a_norm_g, mla_w_q_b, mla_kv_a_norm_g, mla_w_kv_b, mla_qn_nope_g, mla_qn_rope_g, mla_kn_nope_g, mla_kn_rope_g, w_mix_out, norm_mem_g, mem_wq, mem_wk, mem_wv, mem_wo, mem_qn_g, mem_kn_g, norm_ffn_g, ffn_w_gate, ffn_w_up, ffn_w_down):
    depth = w_in.shape[0]
    assert depth == 1, "single-layer trunk"
    bsz, seq, _ = x_prompt.shape
    nseq, dseq, _ = x_sample.shape
    assert dseq == 1, "one new token per decode sequence"
    past_len = page_table.shape[1] * PAGE_SIZE
    n_tok = bsz * seq

    w = w_in[0]
    o_gz = CONV_DIM
    o_b = o_gz + GDN_V
    o_a = o_b + GDN_HEADS
    o_qa = o_a + GDN_HEADS
    o_c = o_qa + Q_LORA
    o_kpe = o_c + KV_LORA
    misc_w = _pad_lanes(jnp.concatenate([w[:, o_kpe:o_kpe + QK_ROPE], w[:, o_b:o_qa]], axis=1))
    w_in_p = jnp.concatenate([w[:, :o_b], w[:, o_qa:o_kpe], misc_w], axis=1).astype(BF16)

    wqb = mla_w_q_b[0].reshape(Q_LORA, MLA_HEADS, QK_NOPE + QK_ROPE)
    wqb = jnp.pad(wqb, ((0, 0), (0, 0), (0, QK_PAD - QK_NOPE - QK_ROPE)))
    wqb = wqb.reshape(Q_LORA, MLA_HEADS * QK_PAD).astype(BF16)
    wkvb = mla_w_kv_b[0].reshape(KV_LORA, MLA_HEADS, QK_NOPE + V_HEAD)
    wk = wkvb[:, :, :QK_NOPE].reshape(KV_LORA, MLA_HEADS * QK_NOPE).astype(BF16)
    wv = wkvb[:, :, QK_NOPE:].reshape(KV_LORA, MLA_HEADS * V_HEAD).astype(BF16)
    wkt = wk.T

    lanep = jnp.zeros((2, LANE), F32)
    lanep = lanep.at[0, MISC_A:MISC_A + GDN_HEADS].set(gdn_A_log[0])
    lanep = lanep.at[1, MISC_A:MISC_A + GDN_HEADS].set(gdn_dt_bias[0])
    rowp = jnp.zeros((2 * SUBLANE,), F32)
    rowp = rowp.at[GDN_HEADS:2 * GDN_HEADS].set(gdn_A_log[0])
    rowp = rowp.at[SUBLANE + GDN_HEADS:SUBLANE + 2 * GDN_HEADS].set(gdn_dt_bias[0])
    rowp = jnp.broadcast_to(rowp[:, None], (2 * SUBLANE, GDN_TB))

    g_mix = _row(norm_mix_g[0])
    g_out = _row(gdn_out_norm_g[0])
    mla_gains = (_row(mla_q_a_norm_g[0]), _row(mla_qn_nope_g[0]), _pad_lanes(_row(mla_qn_rope_g[0])),
                 _row(mla_kv_a_norm_g[0]), _row(mla_kn_nope_g[0]), _pad_lanes(_row(mla_kn_rope_g[0])))
    wmix = w_mix_out[0].astype(BF16)
    wq_mem = mem_wq[0].astype(BF16)
    wk_mem = mem_wk[0].astype(BF16)
    wv_mem = mem_wv[0].astype(BF16)
    wo_mem = mem_wo[0].astype(BF16)
    wg = ffn_w_gate[0].astype(BF16)
    wu = ffn_w_up[0].astype(BF16)
    wd = ffn_w_down[0].astype(BF16)
    g_mem = _row(norm_mem_g[0])
    g_ffn = _row(norm_ffn_g[0])
    g_qn = _row(mem_qn_g[0])
    g_kn = _row(mem_kn_g[0])
    conv_w = gdn_conv_w[0]

    xp = x_prompt.reshape(n_tok, D_MODEL)
    tiles_per_seq = seq // TOKEN_TILE
    qkv_act, qkv_tail, gz, qa, kvc, misc = _in_proj_conv(xp, g_mix, w_in_p, conv_w, TOKEN_TILE,
                                                         tiles_per_seq)
    grow = jnp.swapaxes(misc.reshape(bsz, seq, LANE)[:, :, MISC_B:MISC_B + SUBLANE], 1, 2)
    y_gdn, p_s = _gdn_prompt(qkv_act.reshape(bsz, seq, CONV_DIM), misc.reshape(bsz, seq, LANE), grow,
                             gz.reshape(bsz, seq, GDN_V), lanep, rowp, g_out)
    p_conv = qkv_tail.reshape(bsz, tiles_per_seq, SUBLANE, CONV_DIM)[:, -1, SUBLANE - (CONV_W - 1):, :]

    tabs_p = _rope_tables(jnp.arange(seq), seq)
    q_full, k_full, vt_full, p_c, p_kpe = _mla_prep(qa, kvc, misc, tabs_p, wqb, wk, wv.T, mla_gains,
                                                    TOKEN_TILE)
    y_mla_t = _mla_flash(q_full, k_full, vt_full, bsz, seq)

    mem_k, mem_v = _mem_kv(mem_prompt.reshape(-1, D_MODEL), wk_mem, wv_mem, g_kn, TOKEN_TILE)
    n_mem = mem_prompt.shape[1]
    x1, q_mem = _mix_q(xp, y_gdn.reshape(n_tok, GDN_V), y_mla_t, wmix, g_mem, wq_mem, g_qn, MIX_TILE,
                       mla_transposed=True)
    o_mem = _mem_attn(q_mem.reshape(bsz, seq, MEM_DIM), mem_k.reshape(bsz, n_mem, MEM_DIM),
                      mem_v.reshape(bsz, n_mem, MEM_DIM), TOKEN_TILE)
    y_prompt = _out_ffn(x1, o_mem.reshape(n_tok, MEM_DIM), wo_mem, g_ffn, wg, wu, wd, TOKEN_TILE)

    xs = x_sample.reshape(nseq, D_MODEL)
    qkv_s, gz_s, qa_s, kvc_s, misc_s = _in_proj(xs, g_mix, w_in_p, nseq)
    conv_prev = state_gdn_conv[0]
    y_gdn_s, s_new = _gdn_decode(qkv_s, jnp.swapaxes(conv_prev, 0, 1), misc_s, gz_s, state_gdn_S[0],
                                 conv_w, lanep, g_out)
    s_conv = jnp.concatenate([conv_prev[:, 1:, :], qkv_s[:, None, :]], axis=1)

    tabs_s = _rope_tables(jnp.full((1,), past_len), nseq)
    qabs, qpe, c_new, kpe_new = _mla_dec_prep(qa_s, kvc_s, misc_s, tabs_s, wqb, wk, mla_gains)
    pad_rows = lambda a: jnp.pad(jnp.swapaxes(a, 0, 1), ((0, 0), (0, BF16_ROWS - MLA_HEADS), (0, 0)))
    m_run, l_run, acc_run = _mla_dec_attn(
        page_table, cache_mla_ckv[0], jnp.swapaxes(cache_mla_kpe[0], 1, 2),
        pad_rows(qabs).astype(BF16), pad_rows(qpe[:, :, :QK_ROPE]).astype(BF16), wkt)
    heads_first = lambda a: jnp.swapaxes(a[:, :MLA_HEADS], 0, 1)
    y_mla_s = _mla_dec_finish(heads_first(m_run), heads_first(l_run), heads_first(acc_run),
                              qabs, qpe, c_new, kpe_new, wk, wv)

    x1_s, q_mem_s = _mix_q(xs, y_gdn_s, y_mla_s, wmix, g_mem, wq_mem, g_qn, nseq)
    o_mem_s = _mem_attn_dec(q_mem_s.astype(F32).reshape(nseq, MEM_HEADS, MEM_HD),
                            cache_mem_k[0], cache_mem_v[0])
    y_sample = _out_ffn(x1_s, o_mem_s.reshape(nseq, MEM_DIM).astype(BF16),
                        wo_mem, g_ffn, wg, wu, wd, nseq)

    return (
        y_prompt.reshape(bsz, seq, D_MODEL),
        y_sample.reshape(nseq, 1, D_MODEL),
        p_c.reshape(1, bsz, seq, KV_LORA),
        p_kpe.reshape(1, bsz, seq, QK_ROPE),
        mem_k.reshape(1, bsz, n_mem, MEM_HEADS, MEM_HD),
        mem_v.reshape(1, bsz, n_mem, MEM_HEADS, MEM_HD),
        p_s[None],
        p_conv[None],
        c_new.reshape(1, nseq, 1, KV_LORA),
        kpe_new[:, :QK_ROPE].reshape(1, nseq, 1, QK_ROPE),
        s_new[None],
        s_conv[None],
    )
```

```python
import functools

import jax
import jax.numpy as jnp
from jax import lax
from jax.experimental import pallas as pl
from jax.experimental.pallas import tpu as pltpu

F32 = jnp.float32
BF16 = jnp.bfloat16

D_MODEL = 1024
GDN_HEADS = 4
GDN_DK = 128
GDN_DV = 128
CONV_W = 4
CHUNK = 64
MLA_HEADS = 4
Q_LORA = 384
KV_LORA = 256
QK_NOPE = 128
QK_ROPE = 64
V_HEAD = 128
ROPE_THETA = 10000.0
PAGE_SIZE = 128
MEM_HEADS = 4
MEM_HD = 128
EPS = 1e-6
GDN_QK = GDN_HEADS * GDN_DK
GDN_V = GDN_HEADS * GDN_DV
CONV_DIM = 2 * GDN_QK + GDN_V
MLA_SCALE = (QK_NOPE + QK_ROPE) ** -0.5
MEM_DIM = MEM_HEADS * MEM_HD

LANE = 128
SUBLANE = 8
BF16_ROWS = 16
QK_PAD = 2 * LANE
MISC_B = QK_ROPE
MISC_A = QK_ROPE + GDN_HEADS
VMEM_LIMIT = 56 * 1024 * 1024

TOKEN_TILE = 512
MIX_TILE = 512
GDN_TILE = 2 * CHUNK
GDN_TB = 2 * GDN_TILE
PAGES_PER_STEP = 32
DEC_RING = 4


def _cparams(sem):
    return pltpu.CompilerParams(dimension_semantics=sem, vmem_limit_bytes=VMEM_LIMIT)


def _full(shape):
    n = len(shape)
    return pl.BlockSpec(shape, lambda *_: (0,) * n)


def _dot(a, b):
    return jnp.dot(a, b, preferred_element_type=F32)


def _dot_nt(a, b):
    return lax.dot_general(a, b, (((1,), (1,)), ((), ())), preferred_element_type=F32)


def _dot_tn(a, b):
    return lax.dot_general(a, b, (((0,), (0,)), ((), ())), preferred_element_type=F32)


def _split3(x):
    hi = x.astype(BF16)
    r = x - hi.astype(F32)
    mid = r.astype(BF16)
    lo = (r - mid.astype(F32)).astype(BF16)
    return hi, mid, lo


def _dot01_left(m01, x):
    hi, mid, lo = _split3(x)
    return _dot(m01, hi) + _dot(m01, mid) + _dot(m01, lo)


def _dot01_right(x, m01):
    hi, mid, lo = _split3(x)
    return _dot(hi, m01) + _dot(mid, m01) + _dot(lo, m01)


def _rms(x, g, n=None):
    n = x.shape[-1] if n is None else n
    ms = jnp.sum(x * x, axis=-1, keepdims=True) * (1.0 / n)
    return x * lax.rsqrt(ms + EPS) * g


def _silu(x):
    return x * jax.nn.sigmoid(x)


def _softplus(x):
    return jnp.maximum(x, 0.0) + jnp.log1p(jnp.exp(-jnp.abs(x)))


def _iota(shape, dim):
    return lax.broadcasted_iota(jnp.int32, shape, dim)


def _block_id(i, size):
    assert size & (size - 1) == 0
    return lax.shift_right_logical(i, size.bit_length() - 1)


def _in_proj_body(x_ref, g_ref, w_ref, qkv_ref, gz_ref, qa_ref, kvc_ref, misc_ref):
    h = _rms(x_ref[...], g_ref[...]).astype(BF16)
    off = 0
    for ref in (qkv_ref, gz_ref, qa_ref, kvc_ref, misc_ref):
        n = ref.shape[-1]
        ref[...] = _dot(h, w_ref[:, off:off + n])
        off += n


def _in_proj(x, g, w, tm):
    t = x.shape[0]
    widths = (CONV_DIM, GDN_V, Q_LORA, KV_LORA, LANE)
    return pl.pallas_call(
        _in_proj_body,
        grid=(t // tm,),
        in_specs=[pl.BlockSpec((tm, D_MODEL), lambda i: (i, 0)), _full(g.shape), _full(w.shape)],
        out_specs=[pl.BlockSpec((tm, n), lambda i: (i, 0)) for n in widths],
        out_shape=[jax.ShapeDtypeStruct((t, n), F32) for n in widths],
        compiler_params=_cparams(("parallel",)),
        name="in_proj",
    )(x, g, w)


def _qk_l2norm(x, scale):
    return x * (lax.rsqrt(jnp.sum(x * x, axis=-1, keepdims=True) + EPS) * scale)


def _in_proj_conv_body(x_ref, g_ref, w_ref, convw_ref, qkv_ref, tail_ref, gz_ref, qa_ref, kvc_ref,
                       misc_ref, xe_ref, *, tiles_per_seq):
    tm = x_ref.shape[0]
    @pl.when(pl.program_id(0) % tiles_per_seq == 0)
    def _():
        xe_ref[0:SUBLANE, :] = jnp.zeros((SUBLANE, CONV_DIM), F32)

    h = _rms(x_ref[...], g_ref[...]).astype(BF16)
    w = convw_ref[...]
    pair = 2 * GDN_DK
    blocks = [slice(blk * pair, (blk + 1) * pair) for blk in range(CONV_DIM // pair)]
    for sl in blocks:
        xe_ref[SUBLANE:SUBLANE + tm, sl] = _dot(h, w_ref[:, sl])
    off = CONV_DIM
    for ref in (gz_ref, qa_ref, kvc_ref, misc_ref):
        n = ref.shape[-1]
        ref[...] = _dot(h, w_ref[:, off:off + n])
        off += n
    for blk, sl in enumerate(blocks):
        conv = xe_ref[SUBLANE:SUBLANE + tm, sl] * w[CONV_W - 1:CONV_W, sl]
        for i in range(CONV_W - 1):
            lo = SUBLANE - (CONV_W - 1) + i
            conv = conv + xe_ref[lo:lo + tm, sl] * w[i:i + 1, sl]
        act = _silu(conv)
        for half in range(2):
            head_blk = 2 * blk + half
            hs = slice(half * GDN_DK, (half + 1) * GDN_DK)
            out_sl = slice(head_blk * GDN_DK, (head_blk + 1) * GDN_DK)
            if head_blk < GDN_HEADS:
                qkv_ref[:, out_sl] = _qk_l2norm(act[:, hs], GDN_DK ** -0.5)
            elif head_blk < 2 * GDN_HEADS:
                qkv_ref[:, out_sl] = _qk_l2norm(act[:, hs], 1.0)
            else:
                qkv_ref[:, out_sl] = act[:, hs]
    tail = xe_ref[tm:SUBLANE + tm, :]
    tail_ref[...] = tail
    xe_ref[0:SUBLANE, :] = tail


def _in_proj_conv(x, g, w, convw, tm, tiles_per_seq):
    t = x.shape[0]
    widths = (CONV_DIM, CONV_DIM, GDN_V, Q_LORA, KV_LORA, LANE)
    rows = (tm, SUBLANE, tm, tm, tm, tm)
    return pl.pallas_call(
        functools.partial(_in_proj_conv_body, tiles_per_seq=tiles_per_seq),
        grid=(t // tm,),
        in_specs=[pl.BlockSpec((tm, D_MODEL), lambda i: (i, 0)), _full(g.shape), _full(w.shape),
                  _full(convw.shape)],
        out_specs=[pl.BlockSpec((r, n), lambda i: (i, 0)) for r, n in zip(rows, widths)],
        out_shape=[jax.ShapeDtypeStruct((t // tm * r, n), F32) for r, n in zip(rows, widths)],
        scratch_shapes=[pltpu.VMEM((SUBLANE + tm, CONV_DIM), F32)],
        compiler_params=_cparams(("arbitrary",)),
        name="in_proj_conv",
    )(x, g, w, convw)


def _gate_lane_values(misc, lanep):
    lane = _iota(misc.shape, 1)
    is_a = (lane >= MISC_A) & (lane < MISC_A + GDN_HEADS)
    beta_l = jax.nn.sigmoid(misc)
    g_l = -jnp.exp(lanep[0:1, :]) * _softplus(jnp.where(is_a, misc, 0.0) + lanep[1:2, :])
    return beta_l, g_l


def _expand_matrix(first_lane):
    r = _iota((LANE, GDN_V), 0)
    c = _iota((LANE, GDN_V), 1)
    return jnp.where(r == first_lane + _block_id(c, GDN_DV), 1.0, 0.0).astype(BF16)


def _gdn_gate_out(o, z, gout):
    return _rms(o, gout) * _silu(z)


def _tri_inverse(a_list, eye, same16, same32):
    def mm(xs, ys):
        return [_dot(x.astype(BF16), y.astype(BF16)) for x, y in zip(xs, ys)]

    def add(ts, us):
        return [t + u for t, u in zip(ts, us)]

    def sub(ts, us):
        return [t - u for t, u in zip(ts, us)]

    n = eye.shape[0]
    ad = [jnp.where(same16, a, 0.0) for a in a_list]
    t = [eye - x for x in ad]
    p = mm(ad, ad)
    for _ in range(2):
        tp = mm([jnp.concatenate([x, y], axis=0) for x, y in zip(t, p)], p)
        t = add(t, [x[:n] for x in tp])
        p = [x[n:] for x in tp]
    t = add(t, mm(t, p))
    b1 = [jnp.where(same32, a, 0.0) - x for a, x in zip(a_list, ad)]
    t = sub(t, mm(t, mm(b1, t)))
    b2 = [jnp.where(same32, 0.0, a) for a in a_list]
    t = sub(t, mm(t, mm(b2, t)))
    return t


def _gdn_prompt_body(qkv_ref, misc_ref, grow_ref, gz_ref, lanep_ref, rowp_ref, gout_ref,
                     y_ref, s_out_ref, s_ref):
    tb = GDN_TB
    t_idx = pl.program_id(1)

    @pl.when(t_idx == 0)
    def _():
        s_ref[...] = jnp.zeros(s_ref.shape, F32)

    tile = GDN_TILE
    tiles = range(tb // tile)
    rows = [slice(t * tile, (t + 1) * tile) for t in tiles]
    ri = _iota((tile, tile), 0)
    ci = _iota((tile, tile), 1)
    same64 = _block_id(ri, CHUNK) == _block_id(ci, CHUNK)
    lower = same64 & (ci <= ri)
    strict = same64 & (ci < ri)
    same16 = _block_id(ri, 16) == _block_id(ci, 16)
    same32 = _block_id(ri, 32) == _block_id(ci, 32)
    eye = jnp.where(ri == ci, 1.0, 0.0)
    l_incl = jnp.where(lower, 1.0, 0.0).astype(BF16)
    u_incl = jnp.where(same64 & (ri <= ci), 1.0, 0.0).astype(BF16)
    u_strict = jnp.where(same64 & (ci > ri), 1.0, 0.0).astype(BF16)

    beta_l, g_l = _gate_lane_values(misc_ref[0], lanep_ref[...])
    e_b = _expand_matrix(MISC_B)
    e_a = _expand_matrix(MISC_A)
    beta_bc = _dot01_right(beta_l, e_b)
    gc_col = jnp.concatenate([_dot01_left(l_incl, g_l[r]) for r in rows], axis=0)
    gt_col = jnp.concatenate([_dot01_left(u_strict, g_l[r]) for r in rows], axis=0)
    gc_bc = _dot01_right(gc_col, e_a)
    gt_bc = _dot01_right(gt_col, e_a)
    egc = jnp.exp(gc_bc)
    etail = jnp.exp(gt_bc)

    grow = grow_ref[0]
    rowi = _iota(grow.shape, 0)
    g_r = -jnp.exp(rowp_ref[0:SUBLANE, :]) * _softplus(
        jnp.where(rowi >= GDN_HEADS, grow, 0.0) + rowp_ref[SUBLANE:2 * SUBLANE, :])
    gcr = [_dot01_right(g_r[:, r], u_incl) for r in rows]

    gout = gout_ref[...]
    gz = gz_ref[0]
    heads = range(GDN_HEADS)
    sls = [slice(h * GDN_DK, (h + 1) * GDN_DK) for h in heads]
    units = [(t, h) for t in tiles for h in heads]
    qh, kh, kb, vb, dm = {}, {}, {}, {}, {}
    for h in heads:
        q = qkv_ref[0, :, h * GDN_DK:(h + 1) * GDN_DK]
        k = qkv_ref[0, :, GDN_QK + h * GDN_DK:GDN_QK + (h + 1) * GDN_DK]
        v = qkv_ref[0, :, 2 * GDN_QK + h * GDN_DV:2 * GDN_QK + (h + 1) * GDN_DV]
        bb = beta_bc[:, sls[h]]
        for t in tiles:
            qh[t, h] = q[rows[t]]
            kh[t, h] = k[rows[t]]
            kb[t, h] = (k * bb)[rows[t]]
            vb[t, h] = (v * bb)[rows[t]]
            diff = gc_bc[rows[t], sls[h]] - gcr[t][GDN_HEADS + h:GDN_HEADS + h + 1, :]
            dm[t, h] = jnp.where(lower, jnp.exp(jnp.where(lower, diff, 0.0)), 0.0)
    p = {u: _dot_nt(jnp.concatenate([kb[u], qh[u]], axis=0).astype(BF16), kh[u].astype(BF16))
         for u in units}
    a_mat = [jnp.where(strict, p[u][:tile] * dm[u], 0.0) for u in units]
    attn = {u: (p[u][tile:] * dm[u]).astype(BF16) for u in units}
    t_mat = dict(zip(units, _tri_inverse(a_mat, eye, same16, same32)))
    egc_u = {(t, h): egc[rows[t], sls[h]] for t, h in units}
    uw = {u: _dot(t_mat[u].astype(BF16),
                  jnp.concatenate([vb[u], kb[u] * egc_u[u]], axis=1).astype(BF16)) for u in units}
    qd = {u: qh[u] * egc_u[u] for u in units}
    kt = {(t, h): (kh[t, h] * etail[rows[t], sls[h]]).astype(BF16) for t, h in units}
    s = [s_ref[h] for h in heads]
    o_inter = {u: [] for u in units}
    v_new = {u: [] for u in units}
    for t in tiles:
        for c in range(tile // CHUNK):
            rs = slice(c * CHUNK, (c + 1) * CHUNK)
            last = t * tile + (c + 1) * CHUNK - 1
            wq = [_dot(jnp.concatenate([uw[t, h][rs, GDN_DV:], qd[t, h][rs]], axis=0).astype(BF16),
                       s[h].astype(BF16)) for h in heads]
            vn = [uw[t, h][rs, :GDN_DV] - wq[h][:CHUNK] for h in heads]
            egl = [jnp.exp(gc_bc[last:last + 1, sls[h]]) for h in heads]
            s = [s[h] * egl[h] + _dot_tn(kt[t, h][rs], vn[h].astype(BF16)) for h in heads]
            for h in heads:
                o_inter[t, h].append(wq[h][CHUNK:])
                v_new[t, h].append(vn[h])
    o = {u: jnp.concatenate(o_inter[u], axis=0)
         + _dot(attn[u], jnp.concatenate(v_new[u], axis=0).astype(BF16)) for u in units}
    for h in heads:
        s_ref[h] = s[h]
        for t in tiles:
            y_ref[0, rows[t], sls[h]] = _gdn_gate_out(o[t, h], gz[rows[t], sls[h]], gout).astype(BF16)

    @pl.when(t_idx == pl.num_programs(1) - 1)
    def _():
        s_out_ref[0] = s_ref[...]


def _gdn_prompt(qkv, misc, grow, gz, lanep, rowp, gout):
    b, t, _ = qkv.shape
    tb = GDN_TB
    return pl.pallas_call(
        _gdn_prompt_body,
        grid=(b, t // tb),
        in_specs=[
            pl.BlockSpec((1, tb, CONV_DIM), lambda i, j: (i, j, 0)),
            pl.BlockSpec((1, tb, LANE), lambda i, j: (i, j, 0)),
            pl.BlockSpec((1, SUBLANE, tb), lambda i, j: (i, 0, j)),
            pl.BlockSpec((1, tb, GDN_V), lambda i, j: (i, j, 0)),
            _full(lanep.shape), _full(rowp.shape), _full(gout.shape),
        ],
        out_specs=[
            pl.BlockSpec((1, tb, GDN_V), lambda i, j: (i, j, 0)),
            pl.BlockSpec((1, GDN_HEADS, GDN_DK, GDN_DV), lambda i, j: (i, 0, 0, 0)),
        ],
        out_shape=[
            jax.ShapeDtypeStruct((b, t, GDN_V), BF16),
            jax.ShapeDtypeStruct((b, GDN_HEADS, GDN_DK, GDN_DV), F32),
        ],
        scratch_shapes=[pltpu.VMEM((GDN_HEADS, GDN_DK, GDN_DV), F32)],
        compiler_params=_cparams(("parallel", "arbitrary")),
        name="gdn_prompt",
    )(qkv, misc, grow, gz, lanep, rowp, gout)


GDN_DEC_SEQS = 8


def _gdn_decode_body(qkv_ref, cprev_ref, misc_ref, gz_ref, s_in_ref, convw_ref, lanep_ref, gout_ref,
                     y_ref, s_out_ref, o_ref):
    ns = GDN_DEC_SEQS
    w = convw_ref[...]
    conv = qkv_ref[...] * w[CONV_W - 1:CONV_W, :]
    for i in range(CONV_W - 1):
        conv = conv + cprev_ref[i] * w[i:i + 1, :]
    conv = _silu(conv)
    beta_l, g_l = _gate_lane_values(misc_ref[...], lanep_ref[...])
    beta_bc = _dot01_right(beta_l, _expand_matrix(MISC_B))
    eg = jnp.exp(_dot01_right(g_l, _expand_matrix(MISC_A)))
    row0 = jnp.where(_iota((SUBLANE, LANE), 0) == 0, 1.0, 0.0).astype(BF16)
    units = [(h, i) for h in range(GDN_HEADS) for i in range(ns)]
    qn, kn, vv = {}, {}, {}
    for h in range(GDN_HEADS):
        qh = conv[:, h * GDN_DK:(h + 1) * GDN_DK]
        kh = conv[:, GDN_QK + h * GDN_DK:GDN_QK + (h + 1) * GDN_DK]
        qn[h] = qh * lax.rsqrt(jnp.sum(qh * qh, axis=-1, keepdims=True) + EPS) * (GDN_DK ** -0.5)
        kn[h] = kh * lax.rsqrt(jnp.sum(kh * kh, axis=-1, keepdims=True) + EPS)
        vv[h] = conv[:, 2 * GDN_QK + h * GDN_DV:2 * GDN_QK + (h + 1) * GDN_DV]
    sl = lambda h: slice(h * GDN_DK, (h + 1) * GDN_DK)
    k8 = {u: jnp.broadcast_to(kn[u[0]][u[1]:u[1] + 1, :], (SUBLANE, GDN_DK)) for u in units}
    s_dec = {(h, i): s_in_ref[i, h] * eg[i:i + 1, sl(h)] for h, i in units}
    kv = {u: _dot(k8[u].astype(BF16), s_dec[u].astype(BF16))[0:1, :] for u in units}
    kcol = {}
    for u in units:
        k_hi, k_mid, k_lo = _split3(k8[u])
        kcol[u] = _dot_tn(k_hi, row0) + _dot_tn(k_mid, row0) + _dot_tn(k_lo, row0)
    s_new = {}
    for h, i in units:
        delta = (vv[h][i:i + 1, :] - kv[h, i]) * beta_bc[i:i + 1, sl(h)]
        s_new[h, i] = s_dec[h, i] + kcol[h, i] * delta
        s_out_ref[i, h] = s_new[h, i]
    for h, i in units:
        q8 = jnp.broadcast_to(qn[h][i:i + 1, :], (SUBLANE, GDN_DK))
        o_ref[i:i + 1, sl(h)] = _dot(q8.astype(BF16), s_new[h, i].astype(BF16))[0:1, :]
    gout = gout_ref[...]
    gz = gz_ref[...]
    for h in range(GDN_HEADS):
        sl = slice(h * GDN_DV, (h + 1) * GDN_DV)
        y_ref[:, sl] = _gdn_gate_out(o_ref[:, sl], gz[:, sl], gout).astype(BF16)


def _gdn_decode(qkv, cprev, misc, gz, s_in, convw, lanep, gout):
    n = qkv.shape[0]
    ns = GDN_DEC_SEQS
    state_spec = pl.BlockSpec((ns, GDN_HEADS, GDN_DK, GDN_DV), lambda i: (i, 0, 0, 0))
    return pl.pallas_call(
        _gdn_decode_body,
        grid=(n // ns,),
        in_specs=[
            pl.BlockSpec((ns, CONV_DIM), lambda i: (i, 0)),
            pl.BlockSpec((CONV_W - 1, ns, CONV_DIM), lambda i: (0, i, 0)),
            pl.BlockSpec((ns, LANE), lambda i: (i, 0)),
            pl.BlockSpec((ns, GDN_V), lambda i: (i, 0)),
            state_spec,
            _full(convw.shape), _full(lanep.shape), _full(gout.shape),
        ],
        out_specs=[pl.BlockSpec((ns, GDN_V), lambda i: (i, 0)), state_spec],
        out_shape=[
            jax.ShapeDtypeStruct((n, GDN_V), BF16),
            jax.ShapeDtypeStruct(s_in.shape, F32),
        ],
        scratch_shapes=[pltpu.VMEM((ns, GDN_V), F32)],
        compiler_params=_cparams(("parallel",)),
        name="gdn_decode",
    )(qkv, cprev, misc, gz, s_in, convw, lanep, gout)


def _rope_padded(x, cos, sina, sinb):
    return (x * cos + pltpu.roll(x, LANE - QK_ROPE // 2, axis=1) * sina
            + pltpu.roll(x, QK_ROPE // 2, axis=1) * sinb)


def _mla_queries(qa, wqb_ref, g_qa, g_nope, g_rope, cos, sina, sinb):
    q = _dot(_rms(qa, g_qa).astype(BF16), wqb_ref[...])
    out = []
    for h in range(MLA_HEADS):
        qn = _rms(q[:, h * QK_PAD:h * QK_PAD + QK_NOPE], g_nope)
        qp = _rms(q[:, h * QK_PAD + QK_NOPE:(h + 1) * QK_PAD], g_rope, QK_ROPE)
        out.append((qn, _rope_padded(qp, cos, sina, sinb)))
    return out


def _mla_latent(kvc, misc, g_kva, g_krope, cos, sina, sinb):
    c = _rms(kvc, g_kva)
    lane = _iota(misc.shape, 1)
    kp = _rms(jnp.where(lane < QK_ROPE, misc, 0.0), g_krope, QK_ROPE)
    return c, _rope_padded(kp, cos, sina, sinb)


def _mla_prep_body(qa_ref, kvc_ref, misc_ref, tab_ref, wqb_ref, wk_ref, wvt_ref,
                   g_qa_ref, g_nope_ref, g_rope_ref, g_kva_ref, g_knope_ref, g_krope_ref,
                   q_ref, k_ref, vt_ref, c_ref, kpe_ref):
    cos, sina, sinb = tab_ref[0], tab_ref[1], tab_ref[2]
    qs = _mla_queries(qa_ref[...], wqb_ref, g_qa_ref[...], g_nope_ref[...], g_rope_ref[...],
                      cos, sina, sinb)
    c, kp = _mla_latent(kvc_ref[...], misc_ref[...], g_kva_ref[...], g_krope_ref[...], cos, sina, sinb)
    c_ref[...] = c
    kpe_ref[...] = kp[:, :QK_ROPE]
    cb = c.astype(BF16)
    kn = _dot(cb, wk_ref[...])
    vt = _dot_nt(wvt_ref[...], cb)
    g_kn = g_knope_ref[...]
    kp16 = kp.astype(BF16)
    for h in range(MLA_HEADS):
        qn, qp = qs[h]
        q_ref[h, :, 0:QK_NOPE] = (qn * MLA_SCALE).astype(BF16)
        q_ref[h, :, QK_NOPE:QK_PAD] = (qp * MLA_SCALE).astype(BF16)
        k_ref[h, :, 0:QK_NOPE] = _rms(kn[:, h * QK_NOPE:(h + 1) * QK_NOPE], g_kn).astype(BF16)
        k_ref[h, :, QK_NOPE:QK_PAD] = kp16
        vt_ref[h] = vt[h * V_HEAD:(h + 1) * V_HEAD, :].astype(BF16)


def _mla_prep(qa, kvc, misc, tabs, wqb, wk, wvt, gains, tm):
    t = qa.shape[0]
    n_tab = tabs.shape[1] // tm
    row = lambda n: pl.BlockSpec((tm, n), lambda i: (i, 0))
    head = lambda n: pl.BlockSpec((MLA_HEADS, tm, n), lambda i: (0, i, 0))
    return pl.pallas_call(
        _mla_prep_body,
        grid=(t // tm,),
        in_specs=[row(Q_LORA), row(KV_LORA), row(LANE),
                  pl.BlockSpec((3, tm, LANE), lambda i: (0, i % n_tab, 0)),
                  _full(wqb.shape), _full(wk.shape), _full(wvt.shape)]
                 + [_full(g.shape) for g in gains],
        out_specs=[head(QK_PAD), head(QK_PAD),
                   pl.BlockSpec((MLA_HEADS, V_HEAD, tm), lambda i: (0, 0, i)),
                   row(KV_LORA), row(QK_ROPE)],
        out_shape=[
            jax.ShapeDtypeStruct((MLA_HEADS, t, QK_PAD), BF16),
            jax.ShapeDtypeStruct((MLA_HEADS, t, QK_PAD), BF16),
            jax.ShapeDtypeStruct((MLA_HEADS, V_HEAD, t), BF16),
            jax.ShapeDtypeStruct((t, KV_LORA), F32),
            jax.ShapeDtypeStruct((t, QK_ROPE), F32),
        ],
        compiler_params=_cparams(("parallel",)),
        name="mla_prep",
    )(qa, kvc, misc, tabs, wqb, wk, wvt, *gains)


FLASH_T = 512


FLASH_HEADS = 2


def _flash_body(q_ref, k_ref, vt_ref, o_ref):
    tq = FLASH_T
    qi = pl.program_id(2)
    heads = range(FLASH_HEADS)
    q = [q_ref[h] for h in heads]

    def block(j, carry, masked):
        start = pl.multiple_of(j * tq, tq)
        s = [_dot_nt(k_ref[h, pl.ds(start, tq), :], q[h]) for h in heads]
        if masked:
            keep = _iota(s[0].shape, 0) <= _iota(s[0].shape, 1)
            s = [jnp.where(keep, x, -jnp.inf) for x in s]
        m_new = [jnp.maximum(carry[h][0], jnp.max(s[h], axis=0, keepdims=True)) for h in heads]
        p = [jnp.exp(s[h] - m_new[h]) for h in heads]
        pv = [_dot(vt_ref[h, :, pl.ds(start, tq)], p[h].astype(BF16)) for h in heads]
        out = []
        for h in heads:
            m, l, acc = carry[h]
            corr = jnp.exp(m - m_new[h])
            out.append((m_new[h], l * corr + jnp.sum(p[h], axis=0, keepdims=True),
                        acc * corr + pv[h]))
        return tuple(out)

    init = tuple((jnp.full((1, tq), -jnp.inf, F32), jnp.zeros((1, tq), F32),
                  jnp.zeros((V_HEAD, tq), F32)) for _ in heads)
    carry = lax.fori_loop(0, qi, lambda j, c: block(j, c, False), init)
    carry = block(qi, carry, True)
    for h in heads:
        _, l, acc = carry[h]
        o_ref[h * V_HEAD:(h + 1) * V_HEAD, :] = (acc / l).astype(BF16)


def _mla_flash(q, k, vt, b, t):
    tq = FLASH_T
    nq = t // tq
    nh = FLASH_HEADS
    return pl.pallas_call(
        _flash_body,
        grid=(b, MLA_HEADS // nh, nq),
        in_specs=[
            pl.BlockSpec((nh, tq, QK_PAD), lambda i, h, j: (h, i * nq + j, 0)),
            pl.BlockSpec((nh, t, QK_PAD), lambda i, h, j: (h, i, 0)),
            pl.BlockSpec((nh, V_HEAD, t), lambda i, h, j: (h, 0, i)),
        ],
        out_specs=pl.BlockSpec((nh * V_HEAD, tq), lambda i, h, j: (h, i * nq + j)),
        out_shape=jax.ShapeDtypeStruct((MLA_HEADS * V_HEAD, b * t), BF16),
        compiler_params=_cparams(("parallel", "parallel", "arbitrary")),
        name="mla_flash",
    )(q, k, vt)


def _mla_dec_prep_body(qa_ref, kvc_ref, misc_ref, tab_ref, wqb_ref, wk_ref,
                       g_qa_ref, g_nope_ref, g_rope_ref, g_kva_ref, g_knope_ref, g_krope_ref,
                       qabs_ref, qpe_ref, c_ref, kpe_ref):
    cos, sina, sinb = tab_ref[0], tab_ref[1], tab_ref[2]
    qs = _mla_queries(qa_ref[...], wqb_ref, g_qa_ref[...], g_nope_ref[...], g_rope_ref[...],
                      cos, sina, sinb)
    c, kp = _mla_latent(kvc_ref[...], misc_ref[...], g_kva_ref[...], g_krope_ref[...], cos, sina, sinb)
    c_ref[...] = c
    kpe_ref[...] = kp
    g_kn = g_knope_ref[...]
    for h in range(MLA_HEADS):
        qn, qp = qs[h]
        qabs_ref[h] = _dot_nt((qn * g_kn * MLA_SCALE).astype(BF16),
                              wk_ref[:, h * QK_NOPE:(h + 1) * QK_NOPE])
        qpe_ref[h] = qp * MLA_SCALE


def _mla_dec_prep(qa, kvc, misc, tabs, wqb, wk, gains):
    n = qa.shape[0]
    args = (qa, kvc, misc, tabs, wqb, wk) + tuple(gains)
    return pl.pallas_call(
        _mla_dec_prep_body,
        grid=(1,),
        in_specs=[_full(a.shape) for a in args],
        out_specs=[_full((MLA_HEADS, n, KV_LORA)), _full((MLA_HEADS, n, LANE)),
                   _full((n, KV_LORA)), _full((n, LANE))],
        out_shape=[
            jax.ShapeDtypeStruct((MLA_HEADS, n, KV_LORA), F32),
            jax.ShapeDtypeStruct((MLA_HEADS, n, LANE), F32),
            jax.ShapeDtypeStruct((n, KV_LORA), F32),
            jax.ShapeDtypeStruct((n, LANE), F32),
        ],
        compiler_params=_cparams(("arbitrary",)),
        name="mla_dec_prep",
    )(*args)


def _mla_dec_attn_body(pt_ref, qabs_ref, qpe_ref, wkt_ref, ckv_hbm, kpe_hbm,
                       m_out, l_out, acc_out, lhs_ref, cbuf, kbuf, c16, sem_c, sem_k, *, blocks_per_seq):
    npg = PAGES_PER_STEP
    n_seq = qabs_ref.shape[0]
    total_blocks = n_seq * blocks_per_seq
    nk = MLA_HEADS * QK_NOPE
    keys = npg * PAGE_SIZE

    def block_copies(gb, page_of):
        slot = gb % DEC_RING
        cps = []
        for i in range(npg):
            page = page_of(gb * npg + i)
            cps.append(pltpu.make_async_copy(ckv_hbm.at[page], cbuf.at[slot, i], sem_c.at[slot]))
            cps.append(pltpu.make_async_copy(kpe_hbm.at[page], kbuf.at[slot, i], sem_k.at[slot]))
        return cps

    def start_block(gb):
        for cp in block_copies(gb, lambda idx: pt_ref[idx]):
            cp.start()

    def wait_block(gb):
        for cp in block_copies(gb, lambda idx: 0):
            cp.wait()

    lhs_ref[0:nk, :] = wkt_ref[...]
    for gb in range(DEC_RING - 1):
        start_block(gb)

    def scores(gb):
        seq = gb // blocks_per_seq

        @pl.when(gb % blocks_per_seq == 0)
        def _():
            lhs_ref[nk:nk + BF16_ROWS, :] = qabs_ref[seq]

        slot = gb % DEC_RING
        c16[gb % 2] = cbuf[slot].reshape(keys, KV_LORA).astype(BF16)
        kpt = jnp.concatenate([kbuf[slot, i] for i in range(npg)], axis=1).astype(BF16)
        kt = _dot_nt(lhs_ref[...], c16[gb % 2])
        ssq = [jnp.sum(jnp.square(kt[h * QK_NOPE:(h + 1) * QK_NOPE]), axis=0, keepdims=True)
               for h in range(MLA_HEADS)]
        ssq = jnp.concatenate(ssq + [jnp.ones((SUBLANE - MLA_HEADS, keys), F32)], axis=0)
        pe = _dot(qpe_ref[seq], kpt)
        return kt[nk:nk + SUBLANE] * lax.rsqrt(ssq * (1.0 / QK_NOPE) + EPS) + pe[0:SUBLANE]

    def accumulate(gb, s, carry):
        m_old, l_old, acc_old = carry
        m_new = jnp.maximum(m_old, jnp.max(s, axis=-1, keepdims=True))
        p = jnp.exp(s - m_new)
        corr = jnp.exp(m_old - m_new)
        l_new = l_old * corr + jnp.sum(p, axis=-1, keepdims=True)
        acc_new = acc_old * corr + _dot(p.astype(BF16), c16[gb % 2])
        return m_new, l_new, acc_new

    def finish(seq, state):
        m_fin, l_fin, acc_fin = state
        m_out[seq] = jnp.broadcast_to(m_fin, m_out.shape[1:])
        l_out[seq] = jnp.broadcast_to(l_fin, l_out.shape[1:])
        acc_out[seq] = acc_fin

    def fetch_and_score(gb, after):
        nxt = gb + (DEC_RING - 1)

        @pl.when(nxt < total_blocks)
        def _():
            start_block(nxt)

        wait_block(gb)
        s_cur = scores(gb)
        return s_cur, after()

    init = (jnp.full((SUBLANE, 1), -jnp.inf, F32), jnp.zeros((SUBLANE, 1), F32),
            jnp.zeros((SUBLANE, KV_LORA), F32))

    def body(gb, carry):
        s_prev, state = carry
        s_cur, state = fetch_and_score(gb, lambda: accumulate(gb - 1, s_prev, state))
        closes_seq = gb % blocks_per_seq == 0

        @pl.when(closes_seq)
        def _():
            finish(gb // blocks_per_seq - 1, state)

        return s_cur, tuple(jnp.where(closes_seq, i, x) for i, x in zip(init, state))

    s_last, state = lax.fori_loop(1, total_blocks, body, fetch_and_score(0, lambda: init))
    finish(n_seq - 1, accumulate(total_blocks - 1, s_last, state))


def _mla_dec_attn(page_table, ckv_pool, kpe_pool, qabs, qpe, wkt):
    n, n_pages = page_table.shape
    npg = PAGES_PER_STEP
    nk = MLA_HEADS * QK_NOPE

    assert n_pages % npg == 0 and n * (n_pages // npg) >= DEC_RING
    whole = lambda shape: pl.BlockSpec(shape, lambda i, pt: (0,) * len(shape))
    grid_spec = pltpu.PrefetchScalarGridSpec(
        num_scalar_prefetch=1,
        grid=(1,),
        in_specs=[whole(qabs.shape), whole(qpe.shape), whole(wkt.shape),
                  pl.BlockSpec(memory_space=pl.ANY), pl.BlockSpec(memory_space=pl.ANY)],
        out_specs=[whole((n, SUBLANE, LANE)), whole((n, SUBLANE, LANE)), whole((n, SUBLANE, KV_LORA))],
        scratch_shapes=[
            pltpu.VMEM((nk + BF16_ROWS, KV_LORA), BF16),
            pltpu.VMEM((DEC_RING, npg, PAGE_SIZE, KV_LORA), F32),
            pltpu.VMEM((DEC_RING, npg, QK_ROPE, PAGE_SIZE), F32),
            pltpu.VMEM((2, npg * PAGE_SIZE, KV_LORA), BF16),
            pltpu.SemaphoreType.DMA((DEC_RING,)),
            pltpu.SemaphoreType.DMA((DEC_RING,)),
        ],
    )
    return pl.pallas_call(
        functools.partial(_mla_dec_attn_body, blocks_per_seq=n_pages // npg),
        grid_spec=grid_spec,
        out_shape=[
            jax.ShapeDtypeStruct((n, SUBLANE, LANE), F32),
            jax.ShapeDtypeStruct((n, SUBLANE, LANE), F32),
            jax.ShapeDtypeStruct((n, SUBLANE, KV_LORA), F32),
        ],
        compiler_params=_cparams(("arbitrary",)),
        name="mla_dec_attn",
    )(page_table.reshape(-1), qabs, qpe, wkt, ckv_pool, kpe_pool)


def _mla_dec_finish_body(m_ref, l_ref, acc_ref, qabs_ref, qpe_ref, c_ref, kpe_ref, wk_ref, wv_ref,
                         y_ref):
    c = c_ref[...]
    kp = kpe_ref[...]
    cb = c.astype(BF16)
    kn = _dot(cb, wk_ref[...])
    for h in range(MLA_HEADS):
        knh = kn[:, h * QK_NOPE:(h + 1) * QK_NOPE]
        r = lax.rsqrt(jnp.sum(knh * knh, axis=-1, keepdims=True) * (1.0 / QK_NOPE) + EPS)
        s_new = (r * jnp.sum(qabs_ref[h] * c, axis=-1, keepdims=True)
                 + jnp.sum(qpe_ref[h] * kp, axis=-1, keepdims=True))
        m_old = m_ref[h][:, 0:1]
        m_new = jnp.maximum(m_old, s_new)
        p = jnp.exp(s_new - m_new)
        corr = jnp.exp(m_old - m_new)
        l = l_ref[h][:, 0:1] * corr + p
        lat = (acc_ref[h] * corr + p * c) / l
        y_ref[:, h * V_HEAD:(h + 1) * V_HEAD] = _dot(
            lat.astype(BF16), wv_ref[:, h * V_HEAD:(h + 1) * V_HEAD]).astype(BF16)


def _mla_dec_finish(m, l, acc, qabs, qpe, c, kpe, wk, wv):
    n = c.shape[0]
    args = (m, l, acc, qabs, qpe, c, kpe, wk, wv)
    return pl.pallas_call(
        _mla_dec_finish_body,
        grid=(1,),
        in_specs=[_full(a.shape) for a in args],
        out_specs=_full((n, MLA_HEADS * V_HEAD)),
        out_shape=jax.ShapeDtypeStruct((n, MLA_HEADS * V_HEAD), BF16),
        compiler_params=_cparams(("arbitrary",)),
        name="mla_dec_finish",
    )(*args)


def _mix_q_body(x_ref, yg_ref, ym_ref, wmix_ref, g_mem_ref, wq_ref, g_qn_ref, x1_ref, q_ref, *,
                mla_transposed):
    mla_dot = _dot_tn if mla_transposed else _dot
    x1 = (x_ref[...] + _dot(yg_ref[...], wmix_ref[0:GDN_V, :])
          + mla_dot(ym_ref[...], wmix_ref[GDN_V:, :]))
    x1_ref[...] = x1
    q = _dot(_rms(x1, g_mem_ref[...]).astype(BF16), wq_ref[...])
    g_qn = g_qn_ref[...]
    for h in range(MEM_HEADS):
        sl = slice(h * MEM_HD, (h + 1) * MEM_HD)
        q_ref[:, sl] = (_rms(q[:, sl], g_qn) * (MEM_HD ** -0.5)).astype(BF16)


def _mix_q(x, yg, ym, wmix, g_mem, wq, g_qn, tm, mla_transposed=False):
    t = x.shape[0]
    row = lambda n: pl.BlockSpec((tm, n), lambda i: (i, 0))
    mla_width = MLA_HEADS * V_HEAD
    ym_spec = pl.BlockSpec((mla_width, tm), lambda i: (0, i)) if mla_transposed else row(mla_width)
    return pl.pallas_call(
        functools.partial(_mix_q_body, mla_transposed=mla_transposed),
        grid=(t // tm,),
        in_specs=[row(D_MODEL), row(GDN_V), ym_spec, _full(wmix.shape),
                  _full(g_mem.shape), _full(wq.shape), _full(g_qn.shape)],
        out_specs=[row(D_MODEL), row(MEM_DIM)],
        out_shape=[jax.ShapeDtypeStruct((t, D_MODEL), F32), jax.ShapeDtypeStruct((t, MEM_DIM), BF16)],
        compiler_params=_cparams(("parallel",)),
        name="mix_q",
    )(x, yg, ym, wmix, g_mem, wq, g_qn)


def _mem_attn_body(q_ref, k_ref, v_ref, o_ref):
    q = q_ref[0]
    heads = range(MEM_HEADS)
    sls = [slice(h * MEM_HD, (h + 1) * MEM_HD) for h in heads]
    s = [_dot_nt(q[:, sls[h]], k_ref[0, :, sls[h]].astype(BF16)) for h in heads]
    p = [jnp.exp(s[h] - jnp.max(s[h], axis=-1, keepdims=True)) for h in heads]
    o = [_dot(p[h].astype(BF16), v_ref[0, :, sls[h]].astype(BF16)) for h in heads]
    for h in heads:
        o_ref[0, :, sls[h]] = (o[h] / jnp.sum(p[h], axis=-1, keepdims=True)).astype(BF16)


def _mem_attn(q, k, v, tq):
    nb, t, _ = q.shape
    m = k.shape[1]
    return pl.pallas_call(
        _mem_attn_body,
        grid=(nb, t // tq),
        in_specs=[pl.BlockSpec((1, tq, MEM_DIM), lambda i, j: (i, j, 0)),
                  pl.BlockSpec((1, m, MEM_DIM), lambda i, j: (i, 0, 0)),
                  pl.BlockSpec((1, m, MEM_DIM), lambda i, j: (i, 0, 0))],
        out_specs=pl.BlockSpec((1, tq, MEM_DIM), lambda i, j: (i, j, 0)),
        out_shape=jax.ShapeDtypeStruct((nb, t, MEM_DIM), BF16),
        compiler_params=_cparams(("parallel", "arbitrary")),
        name="mem_attn",
    )(q, k, v)


def _mem_attn_dec_body(q_ref, k_ref, v_ref, o_ref):
    for i in range(q_ref.shape[0]):
        q = q_ref[i:i + 1]
        s = jnp.sum(k_ref[i] * q, axis=-1, keepdims=True)
        p = jnp.exp(s - jnp.max(s, axis=0, keepdims=True))
        o_ref[i:i + 1] = (jnp.sum(p * v_ref[i], axis=0, keepdims=True)
                          / jnp.sum(p, axis=0, keepdims=True))


MEM_DEC_SEQS = 4


def _mem_attn_dec(q, k, v):
    n, m, nh, hd = k.shape
    ns = MEM_DEC_SEQS
    kv_spec = pl.BlockSpec((ns, m, nh, hd), lambda i: (i, 0, 0, 0))
    return pl.pallas_call(
        _mem_attn_dec_body,
        grid=(n // ns,),
        in_specs=[pl.BlockSpec((ns, nh, hd), lambda i: (i, 0, 0)), kv_spec, kv_spec],
        out_specs=pl.BlockSpec((ns, nh, hd), lambda i: (i, 0, 0)),
        out_shape=jax.ShapeDtypeStruct((n, nh, hd), F32),
        compiler_params=_cparams(("parallel",)),
        name="mem_attn_dec",
    )(q, k, v)


def _out_ffn_body(x1_ref, o_ref, wo_ref, g_ffn_ref, wg_ref, wu_ref, wd_ref, y_ref):
    x2 = x1_ref[...] + _dot(o_ref[...], wo_ref[...])
    h = _rms(x2, g_ffn_ref[...]).astype(BF16)
    act = (_silu(_dot(h, wg_ref[...])) * _dot(h, wu_ref[...])).astype(BF16)
    y_ref[...] = x2 + _dot(act, wd_ref[...])


def _out_ffn(x1, o, wo, g_ffn, wg, wu, wd, tm):
    t = x1.shape[0]
    row = lambda n: pl.BlockSpec((tm, n), lambda i: (i, 0))
    const = lambda a: pl.BlockSpec(a.shape, lambda i: (0,) * a.ndim, pipeline_mode=pl.Buffered(1))
    return pl.pallas_call(
        _out_ffn_body,
        grid=(t // tm,),
        in_specs=[row(D_MODEL), row(MEM_DIM), const(wo), const(g_ffn), const(wg), const(wu), const(wd)],
        out_specs=row(D_MODEL),
        out_shape=jax.ShapeDtypeStruct((t, D_MODEL), F32),
        compiler_params=_cparams(("parallel",)),
        name="out_ffn",
    )(x1, o, wo, g_ffn, wg, wu, wd)


def _mem_kv_body(mem_ref, wk_ref, wv_ref, g_ref, k_ref, v_ref):
    mb = mem_ref[...].astype(BF16)
    k = _dot(mb, wk_ref[...])
    g = g_ref[...]
    for h in range(MEM_HEADS):
        sl = slice(h * MEM_HD, (h + 1) * MEM_HD)
        k_ref[:, sl] = _rms(k[:, sl], g)
    v_ref[...] = _dot(mb, wv_ref[...])


def _mem_kv(mem, wk, wv, g, tm):
    t = mem.shape[0]
    row = lambda n: pl.BlockSpec((tm, n), lambda i: (i, 0))
    return pl.pallas_call(
        _mem_kv_body,
        grid=(t // tm,),
        in_specs=[row(D_MODEL), _full(wk.shape), _full(wv.shape), _full(g.shape)],
        out_specs=[row(MEM_DIM), row(MEM_DIM)],
        out_shape=[jax.ShapeDtypeStruct((t, MEM_DIM), F32)] * 2,
        compiler_params=_cparams(("parallel",)),
        name="mem_kv",
    )(mem, wk, wv, g)


def _row(v):
    return v.reshape(1, -1).astype(F32)


def _pad_lanes(v, width=LANE):
    return jnp.pad(v, ((0, 0), (0, width - v.shape[1])))


def _rope_tables(pos, rows):
    half = QK_ROPE // 2
    inv = ROPE_THETA ** (-jnp.arange(half, dtype=F32) / half)
    ang = pos.astype(F32)[:, None] * inv[None, :]
    cos, sin = jnp.cos(ang), jnp.sin(ang)
    zero = jnp.zeros_like(cos)
    tabs = jnp.stack([
        _pad_lanes(jnp.concatenate([cos, cos], axis=1)),
        _pad_lanes(jnp.concatenate([-sin, zero], axis=1)),
        _pad_lanes(jnp.concatenate([zero, sin], axis=1)),
    ])
    return jnp.broadcast_to(tabs, (3, rows, LANE)) if tabs.shape[1] == 1 else tabs


def kernel(x_prompt, x_sample, cache_mla_ckv, cache_mla_kpe, cache_mem_k, cache_mem_v, state_gdn_S, state_gdn_conv, page_table, mem_prompt, norm_mix_g, w_in, gdn_conv_w, gdn_A_log, gdn_dt_bias, gdn_out_norm_g, mla_q_a_norm_g, mla_w_q_b, mla_kv_a_norm_g, mla_w_kv_b, mla_qn_nope_g, mla_qn_rope_g, mla_kn_nope_g, mla_kn_rope_g, w_mix_out, norm_mem_g, mem_wq, mem_wk, mem_wv, mem_wo, mem_qn_g, mem_kn_g, norm_ffn_g, ffn_w_gate, ffn_w_up, ffn_w_down):
    depth = w_in.shape[0]
    assert depth == 1, "single-layer trunk"
    bsz, seq, _ = x_prompt.shape
    nseq, dseq, _ = x_sample.shape
    assert dseq == 1, "one new token per decode sequence"
    past_len = page_table.shape[1] * PAGE_SIZE
    n_tok = bsz * seq

    w = w_in[0]
    o_gz = CONV_DIM
    o_b = o_gz + GDN_V
    o_a = o_b + GDN_HEADS
    o_qa = o_a + GDN_HEADS
    o_c = o_qa + Q_LORA
    o_kpe = o_c + KV_LORA
    misc_w = _pad_lanes(jnp.concatenate([w[:, o_kpe:o_kpe + QK_ROPE], w[:, o_b:o_qa]], axis=1))
    w_in_p = jnp.concatenate([w[:, :o_b], w[:, o_qa:o_kpe], misc_w], axis=1).astype(BF16)

    wqb = mla_w_q_b[0].reshape(Q_LORA, MLA_HEADS, QK_NOPE + QK_ROPE)
    wqb = jnp.pad(wqb, ((0, 0), (0, 0), (0, QK_PAD - QK_NOPE - QK_ROPE)))
    wqb = wqb.reshape(Q_LORA, MLA_HEADS * QK_PAD).astype(BF16)
    wkvb = mla_w_kv_b[0].reshape(KV_LORA, MLA_HEADS, QK_NOPE + V_HEAD)
    wk = wkvb[:, :, :QK_NOPE].reshape(KV_LORA, MLA_HEADS * QK_NOPE).astype(BF16)
    wv = wkvb[:, :, QK_NOPE:].reshape(KV_LORA, MLA_HEADS * V_HEAD).astype(BF16)
    wkt = wk.T

    lanep = jnp.zeros((2, LANE), F32)
    lanep = lanep.at[0, MISC_A:MISC_A + GDN_HEADS].set(gdn_A_log[0])
    lanep = lanep.at[1, MISC_A:MISC_A + GDN_HEADS].set(gdn_dt_bias[0])
    rowp = jnp.zeros((2 * SUBLANE,), F32)
    rowp = rowp.at[GDN_HEADS:2 * GDN_HEADS].set(gdn_A_log[0])
    rowp = rowp.at[SUBLANE + GDN_HEADS:SUBLANE + 2 * GDN_HEADS].set(gdn_dt_bias[0])
    rowp = jnp.broadcast_to(rowp[:, None], (2 * SUBLANE, GDN_TB))

    g_mix = _row(norm_mix_g[0])
    g_out = _row(gdn_out_norm_g[0])
    mla_gains = (_row(mla_q_a_norm_g[0]), _row(mla_qn_nope_g[0]), _pad_lanes(_row(mla_qn_rope_g[0])),
                 _row(mla_kv_a_norm_g[0]), _row(mla_kn_nope_g[0]), _pad_lanes(_row(mla_kn_rope_g[0])))
    wmix = w_mix_out[0].astype(BF16)
    wq_mem = mem_wq[0].astype(BF16)
    wk_mem = mem_wk[0].astype(BF16)
    wv_mem = mem_wv[0].astype(BF16)
    wo_mem = mem_wo[0].astype(BF16)
    wg = ffn_w_gate[0].astype(BF16)
    wu = ffn_w_up[0].astype(BF16)
    wd = ffn_w_down[0].astype(BF16)
    g_mem = _row(norm_mem_g[0])
    g_ffn = _row(norm_ffn_g[0])
    g_qn = _row(mem_qn_g[0])
    g_kn = _row(mem_kn_g[0])
    conv_w = gdn_conv_w[0]

    xp = x_prompt.reshape(n_tok, D_MODEL)
    tiles_per_seq = seq // TOKEN_TILE
    qkv_act, qkv_tail, gz, qa, kvc, misc = _in_proj_conv(xp, g_mix, w_in_p, conv_w, TOKEN_TILE,
                                                         tiles_per_seq)
    grow = jnp.swapaxes(misc.reshape(bsz, seq, LANE)[:, :, MISC_B:MISC_B + SUBLANE], 1, 2)
    y_gdn, p_s = _gdn_prompt(qkv_act.reshape(bsz, seq, CONV_DIM), misc.reshape(bsz, seq, LANE), grow,
                             gz.reshape(bsz, seq, GDN_V), lanep, rowp, g_out)
    p_conv = qkv_tail.reshape(bsz, tiles_per_seq, SUBLANE, CONV_DIM)[:, -1, SUBLANE - (CONV_W - 1):, :]

    tabs_p = _rope_tables(jnp.arange(seq), seq)
    q_full, k_full, vt_full, p_c, p_kpe = _mla_prep(qa, kvc, misc, tabs_p, wqb, wk, wv.T, mla_gains,
                                                    TOKEN_TILE)
    y_mla_t = _mla_flash(q_full, k_full, vt_full, bsz, seq)

    mem_k, mem_v = _mem_kv(mem_prompt.reshape(-1, D_MODEL), wk_mem, wv_mem, g_kn, TOKEN_TILE)
    n_mem = mem_prompt.shape[1]
    x1, q_mem = _mix_q(xp, y_gdn.reshape(n_tok, GDN_V), y_mla_t, wmix, g_mem, wq_mem, g_qn, MIX_TILE,
                       mla_transposed=True)
    o_mem = _mem_attn(q_mem.reshape(bsz, seq, MEM_DIM), mem_k.reshape(bsz, n_mem, MEM_DIM),
                      mem_v.reshape(bsz, n_mem, MEM_DIM), TOKEN_TILE)
    y_prompt = _out_ffn(x1, o_mem.reshape(n_tok, MEM_DIM), wo_mem, g_ffn, wg, wu, wd, TOKEN_TILE)

    xs = x_sample.reshape(nseq, D_MODEL)
    qkv_s, gz_s, qa_s, kvc_s, misc_s = _in_proj(xs, g_mix, w_in_p, nseq)
    conv_prev = state_gdn_conv[0]
    y_gdn_s, s_new = _gdn_decode(qkv_s, jnp.swapaxes(conv_prev, 0, 1), misc_s, gz_s, state_gdn_S[0],
                                 conv_w, lanep, g_out)
    s_conv = jnp.concatenate([conv_prev[:, 1:, :], qkv_s[:, None, :]], axis=1)

    tabs_s = _rope_tables(jnp.full((1,), past_len), nseq)
    qabs, qpe, c_new, kpe_new = _mla_dec_prep(qa_s, kvc_s, misc_s, tabs_s, wqb, wk, mla_gains)
    pad_rows = lambda a: jnp.pad(jnp.swapaxes(a, 0, 1), ((0, 0), (0, BF16_ROWS - MLA_HEADS), (0, 0)))
    m_run, l_run, acc_run = _mla_dec_attn(
        page_table, cache_mla_ckv[0], jnp.swapaxes(cache_mla_kpe[0], 1, 2),
        pad_rows(qabs).astype(BF16), pad_rows(qpe[:, :, :QK_ROPE]).astype(BF16), wkt)
    heads_first = lambda a: jnp.swapaxes(a[:, :MLA_HEADS], 0, 1)
    y_mla_s = _mla_dec_finish(heads_first(m_run), heads_first(l_run), heads_first(acc_run),
                              qabs, qpe, c_new, kpe_new, wk, wv)

    x1_s, q_mem_s = _mix_q(xs, y_gdn_s, y_mla_s, wmix, g_mem, wq_mem, g_qn, nseq)
    o_mem_s = _mem_attn_dec(q_mem_s.astype(F32).reshape(nseq, MEM_HEADS, MEM_HD),
                            cache_mem_k[0], cache_mem_v[0])
    y_sample = _out_ffn(x1_s, o_mem_s.reshape(nseq, MEM_DIM).astype(BF16),
                        wo_mem, g_ffn, wg, wu, wd, nseq)

    return (
        y_prompt.reshape(bsz, seq, D_MODEL),
        y_sample.reshape(nseq, 1, D_MODEL),
        p_c.reshape(1, bsz, seq, KV_LORA),
        p_kpe.reshape(1, bsz, seq, QK_ROPE),
        mem_k.reshape(1, bsz, n_mem, MEM_HEADS, MEM_HD),
        mem_v.reshape(1, bsz, n_mem, MEM_HEADS, MEM_HD),
        p_s[None],
        p_conv[None],
        c_new.reshape(1, nseq, 1, KV_LORA),
        kpe_new[:, :QK_ROPE].reshape(1, nseq, 1, QK_ROPE),
        s_new[None],
        s_conv[None],
    )
```

```python
import functools

import jax
import jax.numpy as jnp
from jax import lax
from jax.experimental import pallas as pl
from jax.experimental.pallas import tpu as pltpu

F32 = jnp.float32
BF16 = jnp.bfloat16

D_MODEL = 1024
GDN_HEADS = 4
GDN_DK = 128
GDN_DV = 128
CONV_W = 4
CHUNK = 64
MLA_HEADS = 4
Q_LORA = 384
KV_LORA = 256
QK_NOPE = 128
QK_ROPE = 64
V_HEAD = 128
ROPE_THETA = 10000.0
PAGE_SIZE = 128
MEM_HEADS = 4
MEM_HD = 128
EPS = 1e-6
GDN_QK = GDN_HEADS * GDN_DK
GDN_V = GDN_HEADS * GDN_DV
CONV_DIM = 2 * GDN_QK + GDN_V
MLA_SCALE = (QK_NOPE + QK_ROPE) ** -0.5
MEM_DIM = MEM_HEADS * MEM_HD

LANE = 128
SUBLANE = 8
BF16_ROWS = 16
QK_PAD = 2 * LANE
MISC_B = QK_ROPE
MISC_A = QK_ROPE + GDN_HEADS
VMEM_LIMIT = 56 * 1024 * 1024

TOKEN_TILE = 512
MIX_TILE = 1024
GDN_TILE = 2 * CHUNK
GDN_TB = 4 * GDN_TILE
PAGES_PER_STEP = 32
DEC_RING = 4


def _cparams(sem):
    return pltpu.CompilerParams(dimension_semantics=sem, vmem_limit_bytes=VMEM_LIMIT)


def _full(shape):
    n = len(shape)
    return pl.BlockSpec(shape, lambda *_: (0,) * n)


def _dot(a, b):
    return jnp.dot(a, b, preferred_element_type=F32)


def _dot_nt(a, b):
    return lax.dot_general(a, b, (((1,), (1,)), ((), ())), preferred_element_type=F32)


def _dot_tn(a, b):
    return lax.dot_general(a, b, (((0,), (0,)), ((), ())), preferred_element_type=F32)


def _split3(x):
    hi = x.astype(BF16)
    r = x - hi.astype(F32)
    mid = r.astype(BF16)
    lo = (r - mid.astype(F32)).astype(BF16)
    return hi, mid, lo


def _dot01_left(m01, x):
    hi, mid, lo = _split3(x)
    return _dot(m01, hi) + _dot(m01, mid) + _dot(m01, lo)


def _dot01_right(x, m01):
    hi, mid, lo = _split3(x)
    return _dot(hi, m01) + _dot(mid, m01) + _dot(lo, m01)


def _rms(x, g, n=None):
    n = x.shape[-1] if n is None else n
    ms = jnp.sum(x * x, axis=-1, keepdims=True) * (1.0 / n)
    return x * lax.rsqrt(ms + EPS) * g


def _silu(x):
    return x * jax.nn.sigmoid(x)


def _softplus(x):
    return jnp.maximum(x, 0.0) + jnp.log1p(jnp.exp(-jnp.abs(x)))


def _iota(shape, dim):
    return lax.broadcasted_iota(jnp.int32, shape, dim)


def _block_id(i, size):
    assert size & (size - 1) == 0
    return lax.shift_right_logical(i, size.bit_length() - 1)


def _in_proj_body(x_ref, g_ref, w_ref, qkv_ref, gz_ref, qa_ref, kvc_ref, misc_ref):
    h = _rms(x_ref[...], g_ref[...]).astype(BF16)
    off = 0
    for ref in (qkv_ref, gz_ref, qa_ref, kvc_ref, misc_ref):
        n = ref.shape[-1]
        ref[...] = _dot(h, w_ref[:, off:off + n])
        off += n


def _in_proj(x, g, w, tm):
    t = x.shape[0]
    widths = (CONV_DIM, GDN_V, Q_LORA, KV_LORA, LANE)
    return pl.pallas_call(
        _in_proj_body,
        grid=(t // tm,),
        in_specs=[pl.BlockSpec((tm, D_MODEL), lambda i: (i, 0)), _full(g.shape), _full(w.shape)],
        out_specs=[pl.BlockSpec((tm, n), lambda i: (i, 0)) for n in widths],
        out_shape=[jax.ShapeDtypeStruct((t, n), F32) for n in widths],
        compiler_params=_cparams(("parallel",)),
        name="in_proj",
    )(x, g, w)


def _qk_l2norm(x, scale):
    return x * (lax.rsqrt(jnp.sum(x * x, axis=-1, keepdims=True) + EPS) * scale)


def _in_proj_conv_body(x_ref, g_ref, w_ref, convw_ref, qkv_ref, tail_ref, gz_ref, qa_ref, kvc_ref,
                       misc_ref, xe_ref, *, tiles_per_seq):
    tm = x_ref.shape[0]
    @pl.when(pl.program_id(0) % tiles_per_seq == 0)
    def _():
        xe_ref[0:SUBLANE, :] = jnp.zeros((SUBLANE, CONV_DIM), F32)

    h = _rms(x_ref[...], g_ref[...]).astype(BF16)
    w = convw_ref[...]
    pair = 2 * GDN_DK
    blocks = [slice(blk * pair, (blk + 1) * pair) for blk in range(CONV_DIM // pair)]
    for sl in blocks:
        xe_ref[SUBLANE:SUBLANE + tm, sl] = _dot(h, w_ref[:, sl])
    off = CONV_DIM
    for ref in (gz_ref, qa_ref, kvc_ref, misc_ref):
        n = ref.shape[-1]
        ref[...] = _dot(h, w_ref[:, off:off + n])
        off += n
    for blk, sl in enumerate(blocks):
        conv = xe_ref[SUBLANE:SUBLANE + tm, sl] * w[CONV_W - 1:CONV_W, sl]
        for i in range(CONV_W - 1):
            lo = SUBLANE - (CONV_W - 1) + i
            conv = conv + xe_ref[lo:lo + tm, sl] * w[i:i + 1, sl]
        act = _silu(conv)
        for half in range(2):
            head_blk = 2 * blk + half
            hs = slice(half * GDN_DK, (half + 1) * GDN_DK)
            out_sl = slice(head_blk * GDN_DK, (head_blk + 1) * GDN_DK)
            if head_blk < GDN_HEADS:
                qkv_ref[:, out_sl] = _qk_l2norm(act[:, hs], GDN_DK ** -0.5)
            elif head_blk < 2 * GDN_HEADS:
                qkv_ref[:, out_sl] = _qk_l2norm(act[:, hs], 1.0)
            else:
                qkv_ref[:, out_sl] = act[:, hs]
    tail = xe_ref[tm:SUBLANE + tm, :]
    tail_ref[...] = tail
    xe_ref[0:SUBLANE, :] = tail


def _in_proj_conv(x, g, w, convw, tm, tiles_per_seq):
    t = x.shape[0]
    widths = (CONV_DIM, CONV_DIM, GDN_V, Q_LORA, KV_LORA, LANE)
    rows = (tm, SUBLANE, tm, tm, tm, tm)
    return pl.pallas_call(
        functools.partial(_in_proj_conv_body, tiles_per_seq=tiles_per_seq),
        grid=(t // tm,),
        in_specs=[pl.BlockSpec((tm, D_MODEL), lambda i: (i, 0)), _full(g.shape), _full(w.shape),
                  _full(convw.shape)],
        out_specs=[pl.BlockSpec((r, n), lambda i: (i, 0)) for r, n in zip(rows, widths)],
        out_shape=[jax.ShapeDtypeStruct((t // tm * r, n), F32) for r, n in zip(rows, widths)],
        scratch_shapes=[pltpu.VMEM((SUBLANE + tm, CONV_DIM), F32)],
        compiler_params=_cparams(("arbitrary",)),
        name="in_proj_conv",
    )(x, g, w, convw)


def _gate_lane_values(misc, lanep):
    lane = _iota(misc.shape, 1)
    is_a = (lane >= MISC_A) & (lane < MISC_A + GDN_HEADS)
    beta_l = jax.nn.sigmoid(misc)
    g_l = -jnp.exp(lanep[0:1, :]) * _softplus(jnp.where(is_a, misc, 0.0) + lanep[1:2, :])
    return beta_l, g_l


def _expand_matrix(first_lane):
    r = _iota((LANE, GDN_V), 0)
    c = _iota((LANE, GDN_V), 1)
    return jnp.where(r == first_lane + _block_id(c, GDN_DV), 1.0, 0.0).astype(BF16)


def _gdn_gate_out(o, z, gout):
    return _rms(o, gout) * _silu(z)


def _tri_inverse(a_list, eye, same16, same32):
    def mm(xs, ys):
        return [_dot(x.astype(BF16), y.astype(BF16)) for x, y in zip(xs, ys)]

    def add(ts, us):
        return [t + u for t, u in zip(ts, us)]

    def sub(ts, us):
        return [t - u for t, u in zip(ts, us)]

    n = eye.shape[0]
    ad = [jnp.where(same16, a, 0.0) for a in a_list]
    t = [eye - x for x in ad]
    p = mm(ad, ad)
    for _ in range(2):
        tp = mm([jnp.concatenate([x, y], axis=0) for x, y in zip(t, p)], p)
        t = add(t, [x[:n] for x in tp])
        p = [x[n:] for x in tp]
    t = add(t, mm(t, p))
    b1 = [jnp.where(same32, a, 0.0) - x for a, x in zip(a_list, ad)]
    t = sub(t, mm(t, mm(b1, t)))
    b2 = [jnp.where(same32, 0.0, a) for a in a_list]
    t = sub(t, mm(t, mm(b2, t)))
    return t


def _gdn_prompt_body(qkv_ref, misc_ref, grow_ref, gz_ref, lanep_ref, rowp_ref, gout_ref,
                     y_ref, s_out_ref, s_ref):
    tb = GDN_TB
    t_idx = pl.program_id(1)

    @pl.when(t_idx == 0)
    def _():
        s_ref[...] = jnp.zeros(s_ref.shape, F32)

    tile = GDN_TILE
    tiles = range(tb // tile)
    rows = [slice(t * tile, (t + 1) * tile) for t in tiles]
    ri = _iota((tile, tile), 0)
    ci = _iota((tile, tile), 1)
    same64 = _block_id(ri, CHUNK) == _block_id(ci, CHUNK)
    lower = same64 & (ci <= ri)
    strict = same64 & (ci < ri)
    same16 = _block_id(ri, 16) == _block_id(ci, 16)
    same32 = _block_id(ri, 32) == _block_id(ci, 32)
    eye = jnp.where(ri == ci, 1.0, 0.0)
    l_incl = jnp.where(lower, 1.0, 0.0).astype(BF16)
    u_incl = jnp.where(same64 & (ri <= ci), 1.0, 0.0).astype(BF16)
    u_strict = jnp.where(same64 & (ci > ri), 1.0, 0.0).astype(BF16)

    beta_l, g_l = _gate_lane_values(misc_ref[0], lanep_ref[...])
    e_b = _expand_matrix(MISC_B)
    e_a = _expand_matrix(MISC_A)
    beta_bc = _dot01_right(beta_l, e_b)
    gc_col = jnp.concatenate([_dot01_left(l_incl, g_l[r]) for r in rows], axis=0)
    gt_col = jnp.concatenate([_dot01_left(u_strict, g_l[r]) for r in rows], axis=0)
    gc_bc = _dot01_right(gc_col, e_a)
    gt_bc = _dot01_right(gt_col, e_a)
    egc = jnp.exp(gc_bc)
    etail = jnp.exp(gt_bc)

    grow = grow_ref[0]
    rowi = _iota(grow.shape, 0)
    g_r = -jnp.exp(rowp_ref[0:SUBLANE, :]) * _softplus(
        jnp.where(rowi >= GDN_HEADS, grow, 0.0) + rowp_ref[SUBLANE:2 * SUBLANE, :])
    gcr = [_dot01_right(g_r[:, r], u_incl) for r in rows]

    gout = gout_ref[...]
    gz = gz_ref[0]
    heads = range(GDN_HEADS)
    sls = [slice(h * GDN_DK, (h + 1) * GDN_DK) for h in heads]
    units = [(t, h) for t in tiles for h in heads]
    qh, kh, kb, vb, dm = {}, {}, {}, {}, {}
    for h in heads:
        q = qkv_ref[0, :, h * GDN_DK:(h + 1) * GDN_DK]
        k = qkv_ref[0, :, GDN_QK + h * GDN_DK:GDN_QK + (h + 1) * GDN_DK]
        v = qkv_ref[0, :, 2 * GDN_QK + h * GDN_DV:2 * GDN_QK + (h + 1) * GDN_DV]
        bb = beta_bc[:, sls[h]]
        for t in tiles:
            qh[t, h] = q[rows[t]]
            kh[t, h] = k[rows[t]]
            kb[t, h] = (k * bb)[rows[t]]
            vb[t, h] = (v * bb)[rows[t]]
            diff = gc_bc[rows[t], sls[h]] - gcr[t][GDN_HEADS + h:GDN_HEADS + h + 1, :]
            dm[t, h] = jnp.where(lower, jnp.exp(jnp.where(lower, diff, 0.0)), 0.0)
    p = {u: _dot_nt(jnp.concatenate([kb[u], qh[u]], axis=0).astype(BF16), kh[u].astype(BF16))
         for u in units}
    a_mat = [jnp.where(strict, p[u][:tile] * dm[u], 0.0) for u in units]
    attn = {u: (p[u][tile:] * dm[u]).astype(BF16) for u in units}
    t_mat = dict(zip(units, _tri_inverse(a_mat, eye, same16, same32)))
    egc_u = {(t, h): egc[rows[t], sls[h]] for t, h in units}
    uw = {u: _dot(t_mat[u].astype(BF16),
                  jnp.concatenate([vb[u], kb[u] * egc_u[u]], axis=1).astype(BF16)) for u in units}
    qd = {u: qh[u] * egc_u[u] for u in units}
    kt = {(t, h): (kh[t, h] * etail[rows[t], sls[h]]).astype(BF16) for t, h in units}
    s = [s_ref[h] for h in heads]
    o_inter = {u: [] for u in units}
    v_new = {u: [] for u in units}
    for t in tiles:
        for c in range(tile // CHUNK):
            rs = slice(c * CHUNK, (c + 1) * CHUNK)
            last = t * tile + (c + 1) * CHUNK - 1
            wq = [_dot(jnp.concatenate([uw[t, h][rs, GDN_DV:], qd[t, h][rs]], axis=0).astype(BF16),
                       s[h].astype(BF16)) for h in heads]
            vn = [uw[t, h][rs, :GDN_DV] - wq[h][:CHUNK] for h in heads]
            egl = [jnp.exp(gc_bc[last:last + 1, sls[h]]) for h in heads]
            s = [s[h] * egl[h] + _dot_tn(kt[t, h][rs], vn[h].astype(BF16)) for h in heads]
            for h in heads:
                o_inter[t, h].append(wq[h][CHUNK:])
                v_new[t, h].append(vn[h])
    o = {u: jnp.concatenate(o_inter[u], axis=0)
         + _dot(attn[u], jnp.concatenate(v_new[u], axis=0).astype(BF16)) for u in units}
    for h in heads:
        s_ref[h] = s[h]
        for t in tiles:
            y_ref[0, rows[t], sls[h]] = _gdn_gate_out(o[t, h], gz[rows[t], sls[h]], gout).astype(BF16)

    @pl.when(t_idx == pl.num_programs(1) - 1)
    def _():
        s_out_ref[0] = s_ref[...]


def _gdn_prompt(qkv, misc, grow, gz, lanep, rowp, gout):
    b, t, _ = qkv.shape
    tb = GDN_TB
    return pl.pallas_call(
        _gdn_prompt_body,
        grid=(b, t // tb),
        in_specs=[
            pl.BlockSpec((1, tb, CONV_DIM), lambda i, j: (i, j, 0)),
            pl.BlockSpec((1, tb, LANE), lambda i, j: (i, j, 0)),
            pl.BlockSpec((1, SUBLANE, tb), lambda i, j: (i, 0, j)),
            pl.BlockSpec((1, tb, GDN_V), lambda i, j: (i, j, 0)),
            _full(lanep.shape), _full(rowp.shape), _full(gout.shape),
        ],
        out_specs=[
            pl.BlockSpec((1, tb, GDN_V), lambda i, j: (i, j, 0)),
            pl.BlockSpec((1, GDN_HEADS, GDN_DK, GDN_DV), lambda i, j: (i, 0, 0, 0)),
        ],
        out_shape=[
            jax.ShapeDtypeStruct((b, t, GDN_V), BF16),
            jax.ShapeDtypeStruct((b, GDN_HEADS, GDN_DK, GDN_DV), F32),
        ],
        scratch_shapes=[pltpu.VMEM((GDN_HEADS, GDN_DK, GDN_DV), F32)],
        compiler_params=_cparams(("parallel", "arbitrary")),
        name="gdn_prompt",
    )(qkv, misc, grow, gz, lanep, rowp, gout)


GDN_DEC_SEQS = 8


def _gdn_decode_body(qkv_ref, cprev_ref, misc_ref, gz_ref, s_in_ref, convw_ref, lanep_ref, gout_ref,
                     y_ref, s_out_ref, o_ref):
    ns = GDN_DEC_SEQS
    w = convw_ref[...]
    conv = qkv_ref[...] * w[CONV_W - 1:CONV_W, :]
    for i in range(CONV_W - 1):
        conv = conv + cprev_ref[i] * w[i:i + 1, :]
    conv = _silu(conv)
    beta_l, g_l = _gate_lane_values(misc_ref[...], lanep_ref[...])
    beta_bc = _dot01_right(beta_l, _expand_matrix(MISC_B))
    eg = jnp.exp(_dot01_right(g_l, _expand_matrix(MISC_A)))
    row0 = jnp.where(_iota((SUBLANE, LANE), 0) == 0, 1.0, 0.0).astype(BF16)
    units = [(h, i) for h in range(GDN_HEADS) for i in range(ns)]
    qn, kn, vv = {}, {}, {}
    for h in range(GDN_HEADS):
        qh = conv[:, h * GDN_DK:(h + 1) * GDN_DK]
        kh = conv[:, GDN_QK + h * GDN_DK:GDN_QK + (h + 1) * GDN_DK]
        qn[h] = qh * lax.rsqrt(jnp.sum(qh * qh, axis=-1, keepdims=True) + EPS) * (GDN_DK ** -0.5)
        kn[h] = kh * lax.rsqrt(jnp.sum(kh * kh, axis=-1, keepdims=True) + EPS)
        vv[h] = conv[:, 2 * GDN_QK + h * GDN_DV:2 * GDN_QK + (h + 1) * GDN_DV]
    sl = lambda h: slice(h * GDN_DK, (h + 1) * GDN_DK)
    k8 = {u: jnp.broadcast_to(kn[u[0]][u[1]:u[1] + 1, :], (SUBLANE, GDN_DK)) for u in units}
    s_dec = {(h, i): s_in_ref[i, h] * eg[i:i + 1, sl(h)] for h, i in units}
    kv = {u: _dot(k8[u].astype(BF16), s_dec[u].astype(BF16))[0:1, :] for u in units}
    kcol = {}
    for u in units:
        k_hi, k_mid, k_lo = _split3(k8[u])
        kcol[u] = _dot_tn(k_hi, row0) + _dot_tn(k_mid, row0) + _dot_tn(k_lo, row0)
    s_new = {}
    for h, i in units:
        delta = (vv[h][i:i + 1, :] - kv[h, i]) * beta_bc[i:i + 1, sl(h)]
        s_new[h, i] = s_dec[h, i] + kcol[h, i] * delta
        s_out_ref[i, h] = s_new[h, i]
    for h, i in units:
        q8 = jnp.broadcast_to(qn[h][i:i + 1, :], (SUBLANE, GDN_DK))
        o_ref[i:i + 1, sl(h)] = _dot(q8.astype(BF16), s_new[h, i].astype(BF16))[0:1, :]
    gout = gout_ref[...]
    gz = gz_ref[...]
    for h in range(GDN_HEADS):
        sl = slice(h * GDN_DV, (h + 1) * GDN_DV)
        y_ref[:, sl] = _gdn_gate_out(o_ref[:, sl], gz[:, sl], gout).astype(BF16)


def _gdn_decode(qkv, cprev, misc, gz, s_in, convw, lanep, gout):
    n = qkv.shape[0]
    ns = GDN_DEC_SEQS
    state_spec = pl.BlockSpec((ns, GDN_HEADS, GDN_DK, GDN_DV), lambda i: (i, 0, 0, 0))
    return pl.pallas_call(
        _gdn_decode_body,
        grid=(n // ns,),
        in_specs=[
            pl.BlockSpec((ns, CONV_DIM), lambda i: (i, 0)),
            pl.BlockSpec((CONV_W - 1, ns, CONV_DIM), lambda i: (0, i, 0)),
            pl.BlockSpec((ns, LANE), lambda i: (i, 0)),
            pl.BlockSpec((ns, GDN_V), lambda i: (i, 0)),
            state_spec,
            _full(convw.shape), _full(lanep.shape), _full(gout.shape),
        ],
        out_specs=[pl.BlockSpec((ns, GDN_V), lambda i: (i, 0)), state_spec],
        out_shape=[
            jax.ShapeDtypeStruct((n, GDN_V), BF16),
            jax.ShapeDtypeStruct(s_in.shape, F32),
        ],
        scratch_shapes=[pltpu.VMEM((ns, GDN_V), F32)],
        compiler_params=_cparams(("parallel",)),
        name="gdn_decode",
    )(qkv, cprev, misc, gz, s_in, convw, lanep, gout)


def _rope_padded(x, cos, sina, sinb):
    return (x * cos + pltpu.roll(x, LANE - QK_ROPE // 2, axis=1) * sina
            + pltpu.roll(x, QK_ROPE // 2, axis=1) * sinb)


def _mla_queries(qa, wqb_ref, g_qa, g_nope, g_rope, cos, sina, sinb):
    q = _dot(_rms(qa, g_qa).astype(BF16), wqb_ref[...])
    out = []
    for h in range(MLA_HEADS):
        qn = _rms(q[:, h * QK_PAD:h * QK_PAD + QK_NOPE], g_nope)
        qp = _rms(q[:, h * QK_PAD + QK_NOPE:(h + 1) * QK_PAD], g_rope, QK_ROPE)
        out.append((qn, _rope_padded(qp, cos, sina, sinb)))
    return out


def _mla_latent(kvc, misc, g_kva, g_krope, cos, sina, sinb):
    c = _rms(kvc, g_kva)
    lane = _iota(misc.shape, 1)
    kp = _rms(jnp.where(lane < QK_ROPE, misc, 0.0), g_krope, QK_ROPE)
    return c, _rope_padded(kp, cos, sina, sinb)


def _mla_prep_body(qa_ref, kvc_ref, misc_ref, tab_ref, wqb_ref, wk_ref, wvt_ref,
                   g_qa_ref, g_nope_ref, g_rope_ref, g_kva_ref, g_knope_ref, g_krope_ref,
                   q_ref, k_ref, vt_ref, c_ref, kpe_ref):
    cos, sina, sinb = tab_ref[0], tab_ref[1], tab_ref[2]
    qs = _mla_queries(qa_ref[...], wqb_ref, g_qa_ref[...], g_nope_ref[...], g_rope_ref[...],
                      cos, sina, sinb)
    c, kp = _mla_latent(kvc_ref[...], misc_ref[...], g_kva_ref[...], g_krope_ref[...], cos, sina, sinb)
    c_ref[...] = c
    kpe_ref[...] = kp[:, :QK_ROPE]
    cb = c.astype(BF16)
    kn = _dot(cb, wk_ref[...])
    vt = _dot_nt(wvt_ref[...], cb)
    g_kn = g_knope_ref[...]
    kp16 = kp.astype(BF16)
    for h in range(MLA_HEADS):
        qn, qp = qs[h]
        q_ref[h, :, 0:QK_NOPE] = (qn * MLA_SCALE).astype(BF16)
        q_ref[h, :, QK_NOPE:QK_PAD] = (qp * MLA_SCALE).astype(BF16)
        k_ref[h, :, 0:QK_NOPE] = _rms(kn[:, h * QK_NOPE:(h + 1) * QK_NOPE], g_kn).astype(BF16)
        k_ref[h, :, QK_NOPE:QK_PAD] = kp16
        vt_ref[h] = vt[h * V_HEAD:(h + 1) * V_HEAD, :].astype(BF16)


def _mla_prep(qa, kvc, misc, tabs, wqb, wk, wvt, gains, tm):
    t = qa.shape[0]
    n_tab = tabs.shape[1] // tm
    row = lambda n: pl.BlockSpec((tm, n), lambda i: (i, 0))
    head = lambda n: pl.BlockSpec((MLA_HEADS, tm, n), lambda i: (0, i, 0))
    return pl.pallas_call(
        _mla_prep_body,
        grid=(t // tm,),
        in_specs=[row(Q_LORA), row(KV_LORA), row(LANE),
                  pl.BlockSpec((3, tm, LANE), lambda i: (0, i % n_tab, 0)),
                  _full(wqb.shape), _full(wk.shape), _full(wvt.shape)]
                 + [_full(g.shape) for g in gains],
        out_specs=[head(QK_PAD), head(QK_PAD),
                   pl.BlockSpec((MLA_HEADS, V_HEAD, tm), lambda i: (0, 0, i)),
                   row(KV_LORA), row(QK_ROPE)],
        out_shape=[
            jax.ShapeDtypeStruct((MLA_HEADS, t, QK_PAD), BF16),
            jax.ShapeDtypeStruct((MLA_HEADS, t, QK_PAD), BF16),
            jax.ShapeDtypeStruct((MLA_HEADS, V_HEAD, t), BF16),
            jax.ShapeDtypeStruct((t, KV_LORA), F32),
            jax.ShapeDtypeStruct((t, QK_ROPE), F32),
        ],
        compiler_params=_cparams(("parallel",)),
        name="mla_prep",
    )(qa, kvc, misc, tabs, wqb, wk, wvt, *gains)


FLASH_T = 512


FLASH_HEADS = 4


def _flash_body(q_ref, k_ref, vt_ref, o_ref):
    tq = FLASH_T
    qi = pl.program_id(2)
    heads = range(FLASH_HEADS)
    q = [q_ref[h] for h in heads]

    def block(j, carry, masked):
        start = pl.multiple_of(j * tq, tq)
        s = [_dot_nt(k_ref[h, pl.ds(start, tq), :], q[h]) for h in heads]
        if masked:
            keep = _iota(s[0].shape, 0) <= _iota(s[0].shape, 1)
            s = [jnp.where(keep, x, -jnp.inf) for x in s]
        m_new = [jnp.maximum(carry[h][0], jnp.max(s[h], axis=0, keepdims=True)) for h in heads]
        p = [jnp.exp(s[h] - m_new[h]) for h in heads]
        pv = [_dot(vt_ref[h, :, pl.ds(start, tq)], p[h].astype(BF16)) for h in heads]
        out = []
        for h in heads:
            m, l, acc = carry[h]
            corr = jnp.exp(m - m_new[h])
            out.append((m_new[h], l * corr + jnp.sum(p[h], axis=0, keepdims=True),
                        acc * corr + pv[h]))
        return tuple(out)

    init = tuple((jnp.full((1, tq), -jnp.inf, F32), jnp.zeros((1, tq), F32),
                  jnp.zeros((V_HEAD, tq), F32)) for _ in heads)
    carry = lax.fori_loop(0, qi, lambda j, c: block(j, c, False), init)
    carry = block(qi, carry, True)
    for h in heads:
        _, l, acc = carry[h]
        o_ref[h * V_HEAD:(h + 1) * V_HEAD, :] = (acc / l).astype(BF16)


def _mla_flash(q, k, vt, b, t):
    tq = FLASH_T
    nq = t // tq
    nh = FLASH_HEADS
    return pl.pallas_call(
        _flash_body,
        grid=(b, MLA_HEADS // nh, nq),
        in_specs=[
            pl.BlockSpec((nh, tq, QK_PAD), lambda i, h, j: (h, i * nq + j, 0)),
            pl.BlockSpec((nh, t, QK_PAD), lambda i, h, j: (h, i, 0)),
            pl.BlockSpec((nh, V_HEAD, t), lambda i, h, j: (h, 0, i)),
        ],
        out_specs=pl.BlockSpec((nh * V_HEAD, tq), lambda i, h, j: (h, i * nq + j)),
        out_shape=jax.ShapeDtypeStruct((MLA_HEADS * V_HEAD, b * t), BF16),
        compiler_params=_cparams(("parallel", "parallel", "arbitrary")),
        name="mla_flash",
    )(q, k, vt)


def _mla_dec_prep_body(qa_ref, kvc_ref, misc_ref, tab_ref, wqb_ref, wk_ref,
                       g_qa_ref, g_nope_ref, g_rope_ref, g_kva_ref, g_knope_ref, g_krope_ref,
                       qabs_ref, qpe_ref, c_ref, kpe_ref):
    cos, sina, sinb = tab_ref[0], tab_ref[1], tab_ref[2]
    qs = _mla_queries(qa_ref[...], wqb_ref, g_qa_ref[...], g_nope_ref[...], g_rope_ref[...],
                      cos, sina, sinb)
    c, kp = _mla_latent(kvc_ref[...], misc_ref[...], g_kva_ref[...], g_krope_ref[...], cos, sina, sinb)
    c_ref[...] = c
    kpe_ref[...] = kp
    g_kn = g_knope_ref[...]
    for h in range(MLA_HEADS):
        qn, qp = qs[h]
        qabs_ref[h] = _dot_nt((qn * g_kn * MLA_SCALE).astype(BF16),
                              wk_ref[:, h * QK_NOPE:(h + 1) * QK_NOPE])
        qpe_ref[h] = qp * MLA_SCALE


def _mla_dec_prep(qa, kvc, misc, tabs, wqb, wk, gains):
    n = qa.shape[0]
    args = (qa, kvc, misc, tabs, wqb, wk) + tuple(gains)
    return pl.pallas_call(
        _mla_dec_prep_body,
        grid=(1,),
        in_specs=[_full(a.shape) for a in args],
        out_specs=[_full((MLA_HEADS, n, KV_LORA)), _full((MLA_HEADS, n, LANE)),
                   _full((n, KV_LORA)), _full((n, LANE))],
        out_shape=[
            jax.ShapeDtypeStruct((MLA_HEADS, n, KV_LORA), F32),
            jax.ShapeDtypeStruct((MLA_HEADS, n, LANE), F32),
            jax.ShapeDtypeStruct((n, KV_LORA), F32),
            jax.ShapeDtypeStruct((n, LANE), F32),
        ],
        compiler_params=_cparams(("arbitrary",)),
        name="mla_dec_prep",
    )(*args)


def _mla_dec_attn_body(pt_ref, qabs_ref, qpe_ref, wkt_ref, ckv_hbm, kpe_hbm,
                       m_out, l_out, acc_out, lhs_ref, cbuf, kbuf, c16, sem_c, sem_k, *, blocks_per_seq):
    npg = PAGES_PER_STEP
    n_seq = qabs_ref.shape[0]
    total_blocks = n_seq * blocks_per_seq
    nk = MLA_HEADS * QK_NOPE
    keys = npg * PAGE_SIZE

    def block_copies(gb, page_of):
        slot = gb % DEC_RING
        cps = []
        for i in range(npg):
            page = page_of(gb * npg + i)
            cps.append(pltpu.make_async_copy(ckv_hbm.at[page], cbuf.at[slot, i], sem_c.at[slot]))
            cps.append(pltpu.make_async_copy(kpe_hbm.at[page], kbuf.at[slot, i], sem_k.at[slot]))
        return cps

    def start_block(gb):
        for cp in block_copies(gb, lambda idx: pt_ref[idx]):
            cp.start()

    def wait_block(gb):
        for cp in block_copies(gb, lambda idx: 0):
            cp.wait()

    lhs_ref[0:nk, :] = wkt_ref[...]
    for gb in range(DEC_RING - 1):
        start_block(gb)

    def scores(gb):
        seq = gb // blocks_per_seq

        @pl.when(gb % blocks_per_seq == 0)
        def _():
            lhs_ref[nk:nk + BF16_ROWS, :] = qabs_ref[seq]

        slot = gb % DEC_RING
        c16[gb % 2] = cbuf[slot].reshape(keys, KV_LORA).astype(BF16)
        kpt = jnp.concatenate([kbuf[slot, i] for i in range(npg)], axis=1).astype(BF16)
        kt = _dot_nt(lhs_ref[...], c16[gb % 2])
        ssq = [jnp.sum(jnp.square(kt[h * QK_NOPE:(h + 1) * QK_NOPE]), axis=0, keepdims=True)
               for h in range(MLA_HEADS)]
        ssq = jnp.concatenate(ssq + [jnp.ones((SUBLANE - MLA_HEADS, keys), F32)], axis=0)
        pe = _dot(qpe_ref[seq], kpt)
        return kt[nk:nk + SUBLANE] * lax.rsqrt(ssq * (1.0 / QK_NOPE) + EPS) + pe[0:SUBLANE]

    def accumulate(gb, s, carry):
        m_old, l_old, acc_old = carry
        m_new = jnp.maximum(m_old, jnp.max(s, axis=-1, keepdims=True))
        p = jnp.exp(s - m_new)
        corr = jnp.exp(m_old - m_new)
        l_new = l_old * corr + jnp.sum(p, axis=-1, keepdims=True)
        acc_new = acc_old * corr + _dot(p.astype(BF16), c16[gb % 2])
        return m_new, l_new, acc_new

    def finish(seq, state):
        m_fin, l_fin, acc_fin = state
        m_out[seq] = jnp.broadcast_to(m_fin, m_out.shape[1:])
        l_out[seq] = jnp.broadcast_to(l_fin, l_out.shape[1:])
        acc_out[seq] = acc_fin

    def fetch_and_score(gb, after):
        nxt = gb + (DEC_RING - 1)

        @pl.when(nxt < total_blocks)
        def _():
            start_block(nxt)

        wait_block(gb)
        s_cur = scores(gb)
        return s_cur, after()

    init = (jnp.full((SUBLANE, 1), -jnp.inf, F32), jnp.zeros((SUBLANE, 1), F32),
            jnp.zeros((SUBLANE, KV_LORA), F32))

    def body(gb, carry):
        s_prev, state = carry
        s_cur, state = fetch_and_score(gb, lambda: accumulate(gb - 1, s_prev, state))
        closes_seq = gb % blocks_per_seq == 0

        @pl.when(closes_seq)
        def _():
            finish(gb // blocks_per_seq - 1, state)

        return s_cur, tuple(jnp.where(closes_seq, i, x) for i, x in zip(init, state))

    s_last, state = lax.fori_loop(1, total_blocks, body, fetch_and_score(0, lambda: init))
    finish(n_seq - 1, accumulate(total_blocks - 1, s_last, state))


def _mla_dec_attn(page_table, ckv_pool, kpe_pool, qabs, qpe, wkt):
    n, n_pages = page_table.shape
    npg = PAGES_PER_STEP
    nk = MLA_HEADS * QK_NOPE

    assert n_pages % npg == 0 and n * (n_pages // npg) >= DEC_RING
    whole = lambda shape: pl.BlockSpec(shape, lambda i, pt: (0,) * len(shape))
    grid_spec = pltpu.PrefetchScalarGridSpec(
        num_scalar_prefetch=1,
        grid=(1,),
        in_specs=[whole(qabs.shape), whole(qpe.shape), whole(wkt.shape),
                  pl.BlockSpec(memory_space=pl.ANY), pl.BlockSpec(memory_space=pl.ANY)],
        out_specs=[whole((n, SUBLANE, LANE)), whole((n, SUBLANE, LANE)), whole((n, SUBLANE, KV_LORA))],
        scratch_shapes=[
            pltpu.VMEM((nk + BF16_ROWS, KV_LORA), BF16),
            pltpu.VMEM((DEC_RING, npg, PAGE_SIZE, KV_LORA), F32),
            pltpu.VMEM((DEC_RING, npg, QK_ROPE, PAGE_SIZE), F32),
            pltpu.VMEM((2, npg * PAGE_SIZE, KV_LORA), BF16),
            pltpu.SemaphoreType.DMA((DEC_RING,)),
            pltpu.SemaphoreType.DMA((DEC_RING,)),
        ],
    )
    return pl.pallas_call(
        functools.partial(_mla_dec_attn_body, blocks_per_seq=n_pages // npg),
        grid_spec=grid_spec,
        out_shape=[
            jax.ShapeDtypeStruct((n, SUBLANE, LANE), F32),
            jax.ShapeDtypeStruct((n, SUBLANE, LANE), F32),
            jax.ShapeDtypeStruct((n, SUBLANE, KV_LORA), F32),
        ],
        compiler_params=_cparams(("arbitrary",)),
        name="mla_dec_attn",
    )(page_table.reshape(-1), qabs, qpe, wkt, ckv_pool, kpe_pool)


def _mla_dec_finish_body(m_ref, l_ref, acc_ref, qabs_ref, qpe_ref, c_ref, kpe_ref, wk_ref, wv_ref,
                         y_ref):
    c = c_ref[...]
    kp = kpe_ref[...]
    cb = c.astype(BF16)
    kn = _dot(cb, wk_ref[...])
    for h in range(MLA_HEADS):
        knh = kn[:, h * QK_NOPE:(h + 1) * QK_NOPE]
        r = lax.rsqrt(jnp.sum(knh * knh, axis=-1, keepdims=True) * (1.0 / QK_NOPE) + EPS)
        s_new = (r * jnp.sum(qabs_ref[h] * c, axis=-1, keepdims=True)
                 + jnp.sum(qpe_ref[h] * kp, axis=-1, keepdims=True))
        m_old = m_ref[h][:, 0:1]
        m_new = jnp.maximum(m_old, s_new)
        p = jnp.exp(s_new - m_new)
        corr = jnp.exp(m_old - m_new)
        l = l_ref[h][:, 0:1] * corr + p
        lat = (acc_ref[h] * corr + p * c) / l
        y_ref[:, h * V_HEAD:(h + 1) * V_HEAD] = _dot(
            lat.astype(BF16), wv_ref[:, h * V_HEAD:(h + 1) * V_HEAD]).astype(BF16)


def _mla_dec_finish(m, l, acc, qabs, qpe, c, kpe, wk, wv):
    n = c.shape[0]
    args = (m, l, acc, qabs, qpe, c, kpe, wk, wv)
    return pl.pallas_call(
        _mla_dec_finish_body,
        grid=(1,),
        in_specs=[_full(a.shape) for a in args],
        out_specs=_full((n, MLA_HEADS * V_HEAD)),
        out_shape=jax.ShapeDtypeStruct((n, MLA_HEADS * V_HEAD), BF16),
        compiler_params=_cparams(("arbitrary",)),
        name="mla_dec_finish",
    )(*args)


def _mix_q_body(x_ref, yg_ref, ym_ref, wmix_ref, g_mem_ref, wq_ref, g_qn_ref, x1_ref, q_ref, *,
                mla_transposed):
    mla_dot = _dot_tn if mla_transposed else _dot
    x1 = (x_ref[...] + _dot(yg_ref[...], wmix_ref[0:GDN_V, :])
          + mla_dot(ym_ref[...], wmix_ref[GDN_V:, :]))
    x1_ref[...] = x1
    q = _dot(_rms(x1, g_mem_ref[...]).astype(BF16), wq_ref[...])
    g_qn = g_qn_ref[...]
    for h in range(MEM_HEADS):
        sl = slice(h * MEM_HD, (h + 1) * MEM_HD)
        q_ref[:, sl] = (_rms(q[:, sl], g_qn) * (MEM_HD ** -0.5)).astype(BF16)


def _mix_q(x, yg, ym, wmix, g_mem, wq, g_qn, tm, mla_transposed=False):
    t = x.shape[0]
    row = lambda n: pl.BlockSpec((tm, n), lambda i: (i, 0))
    mla_width = MLA_HEADS * V_HEAD
    ym_spec = pl.BlockSpec((mla_width, tm), lambda i: (0, i)) if mla_transposed else row(mla_width)
    return pl.pallas_call(
        functools.partial(_mix_q_body, mla_transposed=mla_transposed),
        grid=(t // tm,),
        in_specs=[row(D_MODEL), row(GDN_V), ym_spec, _full(wmix.shape),
                  _full(g_mem.shape), _full(wq.shape), _full(g_qn.shape)],
        out_specs=[row(D_MODEL), row(MEM_DIM)],
        out_shape=[jax.ShapeDtypeStruct((t, D_MODEL), F32), jax.ShapeDtypeStruct((t, MEM_DIM), BF16)],
        compiler_params=_cparams(("parallel",)),
        name="mix_q",
    )(x, yg, ym, wmix, g_mem, wq, g_qn)


def _mem_attn_body(q_ref, k_ref, v_ref, o_ref):
    q = q_ref[0]
    heads = range(MEM_HEADS)
    sls = [slice(h * MEM_HD, (h + 1) * MEM_HD) for h in heads]
    s = [_dot_nt(q[:, sls[h]], k_ref[0, :, sls[h]].astype(BF16)) for h in heads]
    p = [jnp.exp(s[h] - jnp.max(s[h], axis=-1, keepdims=True)) for h in heads]
    o = [_dot(p[h].astype(BF16), v_ref[0, :, sls[h]].astype(BF16)) for h in heads]
    for h in heads:
        o_ref[0, :, sls[h]] = (o[h] / jnp.sum(p[h], axis=-1, keepdims=True)).astype(BF16)


def _mem_attn(q, k, v, tq):
    nb, t, _ = q.shape
    m = k.shape[1]
    return pl.pallas_call(
        _mem_attn_body,
        grid=(nb, t // tq),
        in_specs=[pl.BlockSpec((1, tq, MEM_DIM), lambda i, j: (i, j, 0)),
                  pl.BlockSpec((1, m, MEM_DIM), lambda i, j: (i, 0, 0)),
                  pl.BlockSpec((1, m, MEM_DIM), lambda i, j: (i, 0, 0))],
        out_specs=pl.BlockSpec((1, tq, MEM_DIM), lambda i, j: (i, j, 0)),
        out_shape=jax.ShapeDtypeStruct((nb, t, MEM_DIM), BF16),
        compiler_params=_cparams(("parallel", "arbitrary")),
        name="mem_attn",
    )(q, k, v)


def _mem_attn_dec_body(q_ref, k_ref, v_ref, o_ref):
    for i in range(q_ref.shape[0]):
        q = q_ref[i:i + 1]
        s = jnp.sum(k_ref[i] * q, axis=-1, keepdims=True)
        p = jnp.exp(s - jnp.max(s, axis=0, keepdims=True))
        o_ref[i:i + 1] = (jnp.sum(p * v_ref[i], axis=0, keepdims=True)
                          / jnp.sum(p, axis=0, keepdims=True))


MEM_DEC_SEQS = 4


def _mem_attn_dec(q, k, v):
    n, m, nh, hd = k.shape
    ns = MEM_DEC_SEQS
    kv_spec = pl.BlockSpec((ns, m, nh, hd), lambda i: (i, 0, 0, 0))
    return pl.pallas_call(
        _mem_attn_dec_body,
        grid=(n // ns,),
        in_specs=[pl.BlockSpec((ns, nh, hd), lambda i: (i, 0, 0)), kv_spec, kv_spec],
        out_specs=pl.BlockSpec((ns, nh, hd), lambda i: (i, 0, 0)),
        out_shape=jax.ShapeDtypeStruct((n, nh, hd), F32),
        compiler_params=_cparams(("parallel",)),
        name="mem_attn_dec",
    )(q, k, v)


def _out_ffn_body(x1_ref, o_ref, wo_ref, g_ffn_ref, wg_ref, wu_ref, wd_ref, y_ref):
    x2 = x1_ref[...] + _dot(o_ref[...], wo_ref[...])
    h = _rms(x2, g_ffn_ref[...]).astype(BF16)
    act = (_silu(_dot(h, wg_ref[...])) * _dot(h, wu_ref[...])).astype(BF16)
    y_ref[...] = x2 + _dot(act, wd_ref[...])


def _out_ffn(x1, o, wo, g_ffn, wg, wu, wd, tm):
    t = x1.shape[0]
    row = lambda n: pl.BlockSpec((tm, n), lambda i: (i, 0))
    const = lambda a: pl.BlockSpec(a.shape, lambda i: (0,) * a.ndim, pipeline_mode=pl.Buffered(1))
    return pl.pallas_call(
        _out_ffn_body,
        grid=(t // tm,),
        in_specs=[row(D_MODEL), row(MEM_DIM), const(wo), const(g_ffn), const(wg), const(wu), const(wd)],
        out_specs=row(D_MODEL),
        out_shape=jax.ShapeDtypeStruct((t, D_MODEL), F32),
        compiler_params=_cparams(("parallel",)),
        name="out_ffn",
    )(x1, o, wo, g_ffn, wg, wu, wd)


def _mem_kv_body(mem_ref, wk_ref, wv_ref, g_ref, k_ref, v_ref):
    mb = mem_ref[...].astype(BF16)
    k = _dot(mb, wk_ref[...])
    g = g_ref[...]
    for h in range(MEM_HEADS):
        sl = slice(h * MEM_HD, (h + 1) * MEM_HD)
        k_ref[:, sl] = _rms(k[:, sl], g)
    v_ref[...] = _dot(mb, wv_ref[...])


def _mem_kv(mem, wk, wv, g, tm):
    t = mem.shape[0]
    row = lambda n: pl.BlockSpec((tm, n), lambda i: (i, 0))
    return pl.pallas_call(
        _mem_kv_body,
        grid=(t // tm,),
        in_specs=[row(D_MODEL), _full(wk.shape), _full(wv.shape), _full(g.shape)],
        out_specs=[row(MEM_DIM), row(MEM_DIM)],
        out_shape=[jax.ShapeDtypeStruct((t, MEM_DIM), F32)] * 2,
        compiler_params=_cparams(("parallel",)),
        name="mem_kv",
    )(mem, wk, wv, g)


def _row(v):
    return v.reshape(1, -1).astype(F32)


def _pad_lanes(v, width=LANE):
    return jnp.pad(v, ((0, 0), (0, width - v.shape[1])))


def _rope_tables(pos, rows):
    half = QK_ROPE // 2
    inv = ROPE_THETA ** (-jnp.arange(half, dtype=F32) / half)
    ang = pos.astype(F32)[:, None] * inv[None, :]
    cos, sin = jnp.cos(ang), jnp.sin(ang)
    zero = jnp.zeros_like(cos)
    tabs = jnp.stack([
        _pad_lanes(jnp.concatenate([cos, cos], axis=1)),
        _pad_lanes(jnp.concatenate([-sin, zero], axis=1)),
        _pad_lanes(jnp.concatenate([zero, sin], axis=1)),
    ])
    return jnp.broadcast_to(tabs, (3, rows, LANE)) if tabs.shape[1] == 1 else tabs


def kernel(x_prompt, x_sample, cache_mla_ckv, cache_mla_kpe, cache_mem_k, cache_mem_v, state_gdn_S, state_gdn_conv, page_table, mem_prompt, norm_mix_g, w_in, gdn_conv_w, gdn_A_log, gdn_dt_bias, gdn_out_norm_g, mla_q_a_norm_g, mla_w_q_b, mla_kv_a_norm_g, mla_w_kv_b, mla_qn_nope_g, mla_qn_rope_g, mla_kn_nope_g, mla_kn_rope_g, w_mix_out, norm_mem_g, mem_wq, mem_wk, mem_wv, mem_wo, mem_qn_g, mem_kn_g, norm_ffn_g, ffn_w_gate, ffn_w_up, ffn_w_down):
    depth = w_in.shape[0]
    assert depth == 1, "single-layer trunk"
    bsz, seq, _ = x_prompt.shape
    nseq, dseq, _ = x_sample.shape
    assert dseq == 1, "one new token per decode sequence"
    past_len = page_table.shape[1] * PAGE_SIZE
    n_tok = bsz * seq

    w = w_in[0]
    o_gz = CONV_DIM
    o_b = o_gz + GDN_V
    o_a = o_b + GDN_HEADS
    o_qa = o_a + GDN_HEADS
    o_c = o_qa + Q_LORA
    o_kpe = o_c + KV_LORA
    misc_w = _pad_lanes(jnp.concatenate([w[:, o_kpe:o_kpe + QK_ROPE], w[:, o_b:o_qa]], axis=1))
    w_in_p = jnp.concatenate([w[:, :o_b], w[:, o_qa:o_kpe], misc_w], axis=1).astype(BF16)

    wqb = mla_w_q_b[0].reshape(Q_LORA, MLA_HEADS, QK_NOPE + QK_ROPE)
    wqb = jnp.pad(wqb, ((0, 0), (0, 0), (0, QK_PAD - QK_NOPE - QK_ROPE)))
    wqb = wqb.reshape(Q_LORA, MLA_HEADS * QK_PAD).astype(BF16)
    wkvb = mla_w_kv_b[0].reshape(KV_LORA, MLA_HEADS, QK_NOPE + V_HEAD)
    wk = wkvb[:, :, :QK_NOPE].reshape(KV_LORA, MLA_HEADS * QK_NOPE).astype(BF16)
    wv = wkvb[:, :, QK_NOPE:].reshape(KV_LORA, MLA_HEADS * V_HEAD).astype(BF16)
    wkt = wk.T

    lanep = jnp.zeros((2, LANE), F32)
    lanep = lanep.at[0, MISC_A:MISC_A + GDN_HEADS].set(gdn_A_log[0])
    lanep = lanep.at[1, MISC_A:MISC_A + GDN_HEADS].set(gdn_dt_bias[0])
    rowp = jnp.zeros((2 * SUBLANE,), F32)
    rowp = rowp.at[GDN_HEADS:2 * GDN_HEADS].set(gdn_A_log[0])
    rowp = rowp.at[SUBLANE + GDN_HEADS:SUBLANE + 2 * GDN_HEADS].set(gdn_dt_bias[0])
    rowp = jnp.broadcast_to(rowp[:, None], (2 * SUBLANE, GDN_TB))

    g_mix = _row(norm_mix_g[0])
    g_out = _row(gdn_out_norm_g[0])
    mla_gains = (_row(mla_q_a_norm_g[0]), _row(mla_qn_nope_g[0]), _pad_lanes(_row(mla_qn_rope_g[0])),
                 _row(mla_kv_a_norm_g[0]), _row(mla_kn_nope_g[0]), _pad_lanes(_row(mla_kn_rope_g[0])))
    wmix = w_mix_out[0].astype(BF16)
    wq_mem = mem_wq[0].astype(BF16)
    wk_mem = mem_wk[0].astype(BF16)
    wv_mem = mem_wv[0].astype(BF16)
    wo_mem = mem_wo[0].astype(BF16)
    wg = ffn_w_gate[0].astype(BF16)
    wu = ffn_w_up[0].astype(BF16)
    wd = ffn_w_down[0].astype(BF16)
    g_mem = _row(norm_mem_g[0])
    g_ffn = _row(norm_ffn_g[0])
    g_qn = _row(mem_qn_g[0])
    g_kn = _row(mem_kn_g[0])
    conv_w = gdn_conv_w[0]

    xp = x_prompt.reshape(n_tok, D_MODEL)
    tiles_per_seq = seq // TOKEN_TILE
    qkv_act, qkv_tail, gz, qa, kvc, misc = _in_proj_conv(xp, g_mix, w_in_p, conv_w, TOKEN_TILE,
                                                         tiles_per_seq)
    grow = jnp.swapaxes(misc.reshape(bsz, seq, LANE)[:, :, MISC_B:MISC_B + SUBLANE], 1, 2)
    y_gdn, p_s = _gdn_prompt(qkv_act.reshape(bsz, seq, CONV_DIM), misc.reshape(bsz, seq, LANE), grow,
                             gz.reshape(bsz, seq, GDN_V), lanep, rowp, g_out)
    p_conv = qkv_tail.reshape(bsz, tiles_per_seq, SUBLANE, CONV_DIM)[:, -1, SUBLANE - (CONV_W - 1):, :]

    tabs_p = _rope_tables(jnp.arange(seq), seq)
    q_full, k_full, vt_full, p_c, p_kpe = _mla_prep(qa, kvc, misc, tabs_p, wqb, wk, wv.T, mla_gains,
                                                    TOKEN_TILE)
    y_mla_t = _mla_flash(q_full, k_full, vt_full, bsz, seq)

    mem_k, mem_v = _mem_kv(mem_prompt.reshape(-1, D_MODEL), wk_mem, wv_mem, g_kn, TOKEN_TILE)
    n_mem = mem_prompt.shape[1]
    x1, q_mem = _mix_q(xp, y_gdn.reshape(n_tok, GDN_V), y_mla_t, wmix, g_mem, wq_mem, g_qn, MIX_TILE,
                       mla_transposed=True)
    o_mem = _mem_attn(q_mem.reshape(bsz, seq, MEM_DIM), mem_k.reshape(bsz, n_mem, MEM_DIM),
                      mem_v.reshape(bsz, n_mem, MEM_DIM), TOKEN_TILE)
    y_prompt = _out_ffn(x1, o_mem.reshape(n_tok, MEM_DIM), wo_mem, g_ffn, wg, wu, wd, TOKEN_TILE)

    xs = x_sample.reshape(nseq, D_MODEL)
    qkv_s, gz_s, qa_s, kvc_s, misc_s = _in_proj(xs, g_mix, w_in_p, nseq)
    conv_prev = state_gdn_conv[0]
    y_gdn_s, s_new = _gdn_decode(qkv_s, jnp.swapaxes(conv_prev, 0, 1), misc_s, gz_s, state_gdn_S[0],
                                 conv_w, lanep, g_out)
    s_conv = jnp.concatenate([conv_prev[:, 1:, :], qkv_s[:, None, :]], axis=1)

    tabs_s = _rope_tables(jnp.full((1,), past_len), nseq)
    qabs, qpe, c_new, kpe_new = _mla_dec_prep(qa_s, kvc_s, misc_s, tabs_s, wqb, wk, mla_gains)
    pad_rows = lambda a: jnp.pad(jnp.swapaxes(a, 0, 1), ((0, 0), (0, BF16_ROWS - MLA_HEADS), (0, 0)))
    m_run, l_run, acc_run = _mla_dec_attn(
        page_table, cache_mla_ckv[0], jnp.swapaxes(cache_mla_kpe[0], 1, 2),
        pad_rows(qabs).astype(BF16), pad_rows(qpe[:, :, :QK_ROPE]).astype(BF16), wkt)
    heads_first = lambda a: jnp.swapaxes(a[:, :MLA_HEADS], 0, 1)
    y_mla_s = _mla_dec_finish(heads_first(m_run), heads_first(l_run), heads_first(acc_run),
                              qabs, qpe, c_new, kpe_new, wk, wv)

    x1_s, q_mem_s = _mix_q(xs, y_gdn_s, y_mla_s, wmix, g_mem, wq_mem, g_qn, nseq)
    o_mem_s = _mem_attn_dec(q_mem_s.astype(F32).reshape(nseq, MEM_HEADS, MEM_HD),
                            cache_mem_k[0], cache_mem_v[0])
    y_sample = _out_ffn(x1_s, o_mem_s.reshape(nseq, MEM_DIM).astype(BF16),
                        wo_mem, g_ffn, wg, wu, wd, nseq)

    return (
        y_prompt.reshape(bsz, seq, D_MODEL),
        y_sample.reshape(nseq, 1, D_MODEL),
        p_c.reshape(1, bsz, seq, KV_LORA),
        p_kpe.reshape(1, bsz, seq, QK_ROPE),
        mem_k.reshape(1, bsz, n_mem, MEM_HEADS, MEM_HD),
        mem_v.reshape(1, bsz, n_mem, MEM_HEADS, MEM_HD),
        p_s[None],
        p_conv[None],
        c_new.reshape(1, nseq, 1, KV_LORA),
        kpe_new[:, :QK_ROPE].reshape(1, nseq, 1, QK_ROPE),
        s_new[None],
        s_conv[None],
    )
```

```python
import functools

import jax
import jax.numpy as jnp
from jax import lax
from jax.experimental import pallas as pl
from jax.experimental.pallas import tpu as pltpu

F32 = jnp.float32
BF16 = jnp.bfloat16

D_MODEL = 1024
GDN_HEADS = 4
GDN_DK = 128
GDN_DV = 128
CONV_W = 4
CHUNK = 64
MLA_HEADS = 4
Q_LORA = 384
KV_LORA = 256
QK_NOPE = 128
QK_ROPE = 64
V_HEAD = 128
ROPE_THETA = 10000.0
PAGE_SIZE = 128
MEM_HEADS = 4
MEM_HD = 128
EPS = 1e-6
GDN_QK = GDN_HEADS * GDN_DK
GDN_V = GDN_HEADS * GDN_DV
CONV_DIM = 2 * GDN_QK + GDN_V
MLA_SCALE = (QK_NOPE + QK_ROPE) ** -0.5
MEM_DIM = MEM_HEADS * MEM_HD

LANE = 128
SUBLANE = 8
BF16_ROWS = 16
QK_PAD = 2 * LANE
MISC_B = QK_ROPE
MISC_A = QK_ROPE + GDN_HEADS
VMEM_LIMIT = 56 * 1024 * 1024

TOKEN_TILE = 512
MIX_TILE = 1024
GDN_TILE = 2 * CHUNK
GDN_TB = 4 * GDN_TILE
PAGES_PER_STEP = 32
DEC_RING = 4


def _cparams(sem):
    return pltpu.CompilerParams(dimension_semantics=sem, vmem_limit_bytes=VMEM_LIMIT)


def _full(shape):
    n = len(shape)
    return pl.BlockSpec(shape, lambda *_: (0,) * n)


def _dot(a, b):
    return jnp.dot(a, b, preferred_element_type=F32)


def _dot_nt(a, b):
    return lax.dot_general(a, b, (((1,), (1,)), ((), ())), preferred_element_type=F32)


def _dot_tn(a, b):
    return lax.dot_general(a, b, (((0,), (0,)), ((), ())), preferred_element_type=F32)


def _split3(x):
    hi = x.astype(BF16)
    r = x - hi.astype(F32)
    mid = r.astype(BF16)
    lo = (r - mid.astype(F32)).astype(BF16)
    return hi, mid, lo


def _dot01_left(m01, x):
    hi, mid, lo = _split3(x)
    return _dot(m01, hi) + _dot(m01, mid) + _dot(m01, lo)


def _dot01_right(x, m01):
    hi, mid, lo = _split3(x)
    return _dot(hi, m01) + _dot(mid, m01) + _dot(lo, m01)


def _rms(x, g, n=None):
    n = x.shape[-1] if n is None else n
    ms = jnp.sum(x * x, axis=-1, keepdims=True) * (1.0 / n)
    return x * lax.rsqrt(ms + EPS) * g


def _silu(x):
    return x * jax.nn.sigmoid(x)


def _softplus(x):
    return jnp.maximum(x, 0.0) + jnp.log1p(jnp.exp(-jnp.abs(x)))


def _iota(shape, dim):
    return lax.broadcasted_iota(jnp.int32, shape, dim)


def _block_id(i, size):
    assert size & (size - 1) == 0
    return lax.shift_right_logical(i, size.bit_length() - 1)


def _in_proj_body(x_ref, g_ref, w_ref, qkv_ref, gz_ref, qa_ref, kvc_ref, misc_ref):
    h = _rms(x_ref[...], g_ref[...]).astype(BF16)
    off = 0
    for ref in (qkv_ref, gz_ref, qa_ref, kvc_ref, misc_ref):
        n = ref.shape[-1]
        ref[...] = _dot(h, w_ref[:, off:off + n])
        off += n


def _in_proj(x, g, w, tm):
    t = x.shape[0]
    widths = (CONV_DIM, GDN_V, Q_LORA, KV_LORA, LANE)
    return pl.pallas_call(
        _in_proj_body,
        grid=(t // tm,),
        in_specs=[pl.BlockSpec((tm, D_MODEL), lambda i: (i, 0)), _full(g.shape), _full(w.shape)],
        out_specs=[pl.BlockSpec((tm, n), lambda i: (i, 0)) for n in widths],
        out_shape=[jax.ShapeDtypeStruct((t, n), F32) for n in widths],
        compiler_params=_cparams(("parallel",)),
        name="in_proj",
    )(x, g, w)


def _qk_l2norm(x, scale):
    return x * (lax.rsqrt(jnp.sum(x * x, axis=-1, keepdims=True) + EPS) * scale)


def _in_proj_conv_body(x_ref, g_ref, w_ref, convw_ref, qkv_ref, tail_ref, gz_ref, qa_ref, kvc_ref,
                       misc_ref, xe_ref, *, tiles_per_seq):
    tm = x_ref.shape[0]
    @pl.when(pl.program_id(0) % tiles_per_seq == 0)
    def _():
        xe_ref[0:SUBLANE, :] = jnp.zeros((SUBLANE, CONV_DIM), F32)

    h = _rms(x_ref[...], g_ref[...]).astype(BF16)
    w = convw_ref[...]
    pair = 2 * GDN_DK
    blocks = [slice(blk * pair, (blk + 1) * pair) for blk in range(CONV_DIM // pair)]
    for sl in blocks:
        xe_ref[SUBLANE:SUBLANE + tm, sl] = _dot(h, w_ref[:, sl])
    off = CONV_DIM
    for ref in (gz_ref, qa_ref, kvc_ref, misc_ref):
        n = ref.shape[-1]
        ref[...] = _dot(h, w_ref[:, off:off + n])
        off += n
    for blk, sl in enumerate(blocks):
        conv = xe_ref[SUBLANE:SUBLANE + tm, sl] * w[CONV_W - 1:CONV_W, sl]
        for i in range(CONV_W - 1):
            lo = SUBLANE - (CONV_W - 1) + i
            conv = conv + xe_ref[lo:lo + tm, sl] * w[i:i + 1, sl]
        act = _silu(conv)
        for half in range(2):
            head_blk = 2 * blk + half
            hs = slice(half * GDN_DK, (half + 1) * GDN_DK)
            out_sl = slice(head_blk * GDN_DK, (head_blk + 1) * GDN_DK)
            if head_blk < GDN_HEADS:
                qkv_ref[:, out_sl] = _qk_l2norm(act[:, hs], GDN_DK ** -0.5)
            elif head_blk < 2 * GDN_HEADS:
                qkv_ref[:, out_sl] = _qk_l2norm(act[:, hs], 1.0)
            else:
                qkv_ref[:, out_sl] = act[:, hs]
    tail = xe_ref[tm:SUBLANE + tm, :]
    tail_ref[...] = tail
    xe_ref[0:SUBLANE, :] = tail


def _in_proj_conv(x, g, w, convw, tm, tiles_per_seq):
    t = x.shape[0]
    widths = (CONV_DIM, CONV_DIM, GDN_V, Q_LORA, KV_LORA, LANE)
    rows = (tm, SUBLANE, tm, tm, tm, tm)
    return pl.pallas_call(
        functools.partial(_in_proj_conv_body, tiles_per_seq=tiles_per_seq),
        grid=(t // tm,),
        in_specs=[pl.BlockSpec((tm, D_MODEL), lambda i: (i, 0)), _full(g.shape), _full(w.shape),
                  _full(convw.shape)],
        out_specs=[pl.BlockSpec((r, n), lambda i: (i, 0)) for r, n in zip(rows, widths)],
        out_shape=[jax.ShapeDtypeStruct((t // tm * r, n), F32) for r, n in zip(rows, widths)],
        scratch_shapes=[pltpu.VMEM((SUBLANE + tm, CONV_DIM), F32)],
        compiler_params=_cparams(("arbitrary",)),
        name="in_proj_conv",
    )(x, g, w, convw)


def _gate_lane_values(misc, lanep):
    lane = _iota(misc.shape, 1)
    is_a = (lane >= MISC_A) & (lane < MISC_A + GDN_HEADS)
    beta_l = jax.nn.sigmoid(misc)
    g_l = -jnp.exp(lanep[0:1, :]) * _softplus(jnp.where(is_a, misc, 0.0) + lanep[1:2, :])
    return beta_l, g_l


def _expand_matrix(first_lane):
    r = _iota((LANE, GDN_V), 0)
    c = _iota((LANE, GDN_V), 1)
    return jnp.where(r == first_lane + _block_id(c, GDN_DV), 1.0, 0.0).astype(BF16)


def _gdn_gate_out(o, z, gout):
    return _rms(o, gout) * _silu(z)


def _tri_inverse(a_list, eye, same16, same32):
    def mm(xs, ys):
        return [_dot(x.astype(BF16), y.astype(BF16)) for x, y in zip(xs, ys)]

    def add(ts, us):
        return [t + u for t, u in zip(ts, us)]

    def sub(ts, us):
        return [t - u for t, u in zip(ts, us)]

    n = eye.shape[0]
    ad = [jnp.where(same16, a, 0.0) for a in a_list]
    t = [eye - x for x in ad]
    p = mm(ad, ad)
    for _ in range(2):
        tp = mm([jnp.concatenate([x, y], axis=0) for x, y in zip(t, p)], p)
        t = add(t, [x[:n] for x in tp])
        p = [x[n:] for x in tp]
    t = add(t, mm(t, p))
    b1 = [jnp.where(same32, a, 0.0) - x for a, x in zip(a_list, ad)]
    t = sub(t, mm(t, mm(b1, t)))
    b2 = [jnp.where(same32, 0.0, a) for a in a_list]
    t = sub(t, mm(t, mm(b2, t)))
    return t


def _gdn_prompt_body(qkv_ref, misc_ref, grow_ref, gz_ref, lanep_ref, rowp_ref, gout_ref,
                     y_ref, s_out_ref, s_ref):
    tb = GDN_TB
    t_idx = pl.program_id(1)

    @pl.when(t_idx == 0)
    def _():
        s_ref[...] = jnp.zeros(s_ref.shape, F32)

    tile = GDN_TILE
    tiles = range(tb // tile)
    rows = [slice(t * tile, (t + 1) * tile) for t in tiles]
    ri = _iota((tile, tile), 0)
    ci = _iota((tile, tile), 1)
    same64 = _block_id(ri, CHUNK) == _block_id(ci, CHUNK)
    lower = same64 & (ci <= ri)
    strict = same64 & (ci < ri)
    same16 = _block_id(ri, 16) == _block_id(ci, 16)
    same32 = _block_id(ri, 32) == _block_id(ci, 32)
    eye = jnp.where(ri == ci, 1.0, 0.0)
    l_incl = jnp.where(lower, 1.0, 0.0).astype(BF16)
    u_incl = jnp.where(same64 & (ri <= ci), 1.0, 0.0).astype(BF16)
    u_strict = jnp.where(same64 & (ci > ri), 1.0, 0.0).astype(BF16)

    beta_l, g_l = _gate_lane_values(misc_ref[0], lanep_ref[...])
    e_b = _expand_matrix(MISC_B)
    e_a = _expand_matrix(MISC_A)
    beta_bc = _dot01_right(beta_l, e_b)
    gc_col = jnp.concatenate([_dot01_left(l_incl, g_l[r]) for r in rows], axis=0)
    gt_col = jnp.concatenate([_dot01_left(u_strict, g_l[r]) for r in rows], axis=0)
    gc_bc = _dot01_right(gc_col, e_a)
    gt_bc = _dot01_right(gt_col, e_a)
    egc = jnp.exp(gc_bc)
    etail = jnp.exp(gt_bc)

    grow = grow_ref[0]
    rowi = _iota(grow.shape, 0)
    g_r = -jnp.exp(rowp_ref[0:SUBLANE, :]) * _softplus(
        jnp.where(rowi >= GDN_HEADS, grow, 0.0) + rowp_ref[SUBLANE:2 * SUBLANE, :])
    gcr = [_dot01_right(g_r[:, r], u_incl) for r in rows]

    gout = gout_ref[...]
    gz = gz_ref[0]
    heads = range(GDN_HEADS)
    sls = [slice(h * GDN_DK, (h + 1) * GDN_DK) for h in heads]
    units = [(t, h) for t in tiles for h in heads]
    qh, kh, kb, vb, dm = {}, {}, {}, {}, {}
    for h in heads:
        q = qkv_ref[0, :, h * GDN_DK:(h + 1) * GDN_DK]
        k = qkv_ref[0, :, GDN_QK + h * GDN_DK:GDN_QK + (h + 1) * GDN_DK]
        v = qkv_ref[0, :, 2 * GDN_QK + h * GDN_DV:2 * GDN_QK + (h + 1) * GDN_DV]
        bb = beta_bc[:, sls[h]]
        for t in tiles:
            qh[t, h] = q[rows[t]]
            kh[t, h] = k[rows[t]]
            kb[t, h] = (k * bb)[rows[t]]
            vb[t, h] = (v * bb)[rows[t]]
            diff = gc_bc[rows[t], sls[h]] - gcr[t][GDN_HEADS + h:GDN_HEADS + h + 1, :]
            dm[t, h] = jnp.where(lower, jnp.exp(jnp.where(lower, diff, 0.0)), 0.0)
    p = {u: _dot_nt(jnp.concatenate([kb[u], qh[u]], axis=0).astype(BF16), kh[u].astype(BF16))
         for u in units}
    a_mat = [jnp.where(strict, p[u][:tile] * dm[u], 0.0) for u in units]
    attn = {u: (p[u][tile:] * dm[u]).astype(BF16) for u in units}
    t_mat = dict(zip(units, _tri_inverse(a_mat, eye, same16, same32)))
    egc_u = {(t, h): egc[rows[t], sls[h]] for t, h in units}
    uw = {u: _dot(t_mat[u].astype(BF16),
                  jnp.concatenate([vb[u], kb[u] * egc_u[u]], axis=1).astype(BF16)) for u in units}
    qd = {u: qh[u] * egc_u[u] for u in units}
    kt = {(t, h): (kh[t, h] * etail[rows[t], sls[h]]).astype(BF16) for t, h in units}
    s = [s_ref[h] for h in heads]
    o_inter = {u: [] for u in units}
    v_new = {u: [] for u in units}
    for t in tiles:
        for c in range(tile // CHUNK):
            rs = slice(c * CHUNK, (c + 1) * CHUNK)
            last = t * tile + (c + 1) * CHUNK - 1
            wq = [_dot(jnp.concatenate([uw[t, h][rs, GDN_DV:], qd[t, h][rs]], axis=0).astype(BF16),
                       s[h].astype(BF16)) for h in heads]
            vn = [uw[t, h][rs, :GDN_DV] - wq[h][:CHUNK] for h in heads]
            egl = [jnp.exp(gc_bc[last:last + 1, sls[h]]) for h in heads]
            s = [s[h] * egl[h] + _dot_tn(kt[t, h][rs], vn[h].astype(BF16)) for h in heads]
            for h in heads:
                o_inter[t, h].append(wq[h][CHUNK:])
                v_new[t, h].append(vn[h])
    o = {u: jnp.concatenate(o_inter[u], axis=0)
         + _dot(attn[u], jnp.concatenate(v_new[u], axis=0).astype(BF16)) for u in units}
    for h in heads:
        s_ref[h] = s[h]
        for t in tiles:
            y_ref[0, rows[t], sls[h]] = _gdn_gate_out(o[t, h], gz[rows[t], sls[h]], gout).astype(BF16)

    @pl.when(t_idx == pl.num_programs(1) - 1)
    def _():
        s_out_ref[0] = s_ref[...]


def _gdn_prompt(qkv, misc, grow, gz, lanep, rowp, gout):
    b, t, _ = qkv.shape
    tb = GDN_TB
    return pl.pallas_call(
        _gdn_prompt_body,
        grid=(b, t // tb),
        in_specs=[
            pl.BlockSpec((1, tb, CONV_DIM), lambda i, j: (i, j, 0)),
            pl.BlockSpec((1, tb, LANE), lambda i, j: (i, j, 0)),
            pl.BlockSpec((1, SUBLANE, tb), lambda i, j: (i, 0, j)),
            pl.BlockSpec((1, tb, GDN_V), lambda i, j: (i, j, 0)),
            _full(lanep.shape), _full(rowp.shape), _full(gout.shape),
        ],
        out_specs=[
            pl.BlockSpec((1, tb, GDN_V), lambda i, j: (i, j, 0)),
            pl.BlockSpec((1, GDN_HEADS, GDN_DK, GDN_DV), lambda i, j: (i, 0, 0, 0)),
        ],
        out_shape=[
            jax.ShapeDtypeStruct((b, t, GDN_V), BF16),
            jax.ShapeDtypeStruct((b, GDN_HEADS, GDN_DK, GDN_DV), F32),
        ],
        scratch_shapes=[pltpu.VMEM((GDN_HEADS, GDN_DK, GDN_DV), F32)],
        compiler_params=_cparams(("parallel", "arbitrary")),
        name="gdn_prompt",
    )(qkv, misc, grow, gz, lanep, rowp, gout)


GDN_DEC_SEQS = 8


def _gdn_decode_body(qkv_ref, cprev_ref, misc_ref, gz_ref, s_in_ref, convw_ref, lanep_ref, gout_ref,
                     y_ref, s_out_ref, o_ref):
    ns = GDN_DEC_SEQS
    w = convw_ref[...]
    conv = qkv_ref[...] * w[CONV_W - 1:CONV_W, :]
    for i in range(CONV_W - 1):
        conv = conv + cprev_ref[i] * w[i:i + 1, :]
    conv = _silu(conv)
    beta_l, g_l = _gate_lane_values(misc_ref[...], lanep_ref[...])
    beta_bc = _dot01_right(beta_l, _expand_matrix(MISC_B))
    eg = jnp.exp(_dot01_right(g_l, _expand_matrix(MISC_A)))
    row0 = jnp.where(_iota((SUBLANE, LANE), 0) == 0, 1.0, 0.0).astype(BF16)
    units = [(h, i) for h in range(GDN_HEADS) for i in range(ns)]
    qn, kn, vv = {}, {}, {}
    for h in range(GDN_HEADS):
        qh = conv[:, h * GDN_DK:(h + 1) * GDN_DK]
        kh = conv[:, GDN_QK + h * GDN_DK:GDN_QK + (h + 1) * GDN_DK]
        qn[h] = qh * lax.rsqrt(jnp.sum(qh * qh, axis=-1, keepdims=True) + EPS) * (GDN_DK ** -0.5)
        kn[h] = kh * lax.rsqrt(jnp.sum(kh * kh, axis=-1, keepdims=True) + EPS)
        vv[h] = conv[:, 2 * GDN_QK + h * GDN_DV:2 * GDN_QK + (h + 1) * GDN_DV]
    sl = lambda h: slice(h * GDN_DK, (h + 1) * GDN_DK)
    k8 = {u: jnp.broadcast_to(kn[u[0]][u[1]:u[1] + 1, :], (SUBLANE, GDN_DK)) for u in units}
    s_dec = {(h, i): s_in_ref[i, h] * eg[i:i + 1, sl(h)] for h, i in units}
    kv = {u: _dot(k8[u].astype(BF16), s_dec[u].astype(BF16))[0:1, :] for u in units}
    kcol = {}
    for u in units:
        k_hi, k_mid, k_lo = _split3(k8[u])
        kcol[u] = _dot_tn(k_hi, row0) + _dot_tn(k_mid, row0) + _dot_tn(k_lo, row0)
    s_new = {}
    for h, i in units:
        delta = (vv[h][i:i + 1, :] - kv[h, i]) * beta_bc[i:i + 1, sl(h)]
        s_new[h, i] = s_dec[h, i] + kcol[h, i] * delta
        s_out_ref[i, h] = s_new[h, i]
    for h, i in units:
        q8 = jnp.broadcast_to(qn[h][i:i + 1, :], (SUBLANE, GDN_DK))
        o_ref[i:i + 1, sl(h)] = _dot(q8.astype(BF16), s_new[h, i].astype(BF16))[0:1, :]
    gout = gout_ref[...]
    gz = gz_ref[...]
    for h in range(GDN_HEADS):
        sl = slice(h * GDN_DV, (h + 1) * GDN_DV)
        y_ref[:, sl] = _gdn_gate_out(o_ref[:, sl], gz[:, sl], gout).astype(BF16)


def _gdn_decode(qkv, cprev, misc, gz, s_in, convw, lanep, gout):
    n = qkv.shape[0]
    ns = GDN_DEC_SEQS
    state_spec = pl.BlockSpec((ns, GDN_HEADS, GDN_DK, GDN_DV), lambda i: (i, 0, 0, 0))
    return pl.pallas_call(
        _gdn_decode_body,
        grid=(n // ns,),
        in_specs=[
            pl.BlockSpec((ns, CONV_DIM), lambda i: (i, 0)),
            pl.BlockSpec((CONV_W - 1, ns, CONV_DIM), lambda i: (0, i, 0)),
            pl.BlockSpec((ns, LANE), lambda i: (i, 0)),
            pl.BlockSpec((ns, GDN_V), lambda i: (i, 0)),
            state_spec,
            _full(convw.shape), _full(lanep.shape), _full(gout.shape),
        ],
        out_specs=[pl.BlockSpec((ns, GDN_V), lambda i: (i, 0)), state_spec],
        out_shape=[
            jax.ShapeDtypeStruct((n, GDN_V), BF16),
            jax.ShapeDtypeStruct(s_in.shape, F32),
        ],
        scratch_shapes=[pltpu.VMEM((ns, GDN_V), F32)],
        compiler_params=_cparams(("parallel",)),
        name="gdn_decode",
    )(qkv, cprev, misc, gz, s_in, convw, lanep, gout)


def _rope_padded(x, cos, sina, sinb):
    return (x * cos + pltpu.roll(x, LANE - QK_ROPE // 2, axis=1) * sina
            + pltpu.roll(x, QK_ROPE // 2, axis=1) * sinb)


def _mla_queries(qa, wqb_ref, g_qa, g_nope, g_rope, cos, sina, sinb):
    q = _dot(_rms(qa, g_qa).astype(BF16), wqb_ref[...])
    out = []
    for h in range(MLA_HEADS):
        qn = _rms(q[:, h * QK_PAD:h * QK_PAD + QK_NOPE], g_nope)
        qp = _rms(q[:, h * QK_PAD + QK_NOPE:(h + 1) * QK_PAD], g_rope, QK_ROPE)
        out.append((qn, _rope_padded(qp, cos, sina, sinb)))
    return out


def _mla_latent(kvc, misc, g_kva, g_krope, cos, sina, sinb):
    c = _rms(kvc, g_kva)
    lane = _iota(misc.shape, 1)
    kp = _rms(jnp.where(lane < QK_ROPE, misc, 0.0), g_krope, QK_ROPE)
    return c, _rope_padded(kp, cos, sina, sinb)


def _mla_prep_body(qa_ref, kvc_ref, misc_ref, tab_ref, wqb_ref, wk_ref, wvt_ref,
                   g_qa_ref, g_nope_ref, g_rope_ref, g_kva_ref, g_knope_ref, g_krope_ref,
                   q_ref, k_ref, vt_ref, c_ref, kpe_ref):
    cos, sina, sinb = tab_ref[0], tab_ref[1], tab_ref[2]
    qs = _mla_queries(qa_ref[...], wqb_ref, g_qa_ref[...], g_nope_ref[...], g_rope_ref[...],
                      cos, sina, sinb)
    c, kp = _mla_latent(kvc_ref[...], misc_ref[...], g_kva_ref[...], g_krope_ref[...], cos, sina, sinb)
    c_ref[...] = c
    kpe_ref[...] = kp[:, :QK_ROPE]
    cb = c.astype(BF16)
    kn = _dot(cb, wk_ref[...])
    vt = _dot_nt(wvt_ref[...], cb)
    g_kn = g_knope_ref[...]
    kp16 = kp.astype(BF16)
    for h in range(MLA_HEADS):
        qn, qp = qs[h]
        q_ref[h, :, 0:QK_NOPE] = (qn * MLA_SCALE).astype(BF16)
        q_ref[h, :, QK_NOPE:QK_PAD] = (qp * MLA_SCALE).astype(BF16)
        k_ref[h, :, 0:QK_NOPE] = _rms(kn[:, h * QK_NOPE:(h + 1) * QK_NOPE], g_kn).astype(BF16)
        k_ref[h, :, QK_NOPE:QK_PAD] = kp16
        vt_ref[h] = vt[h * V_HEAD:(h + 1) * V_HEAD, :].astype(BF16)


def _mla_prep(qa, kvc, misc, tabs, wqb, wk, wvt, gains, tm):
    t = qa.shape[0]
    n_tab = tabs.shape[1] // tm
    row = lambda n: pl.BlockSpec((tm, n), lambda i: (i, 0))
    head = lambda n: pl.BlockSpec((MLA_HEADS, tm, n), lambda i: (0, i, 0))
    return pl.pallas_call(
        _mla_prep_body,
        grid=(t // tm,),
        in_specs=[row(Q_LORA), row(KV_LORA), row(LANE),
                  pl.BlockSpec((3, tm, LANE), lambda i: (0, i % n_tab, 0)),
                  _full(wqb.shape), _full(wk.shape), _full(wvt.shape)]
                 + [_full(g.shape) for g in gains],
        out_specs=[head(QK_PAD), head(QK_PAD),
                   pl.BlockSpec((MLA_HEADS, V_HEAD, tm), lambda i: (0, 0, i)),
                   row(KV_LORA), row(QK_ROPE)],
        out_shape=[
            jax.ShapeDtypeStruct((MLA_HEADS, t, QK_PAD), BF16),
            jax.ShapeDtypeStruct((MLA_HEADS, t, QK_PAD), BF16),
            jax.ShapeDtypeStruct((MLA_HEADS, V_HEAD, t), BF16),
            jax.ShapeDtypeStruct((t, KV_LORA), F32),
            jax.ShapeDtypeStruct((t, QK_ROPE), F32),
        ],
        compiler_params=_cparams(("parallel",)),
        name="mla_prep",
    )(qa, kvc, misc, tabs, wqb, wk, wvt, *gains)


FLASH_T = 512


FLASH_HEADS = 4


def _flash_body(q_ref, k_ref, vt_ref, o_ref):
    tq = FLASH_T
    qi = pl.program_id(2)
    heads = range(FLASH_HEADS)
    q = [q_ref[h] for h in heads]

    def block(j, carry, masked):
        start = pl.multiple_of(j * tq, tq)
        s = [_dot_nt(k_ref[h, pl.ds(start, tq), :], q[h]) for h in heads]
        if masked:
            keep = _iota(s[0].shape, 0) <= _iota(s[0].shape, 1)
            s = [jnp.where(keep, x, -jnp.inf) for x in s]
        m_new = [jnp.maximum(carry[h][0], jnp.max(s[h], axis=0, keepdims=True)) for h in heads]
        p = [jnp.exp(s[h] - m_new[h]) for h in heads]
        pv = [_dot(vt_ref[h, :, pl.ds(start, tq)], p[h].astype(BF16)) for h in heads]
        out = []
        for h in heads:
            m, l, acc = carry[h]
            corr = jnp.exp(m - m_new[h])
            out.append((m_new[h], l * corr + jnp.sum(p[h], axis=0, keepdims=True),
                        acc * corr + pv[h]))
        return tuple(out)

    init = tuple((jnp.full((1, tq), -jnp.inf, F32), jnp.zeros((1, tq), F32),
                  jnp.zeros((V_HEAD, tq), F32)) for _ in heads)
    carry = lax.fori_loop(0, qi, lambda j, c: block(j, c, False), init)
    carry = block(qi, carry, True)
    for h in heads:
        _, l, acc = carry[h]
        o_ref[h * V_HEAD:(h + 1) * V_HEAD, :] = (acc / l).astype(BF16)


def _mla_flash(q, k, vt, b, t):
    tq = FLASH_T
    nq = t // tq
    nh = FLASH_HEADS
    return pl.pallas_call(
        _flash_body,
        grid=(b, MLA_HEADS // nh, nq),
        in_specs=[
            pl.BlockSpec((nh, tq, QK_PAD), lambda i, h, j: (h, i * nq + j, 0)),
            pl.BlockSpec((nh, t, QK_PAD), lambda i, h, j: (h, i, 0)),
            pl.BlockSpec((nh, V_HEAD, t), lambda i, h, j: (h, 0, i)),
        ],
        out_specs=pl.BlockSpec((nh * V_HEAD, tq), lambda i, h, j: (h, i * nq + j)),
        out_shape=jax.ShapeDtypeStruct((MLA_HEADS * V_HEAD, b * t), BF16),
        compiler_params=_cparams(("parallel", "parallel", "arbitrary")),
        name="mla_flash",
    )(q, k, vt)


def _mla_dec_prep_body(qa_ref, kvc_ref, misc_ref, tab_ref, wqb_ref, wk_ref,
                       g_qa_ref, g_nope_ref, g_rope_ref, g_kva_ref, g_knope_ref, g_krope_ref,
                       qabs_ref, qpe_ref, c_ref, kpe_ref):
    cos, sina, sinb = tab_ref[0], tab_ref[1], tab_ref[2]
    qs = _mla_queries(qa_ref[...], wqb_ref, g_qa_ref[...], g_nope_ref[...], g_rope_ref[...],
                      cos, sina, sinb)
    c, kp = _mla_latent(kvc_ref[...], misc_ref[...], g_kva_ref[...], g_krope_ref[...], cos, sina, sinb)
    c_ref[...] = c
    kpe_ref[...] = kp
    g_kn = g_knope_ref[...]
    for h in range(MLA_HEADS):
        qn, qp = qs[h]
        qabs_ref[h] = _dot_nt((qn * g_kn * MLA_SCALE).astype(BF16),
                              wk_ref[:, h * QK_NOPE:(h + 1) * QK_NOPE])
        qpe_ref[h] = qp * MLA_SCALE


def _mla_dec_prep(qa, kvc, misc, tabs, wqb, wk, gains):
    n = qa.shape[0]
    args = (qa, kvc, misc, tabs, wqb, wk) + tuple(gains)
    return pl.pallas_call(
        _mla_dec_prep_body,
        grid=(1,),
        in_specs=[_full(a.shape) for a in args],
        out_specs=[_full((MLA_HEADS, n, KV_LORA)), _full((MLA_HEADS, n, LANE)),
                   _full((n, KV_LORA)), _full((n, LANE))],
        out_shape=[
            jax.ShapeDtypeStruct((MLA_HEADS, n, KV_LORA), F32),
            jax.ShapeDtypeStruct((MLA_HEADS, n, LANE), F32),
            jax.ShapeDtypeStruct((n, KV_LORA), F32),
            jax.ShapeDtypeStruct((n, LANE), F32),
        ],
        compiler_params=_cparams(("arbitrary",)),
        name="mla_dec_prep",
    )(*args)


def _mla_dec_attn_body(pt_ref, qabs_ref, qpe_ref, wkt_ref, ckv_hbm, kpe_hbm,
                       m_out, l_out, acc_out, lhs_ref, cbuf, kbuf, c16, sem_c, sem_k, *, blocks_per_seq):
    npg = PAGES_PER_STEP
    n_seq = qabs_ref.shape[0]
    total_blocks = n_seq * blocks_per_seq
    nk = MLA_HEADS * QK_NOPE
    keys = npg * PAGE_SIZE

    def block_copies(gb, page_of):
        slot = gb % DEC_RING
        cps = []
        for i in range(npg):
            page = page_of(gb * npg + i)
            cps.append(pltpu.make_async_copy(ckv_hbm.at[page], cbuf.at[slot, i], sem_c.at[slot]))
            cps.append(pltpu.make_async_copy(kpe_hbm.at[page], kbuf.at[slot, i], sem_k.at[slot]))
        return cps

    def start_block(gb):
        for n, cp in enumerate(block_copies(gb, lambda idx: pt_ref[idx])):
            cp.start(priority=n % 2)

    def wait_block(gb):
        for cp in block_copies(gb, lambda idx: 0):
            cp.wait()

    lhs_ref[0:nk, :] = wkt_ref[...]
    for gb in range(DEC_RING - 1):
        start_block(gb)

    def scores(gb):
        seq = gb // blocks_per_seq

        @pl.when(gb % blocks_per_seq == 0)
        def _():
            lhs_ref[nk:nk + BF16_ROWS, :] = qabs_ref[seq]

        slot = gb % DEC_RING
        c16[gb % 2] = cbuf[slot].reshape(keys, KV_LORA).astype(BF16)
        kpt = jnp.concatenate([kbuf[slot, i] for i in range(npg)], axis=1).astype(BF16)
        kt = _dot_nt(lhs_ref[...], c16[gb % 2])
        ssq = [jnp.sum(jnp.square(kt[h * QK_NOPE:(h + 1) * QK_NOPE]), axis=0, keepdims=True)
               for h in range(MLA_HEADS)]
        ssq = jnp.concatenate(ssq + [jnp.ones((SUBLANE - MLA_HEADS, keys), F32)], axis=0)
        pe = _dot(qpe_ref[seq], kpt)
        return kt[nk:nk + SUBLANE] * lax.rsqrt(ssq * (1.0 / QK_NOPE) + EPS) + pe[0:SUBLANE]

    def accumulate(gb, s, carry):
        m_old, l_old, acc_old = carry
        m_new = jnp.maximum(m_old, jnp.max(s, axis=-1, keepdims=True))
        p = jnp.exp(s - m_new)
        corr = jnp.exp(m_old - m_new)
        l_new = l_old * corr + jnp.sum(p, axis=-1, keepdims=True)
        acc_new = acc_old * corr + _dot(p.astype(BF16), c16[gb % 2])
        return m_new, l_new, acc_new

    def finish(seq, state):
        m_fin, l_fin, acc_fin = state
        m_out[seq] = jnp.broadcast_to(m_fin, m_out.shape[1:])
        l_out[seq] = jnp.broadcast_to(l_fin, l_out.shape[1:])
        acc_out[seq] = acc_fin

    def fetch_and_score(gb, after):
        nxt = gb + (DEC_RING - 1)

        @pl.when(nxt < total_blocks)
        def _():
            start_block(nxt)

        wait_block(gb)
        s_cur = scores(gb)
        return s_cur, after()

    init = (jnp.full((SUBLANE, 1), -jnp.inf, F32), jnp.zeros((SUBLANE, 1), F32),
            jnp.zeros((SUBLANE, KV_LORA), F32))

    def body(gb, carry):
        s_prev, state = carry
        s_cur, state = fetch_and_score(gb, lambda: accumulate(gb - 1, s_prev, state))
        closes_seq = gb % blocks_per_seq == 0

        @pl.when(closes_seq)
        def _():
            finish(gb // blocks_per_seq - 1, state)

        return s_cur, tuple(jnp.where(closes_seq, i, x) for i, x in zip(init, state))

    s_last, state = lax.fori_loop(1, total_blocks, body, fetch_and_score(0, lambda: init))
    finish(n_seq - 1, accumulate(total_blocks - 1, s_last, state))


def _mla_dec_attn(page_table, ckv_pool, kpe_pool, qabs, qpe, wkt):
    n, n_pages = page_table.shape
    npg = PAGES_PER_STEP
    nk = MLA_HEADS * QK_NOPE

    assert n_pages % npg == 0 and n * (n_pages // npg) >= DEC_RING
    whole = lambda shape: pl.BlockSpec(shape, lambda i, pt: (0,) * len(shape))
    grid_spec = pltpu.PrefetchScalarGridSpec(
        num_scalar_prefetch=1,
        grid=(1,),
        in_specs=[whole(qabs.shape), whole(qpe.shape), whole(wkt.shape),
                  pl.BlockSpec(memory_space=pl.ANY), pl.BlockSpec(memory_space=pl.ANY)],
        out_specs=[whole((n, SUBLANE, LANE)), whole((n, SUBLANE, LANE)), whole((n, SUBLANE, KV_LORA))],
        scratch_shapes=[
            pltpu.VMEM((nk + BF16_ROWS, KV_LORA), BF16),
            pltpu.VMEM((DEC_RING, npg, PAGE_SIZE, KV_LORA), F32),
            pltpu.VMEM((DEC_RING, npg, QK_ROPE, PAGE_SIZE), F32),
            pltpu.VMEM((2, npg * PAGE_SIZE, KV_LORA), BF16),
            pltpu.SemaphoreType.DMA((DEC_RING,)),
            pltpu.SemaphoreType.DMA((DEC_RING,)),
        ],
    )
    return pl.pallas_call(
        functools.partial(_mla_dec_attn_body, blocks_per_seq=n_pages // npg),
        grid_spec=grid_spec,
        out_shape=[
            jax.ShapeDtypeStruct((n, SUBLANE, LANE), F32),
            jax.ShapeDtypeStruct((n, SUBLANE, LANE), F32),
            jax.ShapeDtypeStruct((n, SUBLANE, KV_LORA), F32),
        ],
        compiler_params=_cparams(("arbitrary",)),
        name="mla_dec_attn",
    )(page_table.reshape(-1), qabs, qpe, wkt, ckv_pool, kpe_pool)


def _mla_dec_finish_body(m_ref, l_ref, acc_ref, qabs_ref, qpe_ref, c_ref, kpe_ref, wk_ref, wv_ref,
                         y_ref):
    c = c_ref[...]
    kp = kpe_ref[...]
    cb = c.astype(BF16)
    kn = _dot(cb, wk_ref[...])
    for h in range(MLA_HEADS):
        knh = kn[:, h * QK_NOPE:(h + 1) * QK_NOPE]
        r = lax.rsqrt(jnp.sum(knh * knh, axis=-1, keepdims=True) * (1.0 / QK_NOPE) + EPS)
        s_new = (r * jnp.sum(qabs_ref[h] * c, axis=-1, keepdims=True)
                 + jnp.sum(qpe_ref[h] * kp, axis=-1, keepdims=True))
        m_old = m_ref[h][:, 0:1]
        m_new = jnp.maximum(m_old, s_new)
        p = jnp.exp(s_new - m_new)
        corr = jnp.exp(m_old - m_new)
        l = l_ref[h][:, 0:1] * corr + p
        lat = (acc_ref[h] * corr + p * c) / l
        y_ref[:, h * V_HEAD:(h + 1) * V_HEAD] = _dot(
            lat.astype(BF16), wv_ref[:, h * V_HEAD:(h + 1) * V_HEAD]).astype(BF16)


def _mla_dec_finish(m, l, acc, qabs, qpe, c, kpe, wk, wv):
    n = c.shape[0]
    args = (m, l, acc, qabs, qpe, c, kpe, wk, wv)
    return pl.pallas_call(
        _mla_dec_finish_body,
        grid=(1,),
        in_specs=[_full(a.shape) for a in args],
        out_specs=_full((n, MLA_HEADS * V_HEAD)),
        out_shape=jax.ShapeDtypeStruct((n, MLA_HEADS * V_HEAD), BF16),
        compiler_params=_cparams(("arbitrary",)),
        name="mla_dec_finish",
    )(*args)


def _mix_q_body(x_ref, yg_ref, ym_ref, wmix_ref, g_mem_ref, wq_ref, g_qn_ref, x1_ref, q_ref, *,
                mla_transposed):
    mla_dot = _dot_tn if mla_transposed else _dot
    x1 = (x_ref[...] + _dot(yg_ref[...], wmix_ref[0:GDN_V, :])
          + mla_dot(ym_ref[...], wmix_ref[GDN_V:, :]))
    x1_ref[...] = x1
    q = _dot(_rms(x1, g_mem_ref[...]).astype(BF16), wq_ref[...])
    g_qn = g_qn_ref[...]
    for h in range(MEM_HEADS):
        sl = slice(h * MEM_HD, (h + 1) * MEM_HD)
        q_ref[:, sl] = (_rms(q[:, sl], g_qn) * (MEM_HD ** -0.5)).astype(BF16)


def _mix_q(x, yg, ym, wmix, g_mem, wq, g_qn, tm, mla_transposed=False):
    t = x.shape[0]
    row = lambda n: pl.BlockSpec((tm, n), lambda i: (i, 0))
    mla_width = MLA_HEADS * V_HEAD
    ym_spec = pl.BlockSpec((mla_width, tm), lambda i: (0, i)) if mla_transposed else row(mla_width)
    return pl.pallas_call(
        functools.partial(_mix_q_body, mla_transposed=mla_transposed),
        grid=(t // tm,),
        in_specs=[row(D_MODEL), row(GDN_V), ym_spec, _full(wmix.shape),
                  _full(g_mem.shape), _full(wq.shape), _full(g_qn.shape)],
        out_specs=[row(D_MODEL), row(MEM_DIM)],
        out_shape=[jax.ShapeDtypeStruct((t, D_MODEL), F32), jax.ShapeDtypeStruct((t, MEM_DIM), BF16)],
        compiler_params=_cparams(("parallel",)),
        name="mix_q",
    )(x, yg, ym, wmix, g_mem, wq, g_qn)


def _mem_attn_body(q_ref, k_ref, v_ref, o_ref):
    q = q_ref[0]
    heads = range(MEM_HEADS)
    sls = [slice(h * MEM_HD, (h + 1) * MEM_HD) for h in heads]
    s = [_dot_nt(q[:, sls[h]], k_ref[0, :, sls[h]].astype(BF16)) for h in heads]
    p = [jnp.exp(s[h] - jnp.max(s[h], axis=-1, keepdims=True)) for h in heads]
    o = [_dot(p[h].astype(BF16), v_ref[0, :, sls[h]].astype(BF16)) for h in heads]
    for h in heads:
        o_ref[0, :, sls[h]] = (o[h] / jnp.sum(p[h], axis=-1, keepdims=True)).astype(BF16)


def _mem_attn(q, k, v, tq):
    nb, t, _ = q.shape
    m = k.shape[1]
    return pl.pallas_call(
        _mem_attn_body,
        grid=(nb, t // tq),
        in_specs=[pl.BlockSpec((1, tq, MEM_DIM), lambda i, j: (i, j, 0)),
                  pl.BlockSpec((1, m, MEM_DIM), lambda i, j: (i, 0, 0)),
                  pl.BlockSpec((1, m, MEM_DIM), lambda i, j: (i, 0, 0))],
        out_specs=pl.BlockSpec((1, tq, MEM_DIM), lambda i, j: (i, j, 0)),
        out_shape=jax.ShapeDtypeStruct((nb, t, MEM_DIM), BF16),
        compiler_params=_cparams(("parallel", "arbitrary")),
        name="mem_attn",
    )(q, k, v)


def _mem_attn_dec_body(q_ref, k_ref, v_ref, o_ref):
    for i in range(q_ref.shape[0]):
        q = q_ref[i:i + 1]
        s = jnp.sum(k_ref[i] * q, axis=-1, keepdims=True)
        p = jnp.exp(s - jnp.max(s, axis=0, keepdims=True))
        o_ref[i:i + 1] = (jnp.sum(p * v_ref[i], axis=0, keepdims=True)
                          / jnp.sum(p, axis=0, keepdims=True))


MEM_DEC_SEQS = 4


def _mem_attn_dec(q, k, v):
    n, m, nh, hd = k.shape
    ns = MEM_DEC_SEQS
    kv_spec = pl.BlockSpec((ns, m, nh, hd), lambda i: (i, 0, 0, 0))
    return pl.pallas_call(
        _mem_attn_dec_body,
        grid=(n // ns,),
        in_specs=[pl.BlockSpec((ns, nh, hd), lambda i: (i, 0, 0)), kv_spec, kv_spec],
        out_specs=pl.BlockSpec((ns, nh, hd), lambda i: (i, 0, 0)),
        out_shape=jax.ShapeDtypeStruct((n, nh, hd), F32),
        compiler_params=_cparams(("parallel",)),
        name="mem_attn_dec",
    )(q, k, v)


def _out_ffn_body(x1_ref, o_ref, wo_ref, g_ffn_ref, wg_ref, wu_ref, wd_ref, y_ref):
    x2 = x1_ref[...] + _dot(o_ref[...], wo_ref[...])
    h = _rms(x2, g_ffn_ref[...]).astype(BF16)
    act = (_silu(_dot(h, wg_ref[...])) * _dot(h, wu_ref[...])).astype(BF16)
    y_ref[...] = x2 + _dot(act, wd_ref[...])


def _out_ffn(x1, o, wo, g_ffn, wg, wu, wd, tm):
    t = x1.shape[0]
    row = lambda n: pl.BlockSpec((tm, n), lambda i: (i, 0))
    const = lambda a: pl.BlockSpec(a.shape, lambda i: (0,) * a.ndim, pipeline_mode=pl.Buffered(1))
    return pl.pallas_call(
        _out_ffn_body,
        grid=(t // tm,),
        in_specs=[row(D_MODEL), row(MEM_DIM), const(wo), const(g_ffn), const(wg), const(wu), const(wd)],
        out_specs=row(D_MODEL),
        out_shape=jax.ShapeDtypeStruct((t, D_MODEL), F32),
        compiler_params=_cparams(("parallel",)),
        name="out_ffn",
    )(x1, o, wo, g_ffn, wg, wu, wd)


def _mem_kv_body(mem_ref, wk_ref, wv_ref, g_ref, k_ref, v_ref):
    mb = mem_ref[...].astype(BF16)
    k = _dot(mb, wk_ref[...])
    g = g_ref[...]
    for h in range(MEM_HEADS):
        sl = slice(h * MEM_HD, (h + 1) * MEM_HD)
        k_ref[:, sl] = _rms(k[:, sl], g)
    v_ref[...] = _dot(mb, wv_ref[...])


def _mem_kv(mem, wk, wv, g, tm):
    t = mem.shape[0]
    row = lambda n: pl.BlockSpec((tm, n), lambda i: (i, 0))
    return pl.pallas_call(
        _mem_kv_body,
        grid=(t // tm,),
        in_specs=[row(D_MODEL), _full(wk.shape), _full(wv.shape), _full(g.shape)],
        out_specs=[row(MEM_DIM), row(MEM_DIM)],
        out_shape=[jax.ShapeDtypeStruct((t, MEM_DIM), F32)] * 2,
        compiler_params=_cparams(("parallel",)),
        name="mem_kv",
    )(mem, wk, wv, g)


def _row(v):
    return v.reshape(1, -1).astype(F32)


def _pad_lanes(v, width=LANE):
    return jnp.pad(v, ((0, 0), (0, width - v.shape[1])))


def _rope_tables(pos, rows):
    half = QK_ROPE // 2
    inv = ROPE_THETA ** (-jnp.arange(half, dtype=F32) / half)
    ang = pos.astype(F32)[:, None] * inv[None, :]
    cos, sin = jnp.cos(ang), jnp.sin(ang)
    zero = jnp.zeros_like(cos)
    tabs = jnp.stack([
        _pad_lanes(jnp.concatenate([cos, cos], axis=1)),
        _pad_lanes(jnp.concatenate([-sin, zero], axis=1)),
        _pad_lanes(jnp.concatenate([zero, sin], axis=1)),
    ])
    return jnp.broadcast_to(tabs, (3, rows, LANE)) if tabs.shape[1] == 1 else tabs


def kernel(x_prompt, x_sample, cache_mla_ckv, cache_mla_kpe, cache_mem_k, cache_mem_v, state_gdn_S, state_gdn_conv, page_table, mem_prompt, norm_mix_g, w_in, gdn_conv_w, gdn_A_log, gdn_dt_bias, gdn_out_norm_g, mla_q_a_norm_g, mla_w_q_b, mla_kv_a_norm_g, mla_w_kv_b, mla_qn_nope_g, mla_qn_rope_g, mla_kn_nope_g, mla_kn_rope_g, w_mix_out, norm_mem_g, mem_wq, mem_wk, mem_wv, mem_wo, mem_qn_g, mem_kn_g, norm_ffn_g, ffn_w_gate, ffn_w_up, ffn_w_down):
    depth = w_in.shape[0]
    assert depth == 1, "single-layer trunk"
    bsz, seq, _ = x_prompt.shape
    nseq, dseq, _ = x_sample.shape
    assert dseq == 1, "one new token per decode sequence"
    past_len = page_table.shape[1] * PAGE_SIZE
    n_tok = bsz * seq

    w = w_in[0]
    o_gz = CONV_DIM
    o_b = o_gz + GDN_V
    o_a = o_b + GDN_HEADS
    o_qa = o_a + GDN_HEADS
    o_c = o_qa + Q_LORA
    o_kpe = o_c + KV_LORA
    misc_w = _pad_lanes(jnp.concatenate([w[:, o_kpe:o_kpe + QK_ROPE], w[:, o_b:o_qa]], axis=1))
    w_in_p = jnp.concatenate([w[:, :o_b], w[:, o_qa:o_kpe], misc_w], axis=1).astype(BF16)

    wqb = mla_w_q_b[0].reshape(Q_LORA, MLA_HEADS, QK_NOPE + QK_ROPE)
    wqb = jnp.pad(wqb, ((0, 0), (0, 0), (0, QK_PAD - QK_NOPE - QK_ROPE)))
    wqb = wqb.reshape(Q_LORA, MLA_HEADS * QK_PAD).astype(BF16)
    wkvb = mla_w_kv_b[0].reshape(KV_LORA, MLA_HEADS, QK_NOPE + V_HEAD)
    wk = wkvb[:, :, :QK_NOPE].reshape(KV_LORA, MLA_HEADS * QK_NOPE).astype(BF16)
    wv = wkvb[:, :, QK_NOPE:].reshape(KV_LORA, MLA_HEADS * V_HEAD).astype(BF16)
    wkt = wk.T

    lanep = jnp.zeros((2, LANE), F32)
    lanep = lanep.at[0, MISC_A:MISC_A + GDN_HEADS].set(gdn_A_log[0])
    lanep = lanep.at[1, MISC_A:MISC_A + GDN_HEADS].set(gdn_dt_bias[0])
    rowp = jnp.zeros((2 * SUBLANE,), F32)
    rowp = rowp.at[GDN_HEADS:2 * GDN_HEADS].set(gdn_A_log[0])
    rowp = rowp.at[SUBLANE + GDN_HEADS:SUBLANE + 2 * GDN_HEADS].set(gdn_dt_bias[0])
    rowp = jnp.broadcast_to(rowp[:, None], (2 * SUBLANE, GDN_TB))

    g_mix = _row(norm_mix_g[0])
    g_out = _row(gdn_out_norm_g[0])
    mla_gains = (_row(mla_q_a_norm_g[0]), _row(mla_qn_nope_g[0]), _pad_lanes(_row(mla_qn_rope_g[0])),
                 _row(mla_kv_a_norm_g[0]), _row(mla_kn_nope_g[0]), _pad_lanes(_row(mla_kn_rope_g[0])))
    wmix = w_mix_out[0].astype(BF16)
    wq_mem = mem_wq[0].astype(BF16)
    wk_mem = mem_wk[0].astype(BF16)
    wv_mem = mem_wv[0].astype(BF16)
    wo_mem = mem_wo[0].astype(BF16)
    wg = ffn_w_gate[0].astype(BF16)
    wu = ffn_w_up[0].astype(BF16)
    wd = ffn_w_down[0].astype(BF16)
    g_mem = _row(norm_mem_g[0])
    g_ffn = _row(norm_ffn_g[0])
    g_qn = _row(mem_qn_g[0])
    g_kn = _row(mem_kn_g[0])
    conv_w = gdn_conv_w[0]

    xp = x_prompt.reshape(n_tok, D_MODEL)
    tiles_per_seq = seq // TOKEN_TILE
    qkv_act, qkv_tail, gz, qa, kvc, misc = _in_proj_conv(xp, g_mix, w_in_p, conv_w, TOKEN_TILE,
                                                         tiles_per_seq)
    grow = jnp.swapaxes(misc.reshape(bsz, seq, LANE)[:, :, MISC_B:MISC_B + SUBLANE], 1, 2)
    y_gdn, p_s = _gdn_prompt(qkv_act.reshape(bsz, seq, CONV_DIM), misc.reshape(bsz, seq, LANE), grow,
                             gz.reshape(bsz, seq, GDN_V), lanep, rowp, g_out)
    p_conv = qkv_tail.reshape(bsz, tiles_per_seq, SUBLANE, CONV_DIM)[:, -1, SUBLANE - (CONV_W - 1):, :]

    tabs_p = _rope_tables(jnp.arange(seq), seq)
    q_full, k_full, vt_full, p_c, p_kpe = _mla_prep(qa, kvc, misc, tabs_p, wqb, wk, wv.T, mla_gains,
                                                    TOKEN_TILE)
    y_mla_t = _mla_flash(q_full, k_full, vt_full, bsz, seq)

    mem_k, mem_v = _mem_kv(mem_prompt.reshape(-1, D_MODEL), wk_mem, wv_mem, g_kn, TOKEN_TILE)
    n_mem = mem_prompt.shape[1]
    x1, q_mem = _mix_q(xp, y_gdn.reshape(n_tok, GDN_V), y_mla_t, wmix, g_mem, wq_mem, g_qn, MIX_TILE,
                       mla_transposed=True)
    o_mem = _mem_attn(q_mem.reshape(bsz, seq, MEM_DIM), mem_k.reshape(bsz, n_mem, MEM_DIM),
                      mem_v.reshape(bsz, n_mem, MEM_DIM), TOKEN_TILE)
    y_prompt = _out_ffn(x1, o_mem.reshape(n_tok, MEM_DIM), wo_mem, g_ffn, wg, wu, wd, TOKEN_TILE)

    xs = x_sample.reshape(nseq, D_MODEL)
    qkv_s, gz_s, qa_s, kvc_s, misc_s = _in_proj(xs, g_mix, w_in_p, nseq)
    conv_prev = state_gdn_conv[0]
    y_gdn_s, s_new = _gdn_decode(qkv_s, jnp.swapaxes(conv_prev, 0, 1), misc_s, gz_s, state_gdn_S[0],
                                 conv_w, lanep, g_out)
    s_conv = jnp.concatenate([conv_prev[:, 1:, :], qkv_s[:, None, :]], axis=1)

    tabs_s = _rope_tables(jnp.full((1,), past_len), nseq)
    qabs, qpe, c_new, kpe_new = _mla_dec_prep(qa_s, kvc_s, misc_s, tabs_s, wqb, wk, mla_gains)
    pad_rows = lambda a: jnp.pad(jnp.swapaxes(a, 0, 1), ((0, 0), (0, BF16_ROWS - MLA_HEADS), (0, 0)))
    m_run, l_run, acc_run = _mla_dec_attn(
        page_table, cache_mla_ckv[0], jnp.swapaxes(cache_mla_kpe[0], 1, 2),
        pad_rows(qabs).astype(BF16), pad_rows(qpe[:, :, :QK_ROPE]).astype(BF16), wkt)
    heads_first = lambda a: jnp.swapaxes(a[:, :MLA_HEADS], 0, 1)
    y_mla_s = _mla_dec_finish(heads_first(m_run), heads_first(l_run), heads_first(acc_run),
                              qabs, qpe, c_new, kpe_new, wk, wv)

    x1_s, q_mem_s = _mix_q(xs, y_gdn_s, y_mla_s, wmix, g_mem, wq_mem, g_qn, nseq)
    o_mem_s = _mem_attn_dec(q_mem_s.astype(F32).reshape(nseq, MEM_HEADS, MEM_HD),
                            cache_mem_k[0], cache_mem_v[0])
    y_sample = _out_ffn(x1_s, o_mem_s.reshape(nseq, MEM_DIM).astype(BF16),
                        wo_mem, g_ffn, wg, wu, wd, nseq)

    return (
        y_prompt.reshape(bsz, seq, D_MODEL),
        y_sample.reshape(nseq, 1, D_MODEL),
        p_c.reshape(1, bsz, seq, KV_LORA),
        p_kpe.reshape(1, bsz, seq, QK_ROPE),
        mem_k.reshape(1, bsz, n_mem, MEM_HEADS, MEM_HD),
        mem_v.reshape(1, bsz, n_mem, MEM_HEADS, MEM_HD),
        p_s[None],
        p_conv[None],
        c_new.reshape(1, nseq, 1, KV_LORA),
        kpe_new[:, :QK_ROPE].reshape(1, nseq, 1, QK_ROPE),
        s_new[None],
        s_conv[None],
    )
```
